```python
import jax, jax.numpy as jnp
from jax import lax
import numpy as np

D_MODEL = 1024
BATCH = 8
SEQ = 4096
DEPTH = 1

CHUNK = 64
N_META = 16
Q_BLOCK = 128
N_PAD = Q_BLOCK - N_META
PREFIX = N_PAD + N_META
MIX_WIDTH = D_MODEL
RET_HEADS = 4
RET_DV = (MIX_WIDTH // 2) // RET_HEADS
RET_DK = RET_DV // 2
FOX_HEADS = 8
FOX_DH = (MIX_WIDTH // 2) // FOX_HEADS
D_FF = 2816
CONV_W = 3
ROPE_BASE = 10000.0
EPS = 1e-6
NEG = -1e30

RET_QK = RET_HEADS * RET_DK
RET_V = RET_HEADS * RET_DV
FOX_W = FOX_HEADS * FOX_DH
SPLIT_POINTS = (RET_QK, 2 * RET_QK, 2 * RET_QK + RET_V, 2 * RET_QK + 2 * RET_V,
                2 * RET_QK + 2 * RET_V + FOX_W, 2 * RET_QK + 2 * RET_V + 2 * FOX_W,
                2 * RET_QK + 2 * RET_V + 3 * FOX_W)
IN_WIDTH = 2 * RET_QK + 2 * RET_V + 3 * FOX_W + FOX_HEADS

kernel_name = "hymba_retention_fox_convffn_block"


def rms_norm(x, g):
    xf = x.astype(jnp.float32)
    y = xf * lax.rsqrt(jnp.mean(xf * xf, axis=-1, keepdims=True) + EPS)
    return (y * g.astype(jnp.float32)).astype(x.dtype)


def rotary(x, pos):
    half = x.shape[-1] // 2
    inv = 1.0 / (ROPE_BASE ** (jnp.arange(half, dtype=jnp.float32) / half))
    ang = pos.astype(jnp.float32)[:, None] * inv[None, :]
    cos = jnp.cos(ang)[None, :, None, :]
    sin = jnp.sin(ang)[None, :, None, :]
    x1 = x[..., :half].astype(jnp.float32)
    x2 = x[..., half:].astype(jnp.float32)
    return jnp.concatenate([x1 * cos - x2 * sin, x1 * sin + x2 * cos], axis=-1)


def retention(q, k, v, valid):
    B, L, H, dk = q.shape
    dv = v.shape[-1]
    nc = L // CHUNK
    f32 = jnp.float32
    log_g = jnp.log1p(-jnp.exp2(-5.0 - jnp.arange(H, dtype=f32)))
    pos = jnp.arange(L)
    qr = rotary(q, pos)
    kr = rotary(k, pos) * (dk ** -0.5) * valid.astype(f32)[None, :, None, None]
    qc = qr.reshape(B, nc, CHUNK, H, dk)
    kc = kr.reshape(B, nc, CHUNK, H, dk)
    vc = v.astype(f32).reshape(B, nc, CHUNK, H, dv)
    n = jnp.arange(CHUNK, dtype=f32)
    d_intra = jnp.exp(jnp.abs(n[:, None] - n[None, :])[None] * log_g[:, None, None])
    s = jnp.einsum('bcnhd,bcmhd->bchnm', qc, kc) * d_intra[None, None]
    intra = jnp.einsum('bchnm,bcmhe->bcnhe', s, vc)
    w_k = jnp.exp((CHUNK - 1.0 - n)[:, None] * log_g[None, :])
    u = jnp.einsum('bcmhd,bcmhe->bchde', kc * w_k[None, None, :, :, None], vc)
    g_chunk = jnp.exp(CHUNK * log_g)[None, :, None, None]

    def step(r, u_i):
        return g_chunk * r + u_i, r

    _, r_prev = lax.scan(step, jnp.zeros((B, H, dk, dv), f32), jnp.moveaxis(u, 1, 0))
    r_prev = jnp.moveaxis(r_prev, 0, 1)
    w_q = jnp.exp((n + 1.0)[:, None] * log_g[None, :])
    inter = jnp.einsum('bcnhd,bchde->bcnhe', qc * w_q[None, None, :, :, None], r_prev)
    return (intra + inter).reshape(B, L, H, dv)


def forgetting_attention(q, k, v, log_f, valid):
    B, L, H, dh = q.shape
    f32 = jnp.float32
    scale = dh ** -0.5
    c = jnp.cumsum(log_f.astype(f32), axis=1).transpose(0, 2, 1)
    qf, kf, vf = q.astype(f32), k.astype(f32), v.astype(f32)
    pos = jnp.arange(L)
    outs = []
    for blk in range(L // Q_BLOCK):
        q0, q1 = blk * Q_BLOCK, (blk + 1) * Q_BLOCK
        logits = jnp.einsum('bqhd,bkhd->bhqk', qf[:, q0:q1], kf[:, :q1]) * scale
        bias = c[:, :, q0:q1, None] - c[:, :, None, :q1]
        mask = (pos[None, :q1] <= pos[q0:q1, None]) & valid[None, :q1]
        logits = jnp.where(mask[None, None], logits + bias, NEG)
        p = jax.nn.softmax(logits, axis=-1)
        outs.append(jnp.einsum('bhqk,bkhd->bqhd', p, vf[:, :q1]))
    return jnp.concatenate(outs, axis=1)


def hybrid_mixer(h, w_in, forget_b, ret_norm_g, w_out, valid):
    B, L, _ = h.shape
    proj = jnp.einsum('bld,de->ble', h, w_in)
    rq, rk, rv, rg, fq, fk, fv, ff = jnp.split(proj, SPLIT_POINTS, axis=-1)
    o_r = retention(rq.reshape(B, L, RET_HEADS, RET_DK), rk.reshape(B, L, RET_HEADS, RET_DK),
                    rv.reshape(B, L, RET_HEADS, RET_DV), valid)
    o_r = o_r * lax.rsqrt(jnp.mean(o_r * o_r, axis=-1, keepdims=True) + EPS)
    o_r = o_r.reshape(B, L, RET_V) * ret_norm_g.astype(jnp.float32) * jax.nn.silu(rg.astype(jnp.float32))
    log_f = jax.nn.log_sigmoid(ff.astype(jnp.float32) + forget_b.astype(jnp.float32))
    o_f = forgetting_attention(fq.reshape(B, L, FOX_HEADS, FOX_DH), fk.reshape(B, L, FOX_HEADS, FOX_DH),
                               fv.reshape(B, L, FOX_HEADS, FOX_DH), log_f, valid).reshape(B, L, FOX_W)
    mixed = jnp.concatenate([o_r, o_f], axis=-1).astype(h.dtype)
    return jnp.einsum('ble,ed->bld', mixed, w_out)


def conv_ffn(h, w_up, conv_w, conv_b, w_down, valid):
    L = h.shape[1]
    up = jnp.einsum('bld,df->blf', h, w_up)
    a, b = jnp.split(up, 2, axis=-1)
    a = a * valid.astype(a.dtype)[None, :, None]
    a_pad = jnp.pad(a, ((0, 0), (CONV_W - 1, 0), (0, 0)))
    acc = conv_b
    for j in range(CONV_W):
        acc = acc + a_pad[:, j:j + L] * conv_w[j]
    return jnp.einsum('blf,fd->bld', jax.nn.silu(acc) * b, w_down)


def setup_inputs(seed: int = 0) -> dict:
    key = jax.random.key(seed)
    ks = jax.random.split(key, 13)
    f32 = jnp.float32
    return {
        "x": jax.random.normal(ks[0], (BATCH, SEQ, D_MODEL), f32),
        "meta_tokens": jax.random.normal(ks[1], (N_META, D_MODEL), f32),
        "attn_norm_g": 1.0 + 0.02 * jax.random.normal(ks[2], (DEPTH, D_MODEL), f32),
        "w_in": jax.random.normal(ks[3], (DEPTH, D_MODEL, IN_WIDTH), f32) * D_MODEL ** -0.5,
        "fox_forget_b": jax.random.uniform(ks[4], (DEPTH, FOX_HEADS), f32, 1.0, 5.0),
        "ret_norm_g": 1.0 + 0.02 * jax.random.normal(ks[5], (DEPTH, RET_V), f32),
        "w_out": jax.random.normal(ks[6], (DEPTH, MIX_WIDTH, D_MODEL), f32) * MIX_WIDTH ** -0.5,
        "ffn_norm_g": 1.0 + 0.02 * jax.random.normal(ks[7], (DEPTH, D_MODEL), f32),
        "w_up": jax.random.normal(ks[8], (DEPTH, D_MODEL, 2 * D_FF), f32) * D_MODEL ** -0.5,
        "conv_w": jax.random.normal(ks[9], (DEPTH, CONV_W, D_FF), f32) * CONV_W ** -0.5,
        "conv_b": 0.02 * jax.random.normal(ks[10], (DEPTH, D_FF), f32),
        "w_down": jax.random.normal(ks[11], (DEPTH, D_FF, D_MODEL), f32) * D_FF ** -0.5,
        "final_norm_g": 1.0 + 0.02 * jax.random.normal(ks[12], (D_MODEL,), f32),
    }


def reference(x, meta_tokens, attn_norm_g, w_in, fox_forget_b, ret_norm_g, w_out,
              ffn_norm_g, w_up, conv_w, conv_b, w_down, final_norm_g):
    B = x.shape[0]
    pad = jnp.zeros((B, N_PAD, D_MODEL), x.dtype)
    meta = jnp.broadcast_to(meta_tokens.astype(x.dtype)[None], (B, N_META, D_MODEL))
    h = jnp.concatenate([pad, meta, x], axis=1)
    valid = jnp.arange(h.shape[1]) >= N_PAD
    for layer in range(DEPTH):
        h = h + hybrid_mixer(rms_norm(h, attn_norm_g[layer]), w_in[layer], fox_forget_b[layer],
                             ret_norm_g[layer], w_out[layer], valid)
        h = h + conv_ffn(rms_norm(h, ffn_norm_g[layer]), w_up[layer], conv_w[layer], conv_b[layer],
                         w_down[layer], valid)
    return rms_norm(h, final_norm_g)[:, PREFIX:]
```

```python
import functools

import jax
import jax.numpy as jnp
from jax import lax
from jax.experimental import pallas as pl
from jax.experimental.pallas import tpu as pltpu

F32 = jnp.float32
BF16 = jnp.bfloat16

D_MODEL = 1024
CHUNK = 64
N_META = 16
PREFIX = 128
N_PAD = PREFIX - N_META
RET_HEADS = 4
RET_DK = 64
RET_DV = 128
FOX_HEADS = 8
FOX_DH = 64
D_FF = 2816
CONV_W = 3
ROPE_BASE = 10000.0
EPS = 1e-6
NEG = -1e30

RET_QK = RET_HEADS * RET_DK
RET_V = RET_HEADS * RET_DV
FOX_W = FOX_HEADS * FOX_DH
MAIN_W = 2 * RET_QK + 2 * RET_V + 3 * FOX_W
LANES = 128
SUBLANES = 8
N_EXTRA = 7
VMEM_LIMIT = 56 * 1024 * 1024
FF_CHUNK = 256


def _rms(x):
    return lax.rsqrt(jnp.mean(x * x, axis=-1, keepdims=True) + EPS)


def _split3(c):
    hi = c.astype(BF16).astype(F32)
    r = c - hi
    mid = r.astype(BF16).astype(F32)
    return hi, mid, r - mid


def _inproj_kernel(x_ref, g_ref, wm_ref, wf_ref, fb_ref, cq_ref, sq_ref, ck_ref, sk_ref, c0_ref,
                   rq_ref, rk_ref, rv_ref, rg_ref, fq_ref, fk_ref, fv_ref, cl_ref, carry_ref,
                   *, tl, n_invalid):
    j = pl.program_id(1)

    @pl.when(j == 0)
    def _():
        carry_ref[...] = c0_ref[...]

    x = x_ref[0]
    hn = ((x * _rms(x)) * g_ref[...]).astype(BF16)

    def proj(lo, width):
        return jnp.dot(hn, wm_ref[:, lo:lo + width], preferred_element_type=F32)

    lane = lax.broadcasted_iota(jnp.int32, (tl, LANES), 1)
    first_half = (lane & 32) == 0

    for base, c_ref, s_ref, o_ref in ((0, cq_ref, sq_ref, rq_ref), (RET_QK, ck_ref, sk_ref, rk_ref)):
        for blk in range(RET_QK // LANES):
            xb = proj(base + LANES * blk, LANES)
            swapped = jnp.where(first_half, pltpu.roll(xb, LANES - 32, 1), pltpu.roll(xb, 32, 1))
            o_ref[0, :, LANES * blk:LANES * (blk + 1)] = (xb * c_ref[...] + swapped * s_ref[...]).astype(BF16)

    rv_ref[0] = proj(2 * RET_QK, RET_V).astype(BF16)
    rg = proj(2 * RET_QK + RET_V, RET_V)
    rg_ref[0] = (rg * jax.nn.sigmoid(rg)).astype(BF16)

    ff = jnp.dot(hn, wf_ref[...], preferred_element_type=F32) + fb_ref[...]
    lf = jnp.minimum(ff, 0.0) - jnp.log1p(jnp.exp(-jnp.abs(ff)))
    tri = (lax.broadcasted_iota(jnp.int32, (tl, tl), 0) >= lax.broadcasted_iota(jnp.int32, (tl, tl), 1))
    tri = jnp.where(tri, 1.0, 0.0).astype(BF16)
    c = carry_ref[0:1, :]
    for part in _split3(lf):
        c = c + jnp.dot(tri, part.astype(BF16), preferred_element_type=F32)
    carry_ref[...] = jnp.broadcast_to(c[tl - 1:tl, :], (SUBLANES, LANES))
    cl_ref[0] = carry_ref[...]

    if n_invalid:
        row = j * tl + lax.broadcasted_iota(jnp.int32, (tl, LANES), 0)
        key_mask = jnp.where(row >= n_invalid, 0.0, NEG)
    else:
        key_mask = jnp.zeros((tl, LANES), F32)

    c_parts = _split3(c)
    fox_base = 2 * RET_QK + 2 * RET_V
    for pair in range(FOX_HEADS // 2):
        fq = proj(fox_base + LANES * pair, LANES) * (FOX_DH ** -0.5)
        fk = proj(fox_base + FOX_W + LANES * pair, LANES)
        fv = proj(fox_base + 2 * FOX_W + LANES * pair, LANES)
        for hh in range(2):
            h = 2 * pair + hh
            is_data = (lane < FOX_DH) if hh == 0 else (lane >= FOX_DH)
            e = (lane - FOX_DH) if hh == 0 else lane
            hi, mid, lo = (jnp.broadcast_to(part[:, h:h + 1], (tl, LANES)) for part in c_parts)
            ones3 = (e >= 0) & (e < 3)
            qx = jnp.where(e == 0, hi, jnp.where(e == 1, mid, jnp.where(e == 2, lo,
                 jnp.where((e >= 3) & (e < N_EXTRA), 1.0, 0.0))))
            kx = jnp.where(ones3, 1.0, jnp.where(e == 3, -hi, jnp.where(e == 4, -mid,
                 jnp.where(e == 5, -lo, jnp.where(e == 6, key_mask, 0.0)))))
            vx = jnp.where(e == 0, 1.0, 0.0)
            fq_ref[0, h] = jnp.where(is_data, fq, qx).astype(BF16)
            fk_ref[0, h] = jnp.where(is_data, fk, kx).astype(BF16)
            fv_ref[0, h] = jnp.where(is_data, fv, vx).astype(BF16)


def _inproj(x, g, w_main, w_ff, fb, tabs, c0, *, tl, n_invalid):
    nb, rows, _ = x.shape
    grid = (nb, rows // tl)
    row_blk = lambda w: pl.BlockSpec((1, tl, w), lambda b, j: (b, j, 0))
    head_blk = pl.BlockSpec((1, FOX_HEADS, tl, LANES), lambda b, j: (b, 0, j, 0))
    const = lambda shape: pl.BlockSpec(shape, lambda b, j: (0,) * len(shape), pipeline_mode=pl.Buffered(1))
    tab_blk = pl.BlockSpec((tl, LANES), lambda b, j: (j, 0))
    out_shape = (
        jax.ShapeDtypeStruct((nb, rows, RET_QK), BF16), jax.ShapeDtypeStruct((nb, rows, RET_QK), BF16),
        jax.ShapeDtypeStruct((nb, rows, RET_V), BF16), jax.ShapeDtypeStruct((nb, rows, RET_V), BF16),
        jax.ShapeDtypeStruct((nb, FOX_HEADS, rows, LANES), BF16),
        jax.ShapeDtypeStruct((nb, FOX_HEADS, rows, LANES), BF16),
        jax.ShapeDtypeStruct((nb, FOX_HEADS, rows, LANES), BF16),
        jax.ShapeDtypeStruct((nb, SUBLANES, LANES), F32),
    )
    return pl.pallas_call(
        functools.partial(_inproj_kernel, tl=tl, n_invalid=n_invalid),
        grid=grid,
        in_specs=[row_blk(D_MODEL), const((1, D_MODEL)), const((D_MODEL, MAIN_W)), const((D_MODEL, LANES)),
                  const((1, LANES)), tab_blk, tab_blk, tab_blk, tab_blk, const((SUBLANES, LANES))],
        out_specs=(row_blk(RET_QK), row_blk(RET_QK), row_blk(RET_V), row_blk(RET_V),
                   head_blk, head_blk, head_blk, pl.BlockSpec((1, SUBLANES, LANES), lambda b, j: (b, 0, 0))),
        out_shape=out_shape,
        scratch_shapes=[pltpu.VMEM((SUBLANES, LANES), F32)],
        compiler_params=pltpu.CompilerParams(dimension_semantics=("arbitrary", "arbitrary"),
                                             vmem_limit_bytes=VMEM_LIMIT),
        name=f"inproj_{rows}",
    )(x, g, w_main, w_ff, fb, *tabs, c0)


def _retention_kernel(q_ref, k_ref, v_ref, gate_ref, r0_ref, d_ref, wq_ref, wk_ref, gt_ref, ng_ref,
                      o_ref, rf_ref, state_ref, *, t, n_blocks):
    state_ref[...] = r0_ref[...]
    lane = lax.broadcasted_iota(jnp.int32, (t, LANES), 1)

    def block(i, carry):
        rows = pl.ds(pl.multiple_of(i * t, t), t)
        q = q_ref[0, rows, :]
        k = k_ref[0, rows, :]
        v = v_ref[0, rows, :]
        qw = (q.astype(F32) * wq_ref[0]).astype(BF16)
        kw = (k.astype(F32) * wk_ref[0]).astype(BF16)
        for hh in range(2):
            in_head = (lane < RET_DK) if hh == 0 else (lane >= RET_DK)
            vh = v[:, RET_DV * hh:RET_DV * (hh + 1)]
            qm = jnp.where(in_head, q, jnp.zeros_like(q))
            s = lax.dot_general(qm, k, (((1,), (1,)), ((), ())), preferred_element_type=F32)
            intra = jnp.dot((s * d_ref[hh]).astype(BF16), vh, preferred_element_type=F32)
            r = state_ref[hh]
            inter = jnp.dot(qw, r.astype(BF16), preferred_element_type=F32)
            kwm = jnp.where(in_head, kw, jnp.zeros_like(kw))
            u = lax.dot_general(kwm, vh, (((0,), (0,)), ((), ())), preferred_element_type=F32)
            state_ref[hh] = gt_ref[hh] * r + u
            o = intra + inter
            cols = slice(RET_DV * hh, RET_DV * (hh + 1))
            o = (o * _rms(o)) * ng_ref[:, cols] * gate_ref[0, rows, cols].astype(F32)
            o_ref[0, rows, cols] = o.astype(BF16)
        return carry

    lax.fori_loop(0, n_blocks, block, 0)
    rf_ref[0] = state_ref[...]


def _retention(rq, rk, rv, gate, r0, tabs, norm_g, *, t):
    nb, rows, _ = rq.shape
    d_tab, wq_tab, wk_tab, gt_tab = tabs
    n_pairs = RET_HEADS // 2
    seq = lambda w: pl.BlockSpec((1, rows, w), lambda b, p: (b, 0, p))
    return pl.pallas_call(
        functools.partial(_retention_kernel, t=t, n_blocks=rows // t),
        grid=(nb, n_pairs),
        in_specs=[seq(LANES), seq(LANES), seq(2 * RET_DV), seq(2 * RET_DV),
                  pl.BlockSpec((2, LANES, RET_DV), lambda b, p: (p, 0, 0)),
                  pl.BlockSpec((2, t, t), lambda b, p: (p, 0, 0)),
                  pl.BlockSpec((1, t, LANES), lambda b, p: (p, 0, 0)),
                  pl.BlockSpec((1, t, LANES), lambda b, p: (p, 0, 0)),
                  pl.BlockSpec((2, 1, RET_DV), lambda b, p: (p, 0, 0)),
                  pl.BlockSpec((1, 2 * RET_DV), lambda b, p: (0, p))],
        out_specs=(seq(2 * RET_DV), pl.BlockSpec((1, 2, LANES, RET_DV), lambda b, p: (b, p, 0, 0))),
        out_shape=(jax.ShapeDtypeStruct((nb, rows, RET_V), BF16),
                   jax.ShapeDtypeStruct((nb, RET_HEADS, LANES, RET_DV), F32)),
        scratch_shapes=[pltpu.VMEM((2, LANES, RET_DV), F32)],
        compiler_params=pltpu.CompilerParams(dimension_semantics=("arbitrary", "arbitrary"),
                                             vmem_limit_bytes=VMEM_LIMIT),
        name=f"retention_{rows}",
    )(rq, rk, rv, gate, r0, d_tab, wq_tab, wk_tab, gt_tab, norm_g)


def _fox_kernel(*refs, tq, n_tiles, has_prefix):
    if has_prefix:
        q_ref, k_ref, v_ref, kp_ref, vp_ref, o_ref = refs
    else:
        q_ref, k_ref, v_ref, o_ref = refs
    lane = lax.broadcasted_iota(jnp.int32, (tq, LANES), 1)
    causal = lax.broadcasted_iota(jnp.int32, (tq, tq), 1) <= lax.broadcasted_iota(jnp.int32, (tq, tq), 0)
    nt_dims = (((1,), (1,)), ((), ()))

    def update(m, acc, s, v):
        m_new = jnp.maximum(m, jnp.max(s, axis=1, keepdims=True))
        p = jnp.exp(s - m_new)
        acc = jnp.exp(m - m_new) * acc + jnp.dot(p.astype(BF16), v, preferred_element_type=F32)
        return m_new, acc

    def q_tile(i, carry):
        rows = pl.ds(pl.multiple_of(i * tq, tq), tq)
        outs = []
        for hh in range(2):
            q = q_ref[0, hh, rows, :]
            m = jnp.full((tq, 1), NEG, F32)
            acc = jnp.zeros((tq, LANES), F32)
            if has_prefix:
                s = lax.dot_general(q, kp_ref[0, hh], nt_dims, preferred_element_type=F32)
                m, acc = update(m, acc, s, vp_ref[0, hh])

            def k_chunk(jj, mc, hh=hh, q=q):
                kr = pl.ds(pl.multiple_of(jj * tq, tq), tq)
                s = lax.dot_general(q, k_ref[0, hh, kr, :], nt_dims, preferred_element_type=F32)
                return update(mc[0], mc[1], s, v_ref[0, hh, kr, :])

            m, acc = lax.fori_loop(0, i, k_chunk, (m, acc))
            s = lax.dot_general(q, k_ref[0, hh, rows, :], nt_dims, preferred_element_type=F32)
            m, acc = update(m, acc, jnp.where(causal, s, NEG), v_ref[0, hh, rows, :])
            sum_lane = FOX_DH if hh == 0 else 0
            outs.append(acc / acc[:, sum_lane:sum_lane + 1])
        o_ref[0, rows, :] = jnp.where(lane < FOX_DH, outs[0], outs[1]).astype(BF16)
        return carry

    lax.fori_loop(0, n_tiles, q_tile, 0)


def _fox(fq, fk, fv, kp, vp, *, tq):
    nb, _, rows, _ = fq.shape
    has_prefix = kp is not None
    pair_blk = pl.BlockSpec((1, 2, rows, LANES), lambda b, p: (b, p, 0, 0))
    in_specs = [pair_blk, pair_blk, pair_blk]
    args = [fq, fk, fv]
    if has_prefix:
        pre_blk = pl.BlockSpec((1, 2, PREFIX, LANES), lambda b, p: (0, p, 0, 0))
        in_specs += [pre_blk, pre_blk]
        args += [kp, vp]
    return pl.pallas_call(
        functools.partial(_fox_kernel, tq=tq, n_tiles=rows // tq, has_prefix=has_prefix),
        grid=(nb, FOX_HEADS // 2),
        in_specs=in_specs,
        out_specs=pl.BlockSpec((1, rows, LANES), lambda b, p: (b, 0, p)),
        out_shape=jax.ShapeDtypeStruct((nb, rows, FOX_W), BF16),
        compiler_params=pltpu.CompilerParams(dimension_semantics=("arbitrary", "arbitrary"),
                                             vmem_limit_bytes=VMEM_LIMIT),
        name=f"fox_{rows}",
    )(*args)


def _mixer_residual(x_ref, or_ref, of_ref, wo_ref):
    y = jnp.dot(or_ref[0], wo_ref[0:RET_V, :], preferred_element_type=F32)
    y = y + jnp.dot(of_ref[0], wo_ref[RET_V:RET_V + FOX_W, :], preferred_element_type=F32)
    return x_ref[0] + y


def _ffn_prefix_kernel(x_ref, or_ref, of_ref, wo_ref, g2_ref, wa_ref, a_ref, *, n_invalid):
    h1 = _mixer_residual(x_ref, or_ref, of_ref, wo_ref)
    hn = ((h1 * _rms(h1)) * g2_ref[...]).astype(BF16)
    a = jnp.dot(hn, wa_ref[...], preferred_element_type=F32)
    row = lax.broadcasted_iota(jnp.int32, a.shape, 0)
    a_ref[...] = jnp.where(row >= n_invalid, a, 0.0)


def _ffn_prefix(x, o_r, o_f, w_out, g2, w_up):
    full = lambda shape: pl.BlockSpec(shape, lambda i: (0,) * len(shape))
    return pl.pallas_call(
        functools.partial(_ffn_prefix_kernel, n_invalid=N_PAD),
        grid=(1,),
        in_specs=[full((1, PREFIX, D_MODEL)), full((1, PREFIX, RET_V)), full((1, PREFIX, FOX_W)),
                  full((D_MODEL, D_MODEL)), full((1, D_MODEL)), full((D_MODEL, D_FF))],
        out_specs=full((PREFIX, D_FF)),
        out_shape=jax.ShapeDtypeStruct((PREFIX, D_FF), F32),
        compiler_params=pltpu.CompilerParams(vmem_limit_bytes=VMEM_LIMIT),
        name="ffn_prefix",
    )(x, o_r, o_f, w_out, g2, w_up)


def _ffn_kernel(x_ref, or_ref, of_ref, wo_ref, g2_ref, wu_ref, cw_ref, cb_ref, wd_ref, gf_ref, halo_ref,
                o_ref, carry_ref, abuf_ref, *, tm):
    @pl.when(pl.program_id(1) == 0)
    def _():
        carry_ref[...] = halo_ref[...]

    h1 = _mixer_residual(x_ref, or_ref, of_ref, wo_ref)
    hn = ((h1 * _rms(h1)) * g2_ref[...]).astype(BF16)
    y = jnp.zeros((tm, D_MODEL), F32)
    for c in range(D_FF // FF_CHUNK):
        cols = slice(c * FF_CHUNK, (c + 1) * FF_CHUNK)
        a = jnp.dot(hn, wu_ref[:, cols], preferred_element_type=F32)
        b = jnp.dot(hn, wu_ref[:, D_FF + c * FF_CHUNK:D_FF + (c + 1) * FF_CHUNK], preferred_element_type=F32)
        abuf_ref[0:SUBLANES, :] = carry_ref[:, cols]
        abuf_ref[SUBLANES:SUBLANES + tm, :] = a
        carry_ref[:, cols] = a[tm - SUBLANES:tm, :]
        acc = cb_ref[:, cols] + abuf_ref[SUBLANES - 2:SUBLANES - 2 + tm, :] * cw_ref[0:1, cols]
        acc = acc + abuf_ref[SUBLANES - 1:SUBLANES - 1 + tm, :] * cw_ref[1:2, cols]
        acc = acc + a * cw_ref[2:3, cols]
        gated = (acc * jax.nn.sigmoid(acc) * b).astype(BF16)
        y = y + jnp.dot(gated, wd_ref[cols, :], preferred_element_type=F32)
    h2 = h1 + y
    o_ref[0] = (h2 * _rms(h2)) * gf_ref[...]


def _ffn(x, o_r, o_f, w_out, g2, w_up, conv_w, conv_b, w_down, gf, halo, *, tm):
    nb, rows, _ = x.shape
    row_blk = lambda w: pl.BlockSpec((1, tm, w), lambda b, j: (b, j, 0))
    const = lambda shape: pl.BlockSpec(shape, lambda b, j: (0,) * len(shape), pipeline_mode=pl.Buffered(1))
    return pl.pallas_call(
        functools.partial(_ffn_kernel, tm=tm),
        grid=(nb, rows // tm),
        in_specs=[row_blk(D_MODEL), row_blk(RET_V), row_blk(FOX_W), const((D_MODEL, D_MODEL)),
                  const((1, D_MODEL)), const((D_MODEL, 2 * D_FF)), const((CONV_W, D_FF)), const((1, D_FF)),
                  const((D_FF, D_MODEL)), const((1, D_MODEL)), const((SUBLANES, D_FF))],
        out_specs=row_blk(D_MODEL),
        out_shape=jax.ShapeDtypeStruct((nb, rows, D_MODEL), F32),
        scratch_shapes=[pltpu.VMEM((SUBLANES, D_FF), F32), pltpu.VMEM((SUBLANES + tm, FF_CHUNK), F32)],
        compiler_params=pltpu.CompilerParams(dimension_semantics=("arbitrary", "arbitrary"),
                                             vmem_limit_bytes=VMEM_LIMIT),
        name="ffn",
    )(x, o_r, o_f, w_out, g2, w_up, conv_w, conv_b, w_down, gf, halo)


def _rotary_tables(length):
    half = RET_DK // 2
    inv = 1.0 / (ROPE_BASE ** (jnp.arange(half, dtype=F32) / half))
    ang = jnp.arange(length).astype(F32)[:, None] * inv[None, :]
    cos = jnp.tile(jnp.cos(ang), (1, LANES // half))
    sin = jnp.tile(jnp.concatenate([-jnp.sin(ang), jnp.sin(ang)], axis=1), (1, LANES // RET_DK))
    valid = (jnp.arange(length) >= N_PAD).astype(F32)[:, None]
    k_scale = (RET_DK ** -0.5) * valid
    return cos, sin, cos * k_scale, sin * k_scale


def _retention_tables(t):
    log_g = jnp.log1p(-jnp.exp2(-5.0 - jnp.arange(RET_HEADS, dtype=F32)))
    n = jnp.arange(t, dtype=F32)
    visible = (jnp.arange(t)[None, :] // CHUNK) <= (jnp.arange(t)[:, None] // CHUNK)
    d = jnp.exp(jnp.abs(n[:, None] - n[None, :])[None] * log_g[:, None, None]) * visible[None]
    lane_log_g = jnp.repeat(log_g, RET_DK).reshape(RET_HEADS // 2, 1, LANES)
    wq = jnp.exp((n + 1.0)[None, :, None] * lane_log_g)
    wk = jnp.exp((t - 1.0 - n)[None, :, None] * lane_log_g)
    gt = jnp.broadcast_to(jnp.exp(t * log_g)[:, None, None], (RET_HEADS, 1, RET_DV))
    return d, wq, wk, gt


def kernel(x, meta_tokens, attn_norm_g, w_in, fox_forget_b, ret_norm_g, w_out, ffn_norm_g, w_up, conv_w,
           conv_b, w_down, final_norm_g):
    assert w_in.shape[0] == 1, "single-layer block"
    batch, seq, _ = x.shape
    w_main = w_in[0, :, :MAIN_W].astype(BF16)
    w_ff = jnp.pad(w_in[0, :, MAIN_W:], ((0, 0), (0, LANES - FOX_HEADS))).astype(BF16)
    fb = jnp.pad(fox_forget_b[0], (0, LANES - FOX_HEADS)).reshape(1, LANES)
    g1 = attn_norm_g[0].reshape(1, D_MODEL)
    g2 = ffn_norm_g[0].reshape(1, D_MODEL)
    gf = final_norm_g.reshape(1, D_MODEL)
    ng = ret_norm_g[0].reshape(1, RET_V)
    w_out_b = w_out[0].astype(BF16)
    w_up_b = w_up[0].astype(BF16)
    w_down_b = w_down[0].astype(BF16)
    cb = conv_b[0].reshape(1, D_FF)

    rot = _rotary_tables(PREFIX + seq)
    rot_pre = tuple(t[:PREFIX] for t in rot)
    rot_seq = tuple(t[PREFIX:] for t in rot)
    h_pre = jnp.concatenate([jnp.zeros((N_PAD, D_MODEL), x.dtype), meta_tokens.astype(x.dtype)], axis=0)[None]

    zeros_c = jnp.zeros((SUBLANES, LANES), F32)
    rq0, rk0, rv0, rg0, fq0, fk0, fv0, c_pre = _inproj(h_pre, g1, w_main, w_ff, fb, rot_pre, zeros_c,
                                                        tl=PREFIX, n_invalid=N_PAD)
    r_zero = jnp.zeros((RET_HEADS, LANES, RET_DV), F32)
    or0, r_pre = _retention(rq0, rk0, rv0, rg0, r_zero, _retention_tables(PREFIX), ng, t=PREFIX)
    of0 = _fox(fq0, fk0, fv0, None, None, tq=PREFIX)
    a_pre = _ffn_prefix(h_pre, or0, of0, w_out_b, g2, w_up_b)
    halo = a_pre[PREFIX - SUBLANES:]

    rq, rk, rv, rg, fq, fk, fv, _ = _inproj(x, g1, w_main, w_ff, fb, rot_seq, c_pre[0],
                                            tl=512, n_invalid=0)
    o_r, _ = _retention(rq, rk, rv, rg, r_pre[0], _retention_tables(256), ng, t=256)
    o_f = _fox(fq, fk, fv, fk0, fv0, tq=256)
    return _ffn(x, o_r, o_f, w_out_b, g2, w_up_b, conv_w[0], cb, w_down_b, gf, halo, tm=512)
```

```python
import functools

import jax
import jax.numpy as jnp
from jax import lax
from jax.experimental import pallas as pl
from jax.experimental.pallas import tpu as pltpu

F32 = jnp.float32
BF16 = jnp.bfloat16

D_MODEL = 1024
CHUNK = 64
N_META = 16
PREFIX = 128
N_PAD = PREFIX - N_META
RET_HEADS = 4
RET_DK = 64
RET_DV = 128
FOX_HEADS = 8
FOX_DH = 64
D_FF = 2816
CONV_W = 3
ROPE_BASE = 10000.0
EPS = 1e-6
NEG = -1e30

RET_QK = RET_HEADS * RET_DK
RET_V = RET_HEADS * RET_DV
FOX_W = FOX_HEADS * FOX_DH
MAIN_W = 2 * RET_QK + 2 * RET_V + 3 * FOX_W
LANES = 128
SUBLANES = 8
N_EXTRA = 7
VMEM_LIMIT = 56 * 1024 * 1024
FF_CHUNK = 256


def _rms(x):
    return lax.rsqrt(jnp.mean(x * x, axis=-1, keepdims=True) + EPS)


def _split3(c):
    hi = c.astype(BF16).astype(F32)
    r = c - hi
    mid = r.astype(BF16).astype(F32)
    return hi, mid, r - mid


def _inproj_kernel(x_ref, g_ref, wm_ref, wf_ref, fb_ref, cq_ref, sq_ref, ck_ref, sk_ref, c0_ref,
                   rq_ref, rk_ref, rv_ref, rg_ref, fq_ref, fk_ref, fv_ref, cl_ref, carry_ref,
                   *, tl, n_invalid):
    j = pl.program_id(1)

    @pl.when(j == 0)
    def _():
        carry_ref[...] = c0_ref[...]

    x = x_ref[0]
    hn = ((x * _rms(x)) * g_ref[...]).astype(BF16)

    def proj(lo, width):
        return jnp.dot(hn, wm_ref[:, lo:lo + width], preferred_element_type=F32)

    lane = lax.broadcasted_iota(jnp.int32, (tl, LANES), 1)
    first_half = (lane & 32) == 0

    for base, c_ref, s_ref, o_ref in ((0, cq_ref, sq_ref, rq_ref), (RET_QK, ck_ref, sk_ref, rk_ref)):
        for blk in range(RET_QK // LANES):
            xb = proj(base + LANES * blk, LANES)
            swapped = jnp.where(first_half, pltpu.roll(xb, LANES - 32, 1), pltpu.roll(xb, 32, 1))
            o_ref[0, :, LANES * blk:LANES * (blk + 1)] = (xb * c_ref[...] + swapped * s_ref[...]).astype(BF16)

    rv_ref[0] = proj(2 * RET_QK, RET_V).astype(BF16)
    rg = proj(2 * RET_QK + RET_V, RET_V)
    rg_ref[0] = (rg * jax.nn.sigmoid(rg)).astype(BF16)

    ff = jnp.dot(hn, wf_ref[...], preferred_element_type=F32) + fb_ref[...]
    lf = jnp.minimum(ff, 0.0) - jnp.log1p(jnp.exp(-jnp.abs(ff)))
    tri = (lax.broadcasted_iota(jnp.int32, (tl, tl), 0) >= lax.broadcasted_iota(jnp.int32, (tl, tl), 1))
    tri = jnp.where(tri, 1.0, 0.0).astype(BF16)
    c = carry_ref[0:1, :]
    for part in _split3(lf):
        c = c + jnp.dot(tri, part.astype(BF16), preferred_element_type=F32)
    carry_ref[...] = jnp.broadcast_to(c[tl - 1:tl, :], (SUBLANES, LANES))
    cl_ref[0] = carry_ref[...]

    if n_invalid:
        row = j * tl + lax.broadcasted_iota(jnp.int32, (tl, LANES), 0)
        key_mask = jnp.where(row >= n_invalid, 0.0, NEG)
    else:
        key_mask = jnp.zeros((tl, LANES), F32)

    c_parts = _split3(c)
    fox_base = 2 * RET_QK + 2 * RET_V
    for pair in range(FOX_HEADS // 2):
        fq = proj(fox_base + LANES * pair, LANES) * (FOX_DH ** -0.5)
        fk = proj(fox_base + FOX_W + LANES * pair, LANES)
        fv = proj(fox_base + 2 * FOX_W + LANES * pair, LANES)
        for hh in range(2):
            h = 2 * pair + hh
            is_data = (lane < FOX_DH) if hh == 0 else (lane >= FOX_DH)
            e = (lane - FOX_DH) if hh == 0 else lane
            hi, mid, lo = (jnp.broadcast_to(part[:, h:h + 1], (tl, LANES)) for part in c_parts)
            ones3 = (e >= 0) & (e < 3)
            qx = jnp.where(e == 0, hi, jnp.where(e == 1, mid, jnp.where(e == 2, lo,
                 jnp.where((e >= 3) & (e < N_EXTRA), 1.0, 0.0))))
            kx = jnp.where(ones3, 1.0, jnp.where(e == 3, -hi, jnp.where(e == 4, -mid,
                 jnp.where(e == 5, -lo, jnp.where(e == 6, key_mask, 0.0)))))
            vx = jnp.where(e == 0, 1.0, 0.0)
            fq_ref[0, h] = jnp.where(is_data, fq, qx).astype(BF16)
            fk_ref[0, h] = jnp.where(is_data, fk, kx).astype(BF16)
            fv_ref[0, h] = jnp.where(is_data, fv, vx).astype(BF16)


def _inproj(x, g, w_main, w_ff, fb, tabs, c0, *, tl, n_invalid):
    nb, rows, _ = x.shape
    grid = (nb, rows // tl)
    row_blk = lambda w: pl.BlockSpec((1, tl, w), lambda b, j: (b, j, 0))
    head_blk = pl.BlockSpec((1, FOX_HEADS, tl, LANES), lambda b, j: (b, 0, j, 0))
    const = lambda shape: pl.BlockSpec(shape, lambda b, j: (0,) * len(shape), pipeline_mode=pl.Buffered(1))
    tab_blk = pl.BlockSpec((tl, LANES), lambda b, j: (j, 0))
    out_shape = (
        jax.ShapeDtypeStruct((nb, rows, RET_QK), BF16), jax.ShapeDtypeStruct((nb, rows, RET_QK), BF16),
        jax.ShapeDtypeStruct((nb, rows, RET_V), BF16), jax.ShapeDtypeStruct((nb, rows, RET_V), BF16),
        jax.ShapeDtypeStruct((nb, FOX_HEADS, rows, LANES), BF16),
        jax.ShapeDtypeStruct((nb, FOX_HEADS, rows, LANES), BF16),
        jax.ShapeDtypeStruct((nb, FOX_HEADS, rows, LANES), BF16),
        jax.ShapeDtypeStruct((nb, SUBLANES, LANES), F32),
    )
    return pl.pallas_call(
        functools.partial(_inproj_kernel, tl=tl, n_invalid=n_invalid),
        grid=grid,
        in_specs=[row_blk(D_MODEL), const((1, D_MODEL)), const((D_MODEL, MAIN_W)), const((D_MODEL, LANES)),
                  const((1, LANES)), tab_blk, tab_blk, tab_blk, tab_blk, const((SUBLANES, LANES))],
        out_specs=(row_blk(RET_QK), row_blk(RET_QK), row_blk(RET_V), row_blk(RET_V),
                   head_blk, head_blk, head_blk, pl.BlockSpec((1, SUBLANES, LANES), lambda b, j: (b, 0, 0))),
        out_shape=out_shape,
        scratch_shapes=[pltpu.VMEM((SUBLANES, LANES), F32)],
        compiler_params=pltpu.CompilerParams(dimension_semantics=("arbitrary", "arbitrary"),
                                             vmem_limit_bytes=VMEM_LIMIT),
        name=f"inproj_{rows}",
    )(x, g, w_main, w_ff, fb, *tabs, c0)


def _retention_kernel(q_ref, k_ref, v_ref, gate_ref, r0_ref, d_ref, wq_ref, wk_ref, gt_ref, ng_ref,
                      o_ref, rf_ref, state_ref, *, t, n_blocks):
    state_ref[...] = r0_ref[...]
    lane = lax.broadcasted_iota(jnp.int32, (t, LANES), 1)

    def block(i, carry):
        rows = pl.ds(pl.multiple_of(i * t, t), t)
        q = q_ref[0, rows, :]
        k = k_ref[0, rows, :]
        v = v_ref[0, rows, :]
        qw = (q.astype(F32) * wq_ref[0]).astype(BF16)
        kw = (k.astype(F32) * wk_ref[0]).astype(BF16)
        for hh in range(2):
            in_head = (lane < RET_DK) if hh == 0 else (lane >= RET_DK)
            vh = v[:, RET_DV * hh:RET_DV * (hh + 1)]
            qm = jnp.where(in_head, q, jnp.zeros_like(q))
            s = lax.dot_general(qm, k, (((1,), (1,)), ((), ())), preferred_element_type=F32)
            intra = jnp.dot((s * d_ref[hh]).astype(BF16), vh, preferred_element_type=F32)
            r = state_ref[hh]
            inter = jnp.dot(qw, r.astype(BF16), preferred_element_type=F32)
            kwm = jnp.where(in_head, kw, jnp.zeros_like(kw))
            u = lax.dot_general(kwm, vh, (((0,), (0,)), ((), ())), preferred_element_type=F32)
            state_ref[hh] = gt_ref[hh] * r + u
            o = intra + inter
            cols = slice(RET_DV * hh, RET_DV * (hh + 1))
            o = (o * _rms(o)) * ng_ref[:, cols] * gate_ref[0, rows, cols].astype(F32)
            o_ref[0, rows, cols] = o.astype(BF16)
        return carry

    lax.fori_loop(0, n_blocks, block, 0)
    rf_ref[0] = state_ref[...]


def _retention(rq, rk, rv, gate, r0, tabs, norm_g, *, t):
    nb, rows, _ = rq.shape
    d_tab, wq_tab, wk_tab, gt_tab = tabs
    n_pairs = RET_HEADS // 2
    seq = lambda w: pl.BlockSpec((1, rows, w), lambda b, p: (b, 0, p))
    return pl.pallas_call(
        functools.partial(_retention_kernel, t=t, n_blocks=rows // t),
        grid=(nb, n_pairs),
        in_specs=[seq(LANES), seq(LANES), seq(2 * RET_DV), seq(2 * RET_DV),
                  pl.BlockSpec((2, LANES, RET_DV), lambda b, p: (p, 0, 0)),
                  pl.BlockSpec((2, t, t), lambda b, p: (p, 0, 0)),
                  pl.BlockSpec((1, t, LANES), lambda b, p: (p, 0, 0)),
                  pl.BlockSpec((1, t, LANES), lambda b, p: (p, 0, 0)),
                  pl.BlockSpec((2, 1, RET_DV), lambda b, p: (p, 0, 0)),
                  pl.BlockSpec((1, 2 * RET_DV), lambda b, p: (0, p))],
        out_specs=(seq(2 * RET_DV), pl.BlockSpec((1, 2, LANES, RET_DV), lambda b, p: (b, p, 0, 0))),
        out_shape=(jax.ShapeDtypeStruct((nb, rows, RET_V), BF16),
                   jax.ShapeDtypeStruct((nb, RET_HEADS, LANES, RET_DV), F32)),
        scratch_shapes=[pltpu.VMEM((2, LANES, RET_DV), F32)],
        compiler_params=pltpu.CompilerParams(dimension_semantics=("arbitrary", "arbitrary"),
                                             vmem_limit_bytes=VMEM_LIMIT),
        name=f"retention_{rows}",
    )(rq, rk, rv, gate, r0, d_tab, wq_tab, wk_tab, gt_tab, norm_g)


def _fox_kernel(*refs, tq, n_tiles, has_prefix):
    if has_prefix:
        q_ref, k_ref, v_ref, kp_ref, vp_ref, o_ref, m_ref, acc_ref = refs
    else:
        q_ref, k_ref, v_ref, o_ref, m_ref, acc_ref = refs
    lane = lax.broadcasted_iota(jnp.int32, (tq, LANES), 1)
    causal = lax.broadcasted_iota(jnp.int32, (tq, tq), 1) <= lax.broadcasted_iota(jnp.int32, (tq, tq), 0)
    nt_dims = (((1,), (1,)), ((), ()))

    def update(hh, s, v):
        m_old = m_ref[hh]
        m_new = jnp.maximum(m_old, jnp.max(s, axis=1, keepdims=True))
        m_ref[hh] = m_new
        p = jnp.exp(s - jnp.tile(m_new, (1, s.shape[1] // LANES)))
        acc_ref[hh] = jnp.exp(m_old - m_new) * acc_ref[hh] + jnp.dot(p.astype(BF16), v,
                                                                      preferred_element_type=F32)

    def q_tile(i, carry):
        rows = pl.ds(pl.multiple_of(i * tq, tq), tq)
        m_ref[...] = jnp.full(m_ref.shape, NEG, F32)
        acc_ref[...] = jnp.zeros(acc_ref.shape, F32)
        if has_prefix:
            for hh in range(2):
                s = lax.dot_general(q_ref[0, hh, rows, :], kp_ref[0, hh], nt_dims, preferred_element_type=F32)
                update(hh, s, vp_ref[0, hh])

        def k_chunk(jj, c):
            kr = pl.ds(pl.multiple_of(jj * tq, tq), tq)
            for hh in range(2):
                s = lax.dot_general(q_ref[0, hh, rows, :], k_ref[0, hh, kr, :], nt_dims,
                                    preferred_element_type=F32)
                update(hh, s, v_ref[0, hh, kr, :])
            return c

        lax.fori_loop(0, i, k_chunk, 0)
        outs = []
        for hh in range(2):
            s = lax.dot_general(q_ref[0, hh, rows, :], k_ref[0, hh, rows, :], nt_dims,
                                preferred_element_type=F32)
            update(hh, jnp.where(causal, s, NEG), v_ref[0, hh, rows, :])
            acc = acc_ref[hh]
            sum_lane = FOX_DH if hh == 0 else 0
            outs.append(acc / acc[:, sum_lane:sum_lane + 1])
        o_ref[0, rows, :] = jnp.where(lane < FOX_DH, outs[0], outs[1]).astype(BF16)
        return carry

    lax.fori_loop(0, n_tiles, q_tile, 0)


def _fox(fq, fk, fv, kp, vp, *, tq):
    nb, _, rows, _ = fq.shape
    has_prefix = kp is not None
    pair_blk = pl.BlockSpec((1, 2, rows, LANES), lambda b, p: (b, p, 0, 0))
    in_specs = [pair_blk, pair_blk, pair_blk]
    args = [fq, fk, fv]
    if has_prefix:
        pre_blk = pl.BlockSpec((1, 2, PREFIX, LANES), lambda b, p: (0, p, 0, 0))
        in_specs += [pre_blk, pre_blk]
        args += [kp, vp]
    return pl.pallas_call(
        functools.partial(_fox_kernel, tq=tq, n_tiles=rows // tq, has_prefix=has_prefix),
        grid=(nb, FOX_HEADS // 2),
        in_specs=in_specs,
        out_specs=pl.BlockSpec((1, rows, LANES), lambda b, p: (b, 0, p)),
        out_shape=jax.ShapeDtypeStruct((nb, rows, FOX_W), BF16),
        scratch_shapes=[pltpu.VMEM((2, tq, LANES), F32), pltpu.VMEM((2, tq, LANES), F32)],
        compiler_params=pltpu.CompilerParams(dimension_semantics=("arbitrary", "arbitrary"),
                                             vmem_limit_bytes=VMEM_LIMIT),
        name=f"fox_{rows}",
    )(*args)


def _mixer_residual(x_ref, or_ref, of_ref, wo_ref):
    y = jnp.dot(or_ref[0], wo_ref[0:RET_V, :], preferred_element_type=F32)
    y = y + jnp.dot(of_ref[0], wo_ref[RET_V:RET_V + FOX_W, :], preferred_element_type=F32)
    return x_ref[0] + y


def _ffn_prefix_kernel(x_ref, or_ref, of_ref, wo_ref, g2_ref, wa_ref, a_ref, *, n_invalid):
    h1 = _mixer_residual(x_ref, or_ref, of_ref, wo_ref)
    hn = ((h1 * _rms(h1)) * g2_ref[...]).astype(BF16)
    a = jnp.dot(hn, wa_ref[...], preferred_element_type=F32)
    row = lax.broadcasted_iota(jnp.int32, a.shape, 0)
    a_ref[...] = jnp.where(row >= n_invalid, a, 0.0)


def _ffn_prefix(x, o_r, o_f, w_out, g2, w_up):
    full = lambda shape: pl.BlockSpec(shape, lambda i: (0,) * len(shape))
    return pl.pallas_call(
        functools.partial(_ffn_prefix_kernel, n_invalid=N_PAD),
        grid=(1,),
        in_specs=[full((1, PREFIX, D_MODEL)), full((1, PREFIX, RET_V)), full((1, PREFIX, FOX_W)),
                  full((D_MODEL, D_MODEL)), full((1, D_MODEL)), full((D_MODEL, D_FF))],
        out_specs=full((PREFIX, D_FF)),
        out_shape=jax.ShapeDtypeStruct((PREFIX, D_FF), F32),
        compiler_params=pltpu.CompilerParams(vmem_limit_bytes=VMEM_LIMIT),
        name="ffn_prefix",
    )(x, o_r, o_f, w_out, g2, w_up)


def _ffn_kernel(x_ref, or_ref, of_ref, wo_ref, g2_ref, wu_ref, cw_ref, cb_ref, wd_ref, gf_ref, halo_ref,
                o_ref, carry_ref, abuf_ref, *, tm):
    @pl.when(pl.program_id(1) == 0)
    def _():
        carry_ref[...] = halo_ref[...]

    h1 = _mixer_residual(x_ref, or_ref, of_ref, wo_ref)
    hn = ((h1 * _rms(h1)) * g2_ref[...]).astype(BF16)
    y = jnp.zeros((tm, D_MODEL), F32)
    for c in range(D_FF // FF_CHUNK):
        cols = slice(c * FF_CHUNK, (c + 1) * FF_CHUNK)
        a = jnp.dot(hn, wu_ref[:, cols], preferred_element_type=F32)
        b = jnp.dot(hn, wu_ref[:, D_FF + c * FF_CHUNK:D_FF + (c + 1) * FF_CHUNK], preferred_element_type=F32)
        abuf_ref[0:SUBLANES, :] = carry_ref[:, cols]
        abuf_ref[SUBLANES:SUBLANES + tm, :] = a
        carry_ref[:, cols] = a[tm - SUBLANES:tm, :]
        acc = cb_ref[:, cols] + abuf_ref[SUBLANES - 2:SUBLANES - 2 + tm, :] * cw_ref[0:1, cols]
        acc = acc + abuf_ref[SUBLANES - 1:SUBLANES - 1 + tm, :] * cw_ref[1:2, cols]
        acc = acc + a * cw_ref[2:3, cols]
        gated = (acc * jax.nn.sigmoid(acc) * b).astype(BF16)
        y = y + jnp.dot(gated, wd_ref[cols, :], preferred_element_type=F32)
    h2 = h1 + y
    o_ref[0] = (h2 * _rms(h2)) * gf_ref[...]


def _ffn(x, o_r, o_f, w_out, g2, w_up, conv_w, conv_b, w_down, gf, halo, *, tm):
    nb, rows, _ = x.shape
    row_blk = lambda w: pl.BlockSpec((1, tm, w), lambda b, j: (b, j, 0))
    const = lambda shape: pl.BlockSpec(shape, lambda b, j: (0,) * len(shape), pipeline_mode=pl.Buffered(1))
    return pl.pallas_call(
        functools.partial(_ffn_kernel, tm=tm),
        grid=(nb, rows // tm),
        in_specs=[row_blk(D_MODEL), row_blk(RET_V), row_blk(FOX_W), const((D_MODEL, D_MODEL)),
                  const((1, D_MODEL)), const((D_MODEL, 2 * D_FF)), const((CONV_W, D_FF)), const((1, D_FF)),
                  const((D_FF, D_MODEL)), const((1, D_MODEL)), const((SUBLANES, D_FF))],
        out_specs=row_blk(D_MODEL),
        out_shape=jax.ShapeDtypeStruct((nb, rows, D_MODEL), F32),
        scratch_shapes=[pltpu.VMEM((SUBLANES, D_FF), F32), pltpu.VMEM((SUBLANES + tm, FF_CHUNK), F32)],
        compiler_params=pltpu.CompilerParams(dimension_semantics=("arbitrary", "arbitrary"),
                                             vmem_limit_bytes=VMEM_LIMIT),
        name="ffn",
    )(x, o_r, o_f, w_out, g2, w_up, conv_w, conv_b, w_down, gf, halo)


def _rotary_tables(length):
    half = RET_DK // 2
    inv = 1.0 / (ROPE_BASE ** (jnp.arange(half, dtype=F32) / half))
    ang = jnp.arange(length).astype(F32)[:, None] * inv[None, :]
    cos = jnp.tile(jnp.cos(ang), (1, LANES // half))
    sin = jnp.tile(jnp.concatenate([-jnp.sin(ang), jnp.sin(ang)], axis=1), (1, LANES // RET_DK))
    valid = (jnp.arange(length) >= N_PAD).astype(F32)[:, None]
    k_scale = (RET_DK ** -0.5) * valid
    return cos, sin, cos * k_scale, sin * k_scale


def _retention_tables(t):
    log_g = jnp.log1p(-jnp.exp2(-5.0 - jnp.arange(RET_HEADS, dtype=F32)))
    n = jnp.arange(t, dtype=F32)
    visible = (jnp.arange(t)[None, :] // CHUNK) <= (jnp.arange(t)[:, None] // CHUNK)
    d = jnp.exp(jnp.abs(n[:, None] - n[None, :])[None] * log_g[:, None, None]) * visible[None]
    lane_log_g = jnp.repeat(log_g, RET_DK).reshape(RET_HEADS // 2, 1, LANES)
    wq = jnp.exp((n + 1.0)[None, :, None] * lane_log_g)
    wk = jnp.exp((t - 1.0 - n)[None, :, None] * lane_log_g)
    gt = jnp.broadcast_to(jnp.exp(t * log_g)[:, None, None], (RET_HEADS, 1, RET_DV))
    return d, wq, wk, gt


def kernel(x, meta_tokens, attn_norm_g, w_in, fox_forget_b, ret_norm_g, w_out, ffn_norm_g, w_up, conv_w,
           conv_b, w_down, final_norm_g):
    assert w_in.shape[0] == 1, "single-layer block"
    batch, seq, _ = x.shape
    w_main = w_in[0, :, :MAIN_W].astype(BF16)
    w_ff = jnp.pad(w_in[0, :, MAIN_W:], ((0, 0), (0, LANES - FOX_HEADS))).astype(BF16)
    fb = jnp.pad(fox_forget_b[0], (0, LANES - FOX_HEADS)).reshape(1, LANES)
    g1 = attn_norm_g[0].reshape(1, D_MODEL)
    g2 = ffn_norm_g[0].reshape(1, D_MODEL)
    gf = final_norm_g.reshape(1, D_MODEL)
    ng = ret_norm_g[0].reshape(1, RET_V)
    w_out_b = w_out[0].astype(BF16)
    w_up_b = w_up[0].astype(BF16)
    w_down_b = w_down[0].astype(BF16)
    cb = conv_b[0].reshape(1, D_FF)

    rot = _rotary_tables(PREFIX + seq)
    rot_pre = tuple(t[:PREFIX] for t in rot)
    rot_seq = tuple(t[PREFIX:] for t in rot)
    h_pre = jnp.concatenate([jnp.zeros((N_PAD, D_MODEL), x.dtype), meta_tokens.astype(x.dtype)], axis=0)[None]

    zeros_c = jnp.zeros((SUBLANES, LANES), F32)
    rq0, rk0, rv0, rg0, fq0, fk0, fv0, c_pre = _inproj(h_pre, g1, w_main, w_ff, fb, rot_pre, zeros_c,
                                                        tl=PREFIX, n_invalid=N_PAD)
    r_zero = jnp.zeros((RET_HEADS, LANES, RET_DV), F32)
    or0, r_pre = _retention(rq0, rk0, rv0, rg0, r_zero, _retention_tables(PREFIX), ng, t=PREFIX)
    of0 = _fox(fq0, fk0, fv0, None, None, tq=PREFIX)
    a_pre = _ffn_prefix(h_pre, or0, of0, w_out_b, g2, w_up_b)
    halo = a_pre[PREFIX - SUBLANES:]

    rq, rk, rv, rg, fq, fk, fv, _ = _inproj(x, g1, w_main, w_ff, fb, rot_seq, c_pre[0],
                                            tl=512, n_invalid=0)
    o_r, _ = _retention(rq, rk, rv, rg, r_pre[0], _retention_tables(256), ng, t=256)
    o_f = _fox(fq, fk, fv, fk0, fv0, tq=512)
    return _ffn(x, o_r, o_f, w_out_b, g2, w_up_b, conv_w[0], cb, w_down_b, gf, halo, tm=512)
```

```python
import functools

import jax
import jax.numpy as jnp
from jax import lax
from jax.experimental import pallas as pl
from jax.experimental.pallas import tpu as pltpu

F32 = jnp.float32
BF16 = jnp.bfloat16

D_MODEL = 1024
CHUNK = 64
N_META = 16
PREFIX = 128
N_PAD = PREFIX - N_META
RET_HEADS = 4
RET_DK = 64
RET_DV = 128
FOX_HEADS = 8
FOX_DH = 64
D_FF = 2816
CONV_W = 3
ROPE_BASE = 10000.0
EPS = 1e-6
NEG = -1e30

RET_QK = RET_HEADS * RET_DK
RET_V = RET_HEADS * RET_DV
FOX_W = FOX_HEADS * FOX_DH
MAIN_W = 2 * RET_QK + 2 * RET_V + 3 * FOX_W
LANES = 128
SUBLANES = 8
N_EXTRA = 7
LOG2E = 1.4426950408889634
CUMSUM_ROWS = 256
VMEM_LIMIT = 56 * 1024 * 1024
FF_CHUNK = 256


def _rms(x):
    return lax.rsqrt(jnp.mean(x * x, axis=-1, keepdims=True) + EPS)


def _split3(c):
    hi = c.astype(BF16).astype(F32)
    r = c - hi
    mid = r.astype(BF16).astype(F32)
    return hi, mid, r - mid


def _inproj_kernel(x_ref, g_ref, wm_ref, wf_ref, fb_ref, cq_ref, sq_ref, ck_ref, sk_ref, c0_ref,
                   rq_ref, rk_ref, rv_ref, rg_ref, fq_ref, fk_ref, fv_ref, cl_ref, carry_ref,
                   *, tl, n_invalid):
    j = pl.program_id(1)

    @pl.when(j == 0)
    def _():
        carry_ref[...] = c0_ref[...]

    x = x_ref[0]
    hn = ((x * _rms(x)) * g_ref[...]).astype(BF16)

    def proj(lo, width):
        return jnp.dot(hn, wm_ref[:, lo:lo + width], preferred_element_type=F32)

    lane = lax.broadcasted_iota(jnp.int32, (tl, LANES), 1)
    first_half = (lane & 32) == 0

    rqk = proj(0, 2 * RET_QK)
    for base, c_ref, s_ref, o_ref in ((0, cq_ref, sq_ref, rq_ref), (RET_QK, ck_ref, sk_ref, rk_ref)):
        for blk in range(RET_QK // LANES):
            xb = rqk[:, base + LANES * blk:base + LANES * (blk + 1)]
            swapped = jnp.where(first_half, pltpu.roll(xb, LANES - 32, 1), pltpu.roll(xb, 32, 1))
            o_ref[0, :, LANES * blk:LANES * (blk + 1)] = (xb * c_ref[...] + swapped * s_ref[...]).astype(BF16)

    rv_ref[0] = proj(2 * RET_QK, RET_V).astype(BF16)
    rg = proj(2 * RET_QK + RET_V, RET_V)
    rg_ref[0] = (rg * jax.nn.sigmoid(rg)).astype(BF16)

    ff = jnp.dot(hn, wf_ref[...], preferred_element_type=F32) + fb_ref[...]
    lf = jnp.minimum(ff, 0.0) - jnp.log1p(jnp.exp(-jnp.abs(ff)))
    head_lane = lane < FOX_HEADS
    hi, mid, lo = (jnp.where(head_lane, part, 0.0) for part in _split3(lf))
    packed = (hi + pltpu.roll(mid, FOX_HEADS, 1) + pltpu.roll(lo, 2 * FOX_HEADS, 1)).astype(BF16)
    sub = min(tl, CUMSUM_ROWS)
    tri = lax.broadcasted_iota(jnp.int32, (sub, sub), 0) >= lax.broadcasted_iota(jnp.int32, (sub, sub), 1)
    tri = jnp.where(tri, 1.0, 0.0).astype(BF16)
    carry = carry_ref[0:1, :]
    c_blocks = []
    for r in range(tl // sub):
        c3 = jnp.dot(tri, packed[r * sub:(r + 1) * sub], preferred_element_type=F32)
        c_blk = (c3 + pltpu.roll(c3, LANES - FOX_HEADS, 1)) + pltpu.roll(c3, LANES - 2 * FOX_HEADS, 1) + carry
        carry = c_blk[sub - 1:sub, :]
        c_blocks.append(c_blk)
    c = jnp.concatenate(c_blocks, axis=0) if len(c_blocks) > 1 else c_blocks[0]
    carry_ref[...] = jnp.broadcast_to(carry, (SUBLANES, LANES))
    cl_ref[0] = carry_ref[...]

    if n_invalid:
        row = j * tl + lax.broadcasted_iota(jnp.int32, (tl, LANES), 0)
        key_mask = jnp.where(row >= n_invalid, 0.0, NEG)
    else:
        key_mask = jnp.zeros((tl, LANES), F32)

    c_parts = _split3(c * LOG2E)
    fox_base = 2 * RET_QK + 2 * RET_V
    fq_all = proj(fox_base, FOX_W) * (FOX_DH ** -0.5 * LOG2E)
    fk_all = proj(fox_base + FOX_W, FOX_W)
    fv_all = proj(fox_base + 2 * FOX_W, FOX_W)
    for pair in range(FOX_HEADS // 2):
        cols = slice(LANES * pair, LANES * (pair + 1))
        fq, fk, fv = fq_all[:, cols], fk_all[:, cols], fv_all[:, cols]
        for hh in range(2):
            h = 2 * pair + hh
            is_data = (lane < FOX_DH) if hh == 0 else (lane >= FOX_DH)
            e = (lane - FOX_DH) if hh == 0 else lane
            hi, mid, lo = (jnp.broadcast_to(part[:, h:h + 1], (tl, LANES)) for part in c_parts)
            ones3 = (e >= 0) & (e < 3)
            qx = jnp.where(e == 0, hi, jnp.where(e == 1, mid, jnp.where(e == 2, lo,
                 jnp.where((e >= 3) & (e < N_EXTRA), 1.0, 0.0))))
            kx = jnp.where(ones3, 1.0, jnp.where(e == 3, -hi, jnp.where(e == 4, -mid,
                 jnp.where(e == 5, -lo, jnp.where(e == 6, key_mask, 0.0)))))
            vx = jnp.where(e == 0, 1.0, 0.0)
            fq_ref[0, h] = jnp.where(is_data, fq, qx).astype(BF16)
            fk_ref[0, h] = jnp.where(is_data, fk, kx).astype(BF16)
            fv_ref[0, h] = jnp.where(is_data, fv, vx).astype(BF16)


def _inproj(x, g, w_main, w_ff, fb, tabs, c0, *, tl, n_invalid):
    nb, rows, _ = x.shape
    grid = (nb, rows // tl)
    row_blk = lambda w: pl.BlockSpec((1, tl, w), lambda b, j: (b, j, 0))
    head_blk = pl.BlockSpec((1, FOX_HEADS, tl, LANES), lambda b, j: (b, 0, j, 0))
    const = lambda shape: pl.BlockSpec(shape, lambda b, j: (0,) * len(shape), pipeline_mode=pl.Buffered(1))
    tab_blk = pl.BlockSpec((tl, LANES), lambda b, j: (j, 0))
    out_shape = (
        jax.ShapeDtypeStruct((nb, rows, RET_QK), BF16), jax.ShapeDtypeStruct((nb, rows, RET_QK), BF16),
        jax.ShapeDtypeStruct((nb, rows, RET_V), BF16), jax.ShapeDtypeStruct((nb, rows, RET_V), BF16),
        jax.ShapeDtypeStruct((nb, FOX_HEADS, rows, LANES), BF16),
        jax.ShapeDtypeStruct((nb, FOX_HEADS, rows, LANES), BF16),
        jax.ShapeDtypeStruct((nb, FOX_HEADS, rows, LANES), BF16),
        jax.ShapeDtypeStruct((nb, SUBLANES, LANES), F32),
    )
    return pl.pallas_call(
        functools.partial(_inproj_kernel, tl=tl, n_invalid=n_invalid),
        grid=grid,
        in_specs=[row_blk(D_MODEL), const((1, D_MODEL)), const((D_MODEL, MAIN_W)), const((D_MODEL, LANES)),
                  const((1, LANES)), tab_blk, tab_blk, tab_blk, tab_blk, const((SUBLANES, LANES))],
        out_specs=(row_blk(RET_QK), row_blk(RET_QK), row_blk(RET_V), row_blk(RET_V),
                   head_blk, head_blk, head_blk, pl.BlockSpec((1, SUBLANES, LANES), lambda b, j: (b, 0, 0))),
        out_shape=out_shape,
        scratch_shapes=[pltpu.VMEM((SUBLANES, LANES), F32)],
        compiler_params=pltpu.CompilerParams(dimension_semantics=("arbitrary", "arbitrary"),
                                             vmem_limit_bytes=VMEM_LIMIT),
        name=f"inproj_{rows}",
    )(x, g, w_main, w_ff, fb, *tabs, c0)


def _retention_kernel(q_ref, k_ref, v_ref, gate_ref, r0_ref, d_ref, wq_ref, wk_ref, gt_ref, ng_ref,
                      o_ref, rf_ref, state_ref, *, t, n_blocks):
    state_ref[...] = r0_ref[...]
    lane = lax.broadcasted_iota(jnp.int32, (t, LANES), 1)

    def block(i, carry):
        rows = pl.ds(pl.multiple_of(i * t, t), t)
        q = q_ref[0, rows, :]
        k = k_ref[0, rows, :]
        v = v_ref[0, rows, :]
        qw = (q.astype(F32) * wq_ref[0]).astype(BF16)
        kw = (k.astype(F32) * wk_ref[0]).astype(BF16)
        for hh in range(2):
            in_head = (lane < RET_DK) if hh == 0 else (lane >= RET_DK)
            vh = v[:, RET_DV * hh:RET_DV * (hh + 1)]
            qm = jnp.where(in_head, q, jnp.zeros_like(q))
            s = lax.dot_general(qm, k, (((1,), (1,)), ((), ())), preferred_element_type=F32)
            intra = jnp.dot((s * d_ref[hh]).astype(BF16), vh, preferred_element_type=F32)
            r = state_ref[hh]
            inter = jnp.dot(qw, r.astype(BF16), preferred_element_type=F32)
            kwm = jnp.where(in_head, kw, jnp.zeros_like(kw))
            u = lax.dot_general(kwm, vh, (((0,), (0,)), ((), ())), preferred_element_type=F32)
            state_ref[hh] = gt_ref[hh] * r + u
            o = intra + inter
            cols = slice(RET_DV * hh, RET_DV * (hh + 1))
            o = (o * _rms(o)) * ng_ref[:, cols] * gate_ref[0, rows, cols].astype(F32)
            o_ref[0, rows, cols] = o.astype(BF16)
        return carry

    lax.fori_loop(0, n_blocks, block, 0)
    rf_ref[0] = state_ref[...]


def _retention(rq, rk, rv, gate, r0, tabs, norm_g, *, t):
    nb, rows, _ = rq.shape
    d_tab, wq_tab, wk_tab, gt_tab = tabs
    n_pairs = RET_HEADS // 2
    seq = lambda w: pl.BlockSpec((1, rows, w), lambda b, p: (b, 0, p))
    return pl.pallas_call(
        functools.partial(_retention_kernel, t=t, n_blocks=rows // t),
        grid=(nb, n_pairs),
        in_specs=[seq(LANES), seq(LANES), seq(2 * RET_DV), seq(2 * RET_DV),
                  pl.BlockSpec((2, LANES, RET_DV), lambda b, p: (p, 0, 0)),
                  pl.BlockSpec((2, t, t), lambda b, p: (p, 0, 0)),
                  pl.BlockSpec((1, t, LANES), lambda b, p: (p, 0, 0)),
                  pl.BlockSpec((1, t, LANES), lambda b, p: (p, 0, 0)),
                  pl.BlockSpec((2, 1, RET_DV), lambda b, p: (p, 0, 0)),
                  pl.BlockSpec((1, 2 * RET_DV), lambda b, p: (0, p))],
        out_specs=(seq(2 * RET_DV), pl.BlockSpec((1, 2, LANES, RET_DV), lambda b, p: (b, p, 0, 0))),
        out_shape=(jax.ShapeDtypeStruct((nb, rows, RET_V), BF16),
                   jax.ShapeDtypeStruct((nb, RET_HEADS, LANES, RET_DV), F32)),
        scratch_shapes=[pltpu.VMEM((2, LANES, RET_DV), F32)],
        compiler_params=pltpu.CompilerParams(dimension_semantics=("arbitrary", "arbitrary"),
                                             vmem_limit_bytes=VMEM_LIMIT),
        name=f"retention_{rows}",
    )(rq, rk, rv, gate, r0, d_tab, wq_tab, wk_tab, gt_tab, norm_g)


def _fox_kernel(*refs, tq, n_tiles, has_prefix):
    if has_prefix:
        q_ref, k_ref, v_ref, kp_ref, vp_ref, o_ref, m_ref, acc_ref = refs
    else:
        q_ref, k_ref, v_ref, o_ref, m_ref, acc_ref = refs
    lane = lax.broadcasted_iota(jnp.int32, (tq, LANES), 1)
    causal = lax.broadcasted_iota(jnp.int32, (tq, tq), 1) <= lax.broadcasted_iota(jnp.int32, (tq, tq), 0)

    def scores(hh, rows, k):
        return lax.dot_general(q_ref[0, hh, rows, :], k, (((1,), (1,)), ((), ())), preferred_element_type=F32)

    def probs(s, m):
        return jnp.exp2((s - jnp.tile(m, (1, s.shape[1] // LANES))).astype(BF16))

    def q_tile(i, carry):
        rows = pl.ds(pl.multiple_of(i * tq, tq), tq)
        for hh in range(2):
            s = jnp.where(causal, scores(hh, rows, k_ref[0, hh, rows, :]), NEG)
            row_max = jnp.max(s, axis=1, keepdims=True)
            if has_prefix:
                s_pre = scores(hh, rows, kp_ref[0, hh])
                row_max = jnp.maximum(row_max, jnp.max(s_pre, axis=1, keepdims=True))
            m = jnp.broadcast_to(row_max, (tq, LANES))
            acc = jnp.dot(probs(s, m), v_ref[0, hh, rows, :], preferred_element_type=F32)
            if has_prefix:
                acc = acc + jnp.dot(probs(s_pre, m), vp_ref[0, hh], preferred_element_type=F32)
            m_ref[hh] = m
            acc_ref[hh] = acc

        def k_chunk(jj, c):
            kr = pl.ds(pl.multiple_of(jj * tq, tq), tq)
            for hh in range(2):
                s = scores(hh, rows, k_ref[0, hh, kr, :])
                m_old = m_ref[hh]
                m_new = jnp.maximum(m_old, jnp.max(s, axis=1, keepdims=True))
                m_ref[hh] = m_new
                pv = jnp.dot(probs(s, m_new), v_ref[0, hh, kr, :], preferred_element_type=F32)
                acc_ref[hh] = jnp.exp2(m_old - m_new) * acc_ref[hh] + pv
            return c

        lax.fori_loop(0, i, k_chunk, 0)
        outs = []
        for hh in range(2):
            acc = acc_ref[hh]
            sum_lane = FOX_DH if hh == 0 else 0
            outs.append(acc / acc[:, sum_lane:sum_lane + 1])
        o_ref[0, rows, :] = jnp.where(lane < FOX_DH, outs[0], outs[1]).astype(BF16)
        return carry

    lax.fori_loop(0, n_tiles, q_tile, 0)


def _fox(fq, fk, fv, kp, vp, *, tq):
    nb, _, rows, _ = fq.shape
    has_prefix = kp is not None
    pair_blk = pl.BlockSpec((1, 2, rows, LANES), lambda b, p: (b, p, 0, 0))
    in_specs = [pair_blk, pair_blk, pair_blk]
    args = [fq, fk, fv]
    if has_prefix:
        pre_blk = pl.BlockSpec((1, 2, PREFIX, LANES), lambda b, p: (0, p, 0, 0))
        in_specs += [pre_blk, pre_blk]
        args += [kp, vp]
    return pl.pallas_call(
        functools.partial(_fox_kernel, tq=tq, n_tiles=rows // tq, has_prefix=has_prefix),
        grid=(nb, FOX_HEADS // 2),
        in_specs=in_specs,
        out_specs=pl.BlockSpec((1, rows, LANES), lambda b, p: (b, 0, p)),
        out_shape=jax.ShapeDtypeStruct((nb, rows, FOX_W), BF16),
        scratch_shapes=[pltpu.VMEM((2, tq, LANES), F32), pltpu.VMEM((2, tq, LANES), F32)],
        compiler_params=pltpu.CompilerParams(dimension_semantics=("arbitrary", "arbitrary"),
                                             vmem_limit_bytes=VMEM_LIMIT),
        name=f"fox_{rows}",
    )(*args)


def _mixer_residual(x_ref, or_ref, of_ref, wo_ref):
    y = jnp.dot(or_ref[0], wo_ref[0:RET_V, :], preferred_element_type=F32)
    y = y + jnp.dot(of_ref[0], wo_ref[RET_V:RET_V + FOX_W, :], preferred_element_type=F32)
    return x_ref[0] + y


def _ffn_prefix_kernel(x_ref, or_ref, of_ref, wo_ref, g2_ref, wa_ref, a_ref, *, n_invalid):
    h1 = _mixer_residual(x_ref, or_ref, of_ref, wo_ref)
    hn = ((h1 * _rms(h1)) * g2_ref[...]).astype(BF16)
    a = jnp.dot(hn, wa_ref[...], preferred_element_type=F32)
    row = lax.broadcasted_iota(jnp.int32, a.shape, 0)
    a_ref[...] = jnp.where(row >= n_invalid, a, 0.0)


def _ffn_prefix(x, o_r, o_f, w_out, g2, w_up):
    full = lambda shape: pl.BlockSpec(shape, lambda i: (0,) * len(shape))
    return pl.pallas_call(
        functools.partial(_ffn_prefix_kernel, n_invalid=N_PAD),
        grid=(1,),
        in_specs=[full((1, PREFIX, D_MODEL)), full((1, PREFIX, RET_V)), full((1, PREFIX, FOX_W)),
                  full((D_MODEL, D_MODEL)), full((1, D_MODEL)), full((D_MODEL, D_FF))],
        out_specs=full((PREFIX, D_FF)),
        out_shape=jax.ShapeDtypeStruct((PREFIX, D_FF), F32),
        compiler_params=pltpu.CompilerParams(vmem_limit_bytes=VMEM_LIMIT),
        name="ffn_prefix",
    )(x, o_r, o_f, w_out, g2, w_up)


def _ffn_kernel(x_ref, or_ref, of_ref, wo_ref, g2_ref, wu_ref, cw_ref, cb_ref, wd_ref, gf_ref, halo_ref,
                o_ref, carry_ref, abuf_ref, gated_ref, *, tm):
    @pl.when(pl.program_id(1) == 0)
    def _():
        carry_ref[...] = halo_ref[...]

    h1 = _mixer_residual(x_ref, or_ref, of_ref, wo_ref)
    hn = ((h1 * _rms(h1)) * g2_ref[...]).astype(BF16)
    for c in range(D_FF // FF_CHUNK):
        cols = slice(c * FF_CHUNK, (c + 1) * FF_CHUNK)
        a = jnp.dot(hn, wu_ref[:, cols], preferred_element_type=F32)
        b = jnp.dot(hn, wu_ref[:, D_FF + c * FF_CHUNK:D_FF + (c + 1) * FF_CHUNK], preferred_element_type=F32)
        abuf_ref[0:SUBLANES, :] = carry_ref[:, cols]
        abuf_ref[SUBLANES:SUBLANES + tm, :] = a
        carry_ref[:, cols] = a[tm - SUBLANES:tm, :]
        acc = cb_ref[:, cols] + abuf_ref[SUBLANES - 2:SUBLANES - 2 + tm, :] * cw_ref[0:1, cols]
        acc = acc + abuf_ref[SUBLANES - 1:SUBLANES - 1 + tm, :] * cw_ref[1:2, cols]
        acc = acc + a * cw_ref[2:3, cols]
        gated_ref[:, cols] = (acc * jax.nn.sigmoid(acc) * b).astype(BF16)
    h2 = h1 + jnp.dot(gated_ref[...], wd_ref[...], preferred_element_type=F32)
    o_ref[0] = (h2 * _rms(h2)) * gf_ref[...]


def _ffn(x, o_r, o_f, w_out, g2, w_up, conv_w, conv_b, w_down, gf, halo, *, tm):
    nb, rows, _ = x.shape
    row_blk = lambda w: pl.BlockSpec((1, tm, w), lambda b, j: (b, j, 0))
    const = lambda shape: pl.BlockSpec(shape, lambda b, j: (0,) * len(shape), pipeline_mode=pl.Buffered(1))
    return pl.pallas_call(
        functools.partial(_ffn_kernel, tm=tm),
        grid=(nb, rows // tm),
        in_specs=[row_blk(D_MODEL), row_blk(RET_V), row_blk(FOX_W), const((D_MODEL, D_MODEL)),
                  const((1, D_MODEL)), const((D_MODEL, 2 * D_FF)), const((CONV_W, D_FF)), const((1, D_FF)),
                  const((D_FF, D_MODEL)), const((1, D_MODEL)), const((SUBLANES, D_FF))],
        out_specs=row_blk(D_MODEL),
        out_shape=jax.ShapeDtypeStruct((nb, rows, D_MODEL), F32),
        scratch_shapes=[pltpu.VMEM((SUBLANES, D_FF), F32), pltpu.VMEM((SUBLANES + tm, FF_CHUNK), F32),
                        pltpu.VMEM((tm, D_FF), BF16)],
        compiler_params=pltpu.CompilerParams(dimension_semantics=("arbitrary", "arbitrary"),
                                             vmem_limit_bytes=VMEM_LIMIT),
        name="ffn",
    )(x, o_r, o_f, w_out, g2, w_up, conv_w, conv_b, w_down, gf, halo)


def _rotary_tables(length):
    half = RET_DK // 2
    inv = 1.0 / (ROPE_BASE ** (jnp.arange(half, dtype=F32) / half))
    ang = jnp.arange(length).astype(F32)[:, None] * inv[None, :]
    cos = jnp.tile(jnp.cos(ang), (1, LANES // half))
    sin = jnp.tile(jnp.concatenate([-jnp.sin(ang), jnp.sin(ang)], axis=1), (1, LANES // RET_DK))
    valid = (jnp.arange(length) >= N_PAD).astype(F32)[:, None]
    k_scale = (RET_DK ** -0.5) * valid
    return cos, sin, cos * k_scale, sin * k_scale


def _retention_tables(t):
    log_g = jnp.log1p(-jnp.exp2(-5.0 - jnp.arange(RET_HEADS, dtype=F32)))
    n = jnp.arange(t, dtype=F32)
    visible = (jnp.arange(t)[None, :] // CHUNK) <= (jnp.arange(t)[:, None] // CHUNK)
    d = jnp.exp(jnp.abs(n[:, None] - n[None, :])[None] * log_g[:, None, None]) * visible[None]
    lane_log_g = jnp.repeat(log_g, RET_DK).reshape(RET_HEADS // 2, 1, LANES)
    wq = jnp.exp((n + 1.0)[None, :, None] * lane_log_g)
    wk = jnp.exp((t - 1.0 - n)[None, :, None] * lane_log_g)
    gt = jnp.broadcast_to(jnp.exp(t * log_g)[:, None, None], (RET_HEADS, 1, RET_DV))
    return d, wq, wk, gt


def kernel(x, meta_tokens, attn_norm_g, w_in, fox_forget_b, ret_norm_g, w_out, ffn_norm_g, w_up, conv_w,
           conv_b, w_down, final_norm_g):
    assert w_in.shape[0] == 1, "single-layer block"
    batch, seq, _ = x.shape
    w_main = w_in[0, :, :MAIN_W].astype(BF16)
    w_ff = jnp.pad(w_in[0, :, MAIN_W:], ((0, 0), (0, LANES - FOX_HEADS))).astype(BF16)
    fb = jnp.pad(fox_forget_b[0], (0, LANES - FOX_HEADS)).reshape(1, LANES)
    g1 = attn_norm_g[0].reshape(1, D_MODEL)
    g2 = ffn_norm_g[0].reshape(1, D_MODEL)
    gf = final_norm_g.reshape(1, D_MODEL)
    ng = ret_norm_g[0].reshape(1, RET_V)
    w_out_b = w_out[0].astype(BF16)
    w_up_b = w_up[0].astype(BF16)
    w_down_b = w_down[0].astype(BF16)
    cb = conv_b[0].reshape(1, D_FF)

    rot = _rotary_tables(PREFIX + seq)
    rot_pre = tuple(t[:PREFIX] for t in rot)
    rot_seq = tuple(t[PREFIX:] for t in rot)
    h_pre = jnp.concatenate([jnp.zeros((N_PAD, D_MODEL), x.dtype), meta_tokens.astype(x.dtype)], axis=0)[None]

    zeros_c = jnp.zeros((SUBLANES, LANES), F32)
    rq0, rk0, rv0, rg0, fq0, fk0, fv0, c_pre = _inproj(h_pre, g1, w_main, w_ff, fb, rot_pre, zeros_c,
                                                        tl=PREFIX, n_invalid=N_PAD)
    r_zero = jnp.zeros((RET_HEADS, LANES, RET_DV), F32)
    or0, r_pre = _retention(rq0, rk0, rv0, rg0, r_zero, _retention_tables(PREFIX), ng, t=PREFIX)
    of0 = _fox(fq0, fk0, fv0, None, None, tq=PREFIX)
    a_pre = _ffn_prefix(h_pre, or0, of0, w_out_b, g2, w_up_b)
    halo = a_pre[PREFIX - SUBLANES:]

    rq, rk, rv, rg, fq, fk, fv, _ = _inproj(x, g1, w_main, w_ff, fb, rot_seq, c_pre[0],
                                            tl=1024, n_invalid=0)
    o_r, _ = _retention(rq, rk, rv, rg, r_pre[0], _retention_tables(256), ng, t=256)
    o_f = _fox(fq, fk, fv, fk0, fv0, tq=512)
    return _ffn(x, o_r, o_f, w_out_b, g2, w_up_b, conv_w[0], cb, w_down_b, gf, halo, tm=512)
```

```python
import functools

import jax
import jax.numpy as jnp
from jax import lax
from jax.experimental import pallas as pl
from jax.experimental.pallas import tpu as pltpu

F32 = jnp.float32
BF16 = jnp.bfloat16

D_MODEL = 1024
CHUNK = 64
N_META = 16
PREFIX = 128
N_PAD = PREFIX - N_META
RET_HEADS = 4
RET_DK = 64
RET_DV = 128
FOX_HEADS = 8
FOX_DH = 64
D_FF = 2816
CONV_W = 3
ROPE_BASE = 10000.0
EPS = 1e-6
NEG = -1e30

RET_QK = RET_HEADS * RET_DK
RET_V = RET_HEADS * RET_DV
FOX_W = FOX_HEADS * FOX_DH
MAIN_W = 2 * RET_QK + 2 * RET_V + 3 * FOX_W
LANES = 128
SUBLANES = 8
N_EXTRA = 7
LOG2E = 1.4426950408889634
CUMSUM_ROWS = 256
VMEM_LIMIT = 56 * 1024 * 1024
FF_CHUNK = 256
RET_UNROLL = 4


def _rms(x):
    return lax.rsqrt(jnp.mean(x * x, axis=-1, keepdims=True) + EPS)


def _split3(c):
    hi = c.astype(BF16).astype(F32)
    r = c - hi
    mid = r.astype(BF16).astype(F32)
    return hi, mid, r - mid


def _inproj_kernel(x_ref, g_ref, wm_ref, wf_ref, fb_ref, cq_ref, sq_ref, ck_ref, sk_ref, c0_ref,
                   rq_ref, rk_ref, rv_ref, rg_ref, fq_ref, fk_ref, fv_ref, cl_ref, carry_ref,
                   *, tl, n_invalid):
    j = pl.program_id(1)

    @pl.when(j == 0)
    def _():
        carry_ref[...] = c0_ref[...]

    x = x_ref[0]
    hn = ((x * _rms(x)) * g_ref[...]).astype(BF16)

    def proj(lo, width):
        return jnp.dot(hn, wm_ref[:, lo:lo + width], preferred_element_type=F32)

    lane = lax.broadcasted_iota(jnp.int32, (tl, LANES), 1)
    first_half = (lane & 32) == 0

    rqk = proj(0, 2 * RET_QK)
    for base, c_ref, s_ref, o_ref in ((0, cq_ref, sq_ref, rq_ref), (RET_QK, ck_ref, sk_ref, rk_ref)):
        for blk in range(RET_QK // LANES):
            xb = rqk[:, base + LANES * blk:base + LANES * (blk + 1)]
            swapped = jnp.where(first_half, pltpu.roll(xb, LANES - 32, 1), pltpu.roll(xb, 32, 1))
            o_ref[0, :, LANES * blk:LANES * (blk + 1)] = (xb * c_ref[...] + swapped * s_ref[...]).astype(BF16)

    rv_ref[0] = proj(2 * RET_QK, RET_V).astype(BF16)
    rg = proj(2 * RET_QK + RET_V, RET_V)
    rg_ref[0] = (rg * jax.nn.sigmoid(rg)).astype(BF16)

    ff = jnp.dot(hn, wf_ref[...], preferred_element_type=F32) + fb_ref[...]
    lf = jnp.minimum(ff, 0.0) - jnp.log1p(jnp.exp(-jnp.abs(ff)))
    head_lane = lane < FOX_HEADS
    hi, mid, lo = (jnp.where(head_lane, part, 0.0) for part in _split3(lf))
    packed = (hi + pltpu.roll(mid, FOX_HEADS, 1) + pltpu.roll(lo, 2 * FOX_HEADS, 1)).astype(BF16)
    sub = min(tl, CUMSUM_ROWS)
    tri = lax.broadcasted_iota(jnp.int32, (sub, sub), 0) >= lax.broadcasted_iota(jnp.int32, (sub, sub), 1)
    tri = jnp.where(tri, 1.0, 0.0).astype(BF16)
    carry = carry_ref[0:1, :]
    c_blocks = []
    for r in range(tl // sub):
        c3 = jnp.dot(tri, packed[r * sub:(r + 1) * sub], preferred_element_type=F32)
        c_blk = (c3 + pltpu.roll(c3, LANES - FOX_HEADS, 1)) + pltpu.roll(c3, LANES - 2 * FOX_HEADS, 1) + carry
        carry = c_blk[sub - 1:sub, :]
        c_blocks.append(c_blk)
    c = jnp.concatenate(c_blocks, axis=0) if len(c_blocks) > 1 else c_blocks[0]
    carry_ref[...] = jnp.broadcast_to(carry, (SUBLANES, LANES))
    cl_ref[0] = carry_ref[...]

    if n_invalid:
        row = j * tl + lax.broadcasted_iota(jnp.int32, (tl, LANES), 0)
        key_mask = jnp.where(row >= n_invalid, 0.0, NEG)
    else:
        key_mask = jnp.zeros((tl, LANES), F32)

    c_parts = _split3(c * LOG2E)
    fox_base = 2 * RET_QK + 2 * RET_V
    fq_all = proj(fox_base, FOX_W) * (FOX_DH ** -0.5 * LOG2E)
    fk_all = proj(fox_base + FOX_W, FOX_W)
    fv_all = proj(fox_base + 2 * FOX_W, FOX_W)
    for pair in range(FOX_HEADS // 2):
        cols = slice(LANES * pair, LANES * (pair + 1))
        fq, fk, fv = fq_all[:, cols], fk_all[:, cols], fv_all[:, cols]
        for hh in range(2):
            h = 2 * pair + hh
            is_data = (lane < FOX_DH) if hh == 0 else (lane >= FOX_DH)
            e = (lane - FOX_DH) if hh == 0 else lane
            hi, mid, lo = (jnp.broadcast_to(part[:, h:h + 1], (tl, LANES)) for part in c_parts)
            ones3 = (e >= 0) & (e < 3)
            qx = jnp.where(e == 0, hi, jnp.where(e == 1, mid, jnp.where(e == 2, lo,
                 jnp.where((e >= 3) & (e < N_EXTRA), 1.0, 0.0))))
            kx = jnp.where(ones3, 1.0, jnp.where(e == 3, -hi, jnp.where(e == 4, -mid,
                 jnp.where(e == 5, -lo, jnp.where(e == 6, key_mask, 0.0)))))
            vx = jnp.where(e == 0, 1.0, 0.0)
            fq_ref[0, h] = jnp.where(is_data, fq, qx).astype(BF16)
            fk_ref[0, h] = jnp.where(is_data, fk, kx).astype(BF16)
            fv_ref[0, h] = jnp.where(is_data, fv, vx).astype(BF16)


def _inproj(x, g, w_main, w_ff, fb, tabs, c0, *, tl, n_invalid):
    nb, rows, _ = x.shape
    grid = (nb, rows // tl)
    row_blk = lambda w: pl.BlockSpec((1, tl, w), lambda b, j: (b, j, 0))
    head_blk = pl.BlockSpec((1, FOX_HEADS, tl, LANES), lambda b, j: (b, 0, j, 0))
    const = lambda shape: pl.BlockSpec(shape, lambda b, j: (0,) * len(shape), pipeline_mode=pl.Buffered(1))
    tab_blk = pl.BlockSpec((tl, LANES), lambda b, j: (j, 0))
    out_shape = (
        jax.ShapeDtypeStruct((nb, rows, RET_QK), BF16), jax.ShapeDtypeStruct((nb, rows, RET_QK), BF16),
        jax.ShapeDtypeStruct((nb, rows, RET_V), BF16), jax.ShapeDtypeStruct((nb, rows, RET_V), BF16),
        jax.ShapeDtypeStruct((nb, FOX_HEADS, rows, LANES), BF16),
        jax.ShapeDtypeStruct((nb, FOX_HEADS, rows, LANES), BF16),
        jax.ShapeDtypeStruct((nb, FOX_HEADS, rows, LANES), BF16),
        jax.ShapeDtypeStruct((nb, SUBLANES, LANES), F32),
    )
    return pl.pallas_call(
        functools.partial(_inproj_kernel, tl=tl, n_invalid=n_invalid),
        grid=grid,
        in_specs=[row_blk(D_MODEL), const((1, D_MODEL)), const((D_MODEL, MAIN_W)), const((D_MODEL, LANES)),
                  const((1, LANES)), tab_blk, tab_blk, tab_blk, tab_blk, const((SUBLANES, LANES))],
        out_specs=(row_blk(RET_QK), row_blk(RET_QK), row_blk(RET_V), row_blk(RET_V),
                   head_blk, head_blk, head_blk, pl.BlockSpec((1, SUBLANES, LANES), lambda b, j: (b, 0, 0))),
        out_shape=out_shape,
        scratch_shapes=[pltpu.VMEM((SUBLANES, LANES), F32)],
        compiler_params=pltpu.CompilerParams(dimension_semantics=("arbitrary", "arbitrary"),
                                             vmem_limit_bytes=VMEM_LIMIT),
        name=f"inproj_{rows}",
    )(x, g, w_main, w_ff, fb, *tabs, c0)


def _retention_kernel(q_ref, k_ref, v_ref, gate_ref, r0_ref, d_ref, wq_ref, wk_ref, gt_ref, ng_ref,
                      o_ref, rf_ref, state_ref, *, t, n_blocks):
    state_ref[...] = r0_ref[...]
    lane = lax.broadcasted_iota(jnp.int32, (t, LANES), 1)

    def block(i, carry):
        rows = pl.ds(pl.multiple_of(i * t, t), t)
        q = q_ref[0, rows, :]
        k = k_ref[0, rows, :]
        v = v_ref[0, rows, :]
        qw = (q.astype(F32) * wq_ref[0]).astype(BF16)
        kw = (k.astype(F32) * wk_ref[0]).astype(BF16)
        for hh in range(2):
            in_head = (lane < RET_DK) if hh == 0 else (lane >= RET_DK)
            vh = v[:, RET_DV * hh:RET_DV * (hh + 1)]
            qm = jnp.where(in_head, q, jnp.zeros_like(q))
            s = lax.dot_general(qm, k, (((1,), (1,)), ((), ())), preferred_element_type=F32)
            intra = jnp.dot((s * d_ref[hh]).astype(BF16), vh, preferred_element_type=F32)
            r = state_ref[hh]
            inter = jnp.dot(qw, r.astype(BF16), preferred_element_type=F32)
            kwm = jnp.where(in_head, kw, jnp.zeros_like(kw))
            u = lax.dot_general(kwm, vh, (((0,), (0,)), ((), ())), preferred_element_type=F32)
            state_ref[hh] = gt_ref[hh] * r + u
            o = intra + inter
            cols = slice(RET_DV * hh, RET_DV * (hh + 1))
            o = (o * _rms(o)) * ng_ref[:, cols] * gate_ref[0, rows, cols].astype(F32)
            o_ref[0, rows, cols] = o.astype(BF16)
        return carry

    lax.fori_loop(0, n_blocks, block, 0, unroll=min(n_blocks, RET_UNROLL))
    rf_ref[0] = state_ref[...]


def _retention(rq, rk, rv, gate, r0, tabs, norm_g, *, t):
    nb, rows, _ = rq.shape
    d_tab, wq_tab, wk_tab, gt_tab = tabs
    n_pairs = RET_HEADS // 2
    seq = lambda w: pl.BlockSpec((1, rows, w), lambda b, p: (b, 0, p))
    return pl.pallas_call(
        functools.partial(_retention_kernel, t=t, n_blocks=rows // t),
        grid=(nb, n_pairs),
        in_specs=[seq(LANES), seq(LANES), seq(2 * RET_DV), seq(2 * RET_DV),
                  pl.BlockSpec((2, LANES, RET_DV), lambda b, p: (p, 0, 0)),
                  pl.BlockSpec((2, t, t), lambda b, p: (p, 0, 0)),
                  pl.BlockSpec((1, t, LANES), lambda b, p: (p, 0, 0)),
                  pl.BlockSpec((1, t, LANES), lambda b, p: (p, 0, 0)),
                  pl.BlockSpec((2, 1, RET_DV), lambda b, p: (p, 0, 0)),
                  pl.BlockSpec((1, 2 * RET_DV), lambda b, p: (0, p))],
        out_specs=(seq(2 * RET_DV), pl.BlockSpec((1, 2, LANES, RET_DV), lambda b, p: (b, p, 0, 0))),
        out_shape=(jax.ShapeDtypeStruct((nb, rows, RET_V), BF16),
                   jax.ShapeDtypeStruct((nb, RET_HEADS, LANES, RET_DV), F32)),
        scratch_shapes=[pltpu.VMEM((2, LANES, RET_DV), F32)],
        compiler_params=pltpu.CompilerParams(dimension_semantics=("arbitrary", "arbitrary"),
                                             vmem_limit_bytes=VMEM_LIMIT),
        name=f"retention_{rows}",
    )(rq, rk, rv, gate, r0, d_tab, wq_tab, wk_tab, gt_tab, norm_g)


def _fox_kernel(*refs, tq, tk, n_tiles, has_prefix):
    if has_prefix:
        q_ref, k_ref, v_ref, kp_ref, vp_ref, o_ref, m_ref, acc_ref = refs
    else:
        q_ref, k_ref, v_ref, o_ref, m_ref, acc_ref = refs
    n_sub = tq // tk
    lane = lax.broadcasted_iota(jnp.int32, (tq, LANES), 1)

    def scores(hh, rows, k):
        return lax.dot_general(q_ref[0, hh, rows, :], k, (((1,), (1,)), ((), ())), preferred_element_type=F32)

    def probs(s, m):
        return jnp.exp2((s - jnp.tile(m, (1, s.shape[1] // LANES))).astype(BF16))

    def q_tile(i, carry):
        row0 = pl.multiple_of(i * tq, tq)
        rows = pl.ds(row0, tq)
        for hh in range(2):
            for r in range(n_sub):
                sub_rows = pl.ds(row0 + r * tk, tk)
                keys = pl.ds(row0, (r + 1) * tk)
                shape = (tk, (r + 1) * tk)
                visible = (lax.broadcasted_iota(jnp.int32, shape, 1)
                           <= lax.broadcasted_iota(jnp.int32, shape, 0) + r * tk)
                s = jnp.where(visible, scores(hh, sub_rows, k_ref[0, hh, keys, :]), NEG)
                row_max = jnp.max(s, axis=1, keepdims=True)
                if has_prefix:
                    s_pre = scores(hh, sub_rows, kp_ref[0, hh])
                    row_max = jnp.maximum(row_max, jnp.max(s_pre, axis=1, keepdims=True))
                m = jnp.broadcast_to(row_max, (tk, LANES))
                acc = jnp.dot(probs(s, m), v_ref[0, hh, keys, :], preferred_element_type=F32)
                if has_prefix:
                    acc = acc + jnp.dot(probs(s_pre, m), vp_ref[0, hh], preferred_element_type=F32)
                m_ref[hh, r * tk:(r + 1) * tk] = m
                acc_ref[hh, r * tk:(r + 1) * tk] = acc

        def k_chunk(jj, c):
            kr = pl.ds(pl.multiple_of(jj * tk, tk), tk)
            for hh in range(2):
                s = scores(hh, rows, k_ref[0, hh, kr, :])
                m_old = m_ref[hh]
                m_new = jnp.maximum(m_old, jnp.max(s, axis=1, keepdims=True))
                m_ref[hh] = m_new
                pv = jnp.dot(probs(s, m_new), v_ref[0, hh, kr, :], preferred_element_type=F32)
                acc_ref[hh] = jnp.exp2(m_old - m_new) * acc_ref[hh] + pv
            return c

        lax.fori_loop(0, i * n_sub, k_chunk, 0)
        outs = []
        for hh in range(2):
            acc = acc_ref[hh]
            sum_lane = FOX_DH if hh == 0 else 0
            outs.append(acc / acc[:, sum_lane:sum_lane + 1])
        o_ref[0, rows, :] = jnp.where(lane < FOX_DH, outs[0], outs[1]).astype(BF16)
        return carry

    lax.fori_loop(0, n_tiles, q_tile, 0)


def _fox(fq, fk, fv, kp, vp, *, tq, tk):
    nb, _, rows, _ = fq.shape
    has_prefix = kp is not None
    pair_blk = pl.BlockSpec((1, 2, rows, LANES), lambda b, p: (b, p, 0, 0))
    in_specs = [pair_blk, pair_blk, pair_blk]
    args = [fq, fk, fv]
    if has_prefix:
        pre_blk = pl.BlockSpec((1, 2, PREFIX, LANES), lambda b, p: (0, p, 0, 0))
        in_specs += [pre_blk, pre_blk]
        args += [kp, vp]
    return pl.pallas_call(
        functools.partial(_fox_kernel, tq=tq, tk=tk, n_tiles=rows // tq, has_prefix=has_prefix),
        grid=(nb, FOX_HEADS // 2),
        in_specs=in_specs,
        out_specs=pl.BlockSpec((1, rows, LANES), lambda b, p: (b, 0, p)),
        out_shape=jax.ShapeDtypeStruct((nb, rows, FOX_W), BF16),
        scratch_shapes=[pltpu.VMEM((2, tq, LANES), F32), pltpu.VMEM((2, tq, LANES), F32)],
        compiler_params=pltpu.CompilerParams(dimension_semantics=("arbitrary", "arbitrary"),
                                             vmem_limit_bytes=VMEM_LIMIT),
        name=f"fox_{rows}",
    )(*args)


def _mixer_residual(x_ref, or_ref, of_ref, wo_ref):
    y = jnp.dot(or_ref[0], wo_ref[0:RET_V, :], preferred_element_type=F32)
    y = y + jnp.dot(of_ref[0], wo_ref[RET_V:RET_V + FOX_W, :], preferred_element_type=F32)
    return x_ref[0] + y


def _ffn_prefix_kernel(x_ref, or_ref, of_ref, wo_ref, g2_ref, wa_ref, a_ref, *, n_invalid):
    h1 = _mixer_residual(x_ref, or_ref, of_ref, wo_ref)
    hn = ((h1 * _rms(h1)) * g2_ref[...]).astype(BF16)
    a = jnp.dot(hn, wa_ref[...], preferred_element_type=F32)
    row = lax.broadcasted_iota(jnp.int32, a.shape, 0)
    a_ref[...] = jnp.where(row >= n_invalid, a, 0.0)


def _ffn_prefix(x, o_r, o_f, w_out, g2, w_up):
    full = lambda shape: pl.BlockSpec(shape, lambda i: (0,) * len(shape))
    return pl.pallas_call(
        functools.partial(_ffn_prefix_kernel, n_invalid=N_PAD),
        grid=(1,),
        in_specs=[full((1, PREFIX, D_MODEL)), full((1, PREFIX, RET_V)), full((1, PREFIX, FOX_W)),
                  full((D_MODEL, D_MODEL)), full((1, D_MODEL)), full((D_MODEL, D_FF))],
        out_specs=full((PREFIX, D_FF)),
        out_shape=jax.ShapeDtypeStruct((PREFIX, D_FF), F32),
        compiler_params=pltpu.CompilerParams(vmem_limit_bytes=VMEM_LIMIT),
        name="ffn_prefix",
    )(x, o_r, o_f, w_out, g2, w_up)


def _ffn_kernel(x_ref, or_ref, of_ref, wo_ref, g2_ref, wu_ref, cw_ref, cb_ref, wd_ref, gf_ref, halo_ref,
                o_ref, carry_ref, abuf_ref, gated_ref, *, tm):
    @pl.when(pl.program_id(1) == 0)
    def _():
        carry_ref[...] = halo_ref[...]

    h1 = _mixer_residual(x_ref, or_ref, of_ref, wo_ref)
    hn = ((h1 * _rms(h1)) * g2_ref[...]).astype(BF16)
    for c in range(D_FF // FF_CHUNK):
        cols = slice(c * FF_CHUNK, (c + 1) * FF_CHUNK)
        a = jnp.dot(hn, wu_ref[:, cols], preferred_element_type=F32)
        b = jnp.dot(hn, wu_ref[:, D_FF + c * FF_CHUNK:D_FF + (c + 1) * FF_CHUNK], preferred_element_type=F32)
        abuf_ref[0:SUBLANES, :] = carry_ref[:, cols]
        abuf_ref[SUBLANES:SUBLANES + tm, :] = a
        carry_ref[:, cols] = a[tm - SUBLANES:tm, :]
        acc = cb_ref[:, cols] + abuf_ref[SUBLANES - 2:SUBLANES - 2 + tm, :] * cw_ref[0:1, cols]
        acc = acc + abuf_ref[SUBLANES - 1:SUBLANES - 1 + tm, :] * cw_ref[1:2, cols]
        acc = acc + a * cw_ref[2:3, cols]
        gated_ref[:, cols] = (acc * jax.nn.sigmoid(acc) * b).astype(BF16)
    h2 = h1 + jnp.dot(gated_ref[...], wd_ref[...], preferred_element_type=F32)
    o_ref[0] = (h2 * _rms(h2)) * gf_ref[...]


def _ffn(x, o_r, o_f, w_out, g2, w_up, conv_w, conv_b, w_down, gf, halo, *, tm):
    nb, rows, _ = x.shape
    row_blk = lambda w: pl.BlockSpec((1, tm, w), lambda b, j: (b, j, 0))
    const = lambda shape: pl.BlockSpec(shape, lambda b, j: (0,) * len(shape), pipeline_mode=pl.Buffered(1))
    return pl.pallas_call(
        functools.partial(_ffn_kernel, tm=tm),
        grid=(nb, rows // tm),
        in_specs=[row_blk(D_MODEL), row_blk(RET_V), row_blk(FOX_W), const((D_MODEL, D_MODEL)),
                  const((1, D_MODEL)), const((D_MODEL, 2 * D_FF)), const((CONV_W, D_FF)), const((1, D_FF)),
                  const((D_FF, D_MODEL)), const((1, D_MODEL)), const((SUBLANES, D_FF))],
        out_specs=row_blk(D_MODEL),
        out_shape=jax.ShapeDtypeStruct((nb, rows, D_MODEL), F32),
        scratch_shapes=[pltpu.VMEM((SUBLANES, D_FF), F32), pltpu.VMEM((SUBLANES + tm, FF_CHUNK), F32),
                        pltpu.VMEM((tm, D_FF), BF16)],
        compiler_params=pltpu.CompilerParams(dimension_semantics=("arbitrary", "arbitrary"),
                                             vmem_limit_bytes=VMEM_LIMIT),
        name="ffn",
    )(x, o_r, o_f, w_out, g2, w_up, conv_w, conv_b, w_down, gf, halo)


def _rotary_tables(length):
    half = RET_DK // 2
    inv = 1.0 / (ROPE_BASE ** (jnp.arange(half, dtype=F32) / half))
    ang = jnp.arange(length).astype(F32)[:, None] * inv[None, :]
    cos = jnp.tile(jnp.cos(ang), (1, LANES // half))
    sin = jnp.tile(jnp.concatenate([-jnp.sin(ang), jnp.sin(ang)], axis=1), (1, LANES // RET_DK))
    valid = (jnp.arange(length) >= N_PAD).astype(F32)[:, None]
    k_scale = (RET_DK ** -0.5) * valid
    return cos, sin, cos * k_scale, sin * k_scale


def _retention_tables(t):
    log_g = jnp.log1p(-jnp.exp2(-5.0 - jnp.arange(RET_HEADS, dtype=F32)))
    n = jnp.arange(t, dtype=F32)
    visible = (jnp.arange(t)[None, :] // CHUNK) <= (jnp.arange(t)[:, None] // CHUNK)
    d = jnp.exp(jnp.abs(n[:, None] - n[None, :])[None] * log_g[:, None, None]) * visible[None]
    lane_log_g = jnp.repeat(log_g, RET_DK).reshape(RET_HEADS // 2, 1, LANES)
    wq = jnp.exp((n + 1.0)[None, :, None] * lane_log_g)
    wk = jnp.exp((t - 1.0 - n)[None, :, None] * lane_log_g)
    gt = jnp.broadcast_to(jnp.exp(t * log_g)[:, None, None], (RET_HEADS, 1, RET_DV))
    return d, wq, wk, gt


def kernel(x, meta_tokens, attn_norm_g, w_in, fox_forget_b, ret_norm_g, w_out, ffn_norm_g, w_up, conv_w,
           conv_b, w_down, final_norm_g):
    assert w_in.shape[0] == 1, "single-layer block"
    batch, seq, _ = x.shape
    w_main = w_in[0, :, :MAIN_W].astype(BF16)
    w_ff = jnp.pad(w_in[0, :, MAIN_W:], ((0, 0), (0, LANES - FOX_HEADS))).astype(BF16)
    fb = jnp.pad(fox_forget_b[0], (0, LANES - FOX_HEADS)).reshape(1, LANES)
    g1 = attn_norm_g[0].reshape(1, D_MODEL)
    g2 = ffn_norm_g[0].reshape(1, D_MODEL)
    gf = final_norm_g.reshape(1, D_MODEL)
    ng = ret_norm_g[0].reshape(1, RET_V)
    w_out_b = w_out[0].astype(BF16)
    w_up_b = w_up[0].astype(BF16)
    w_down_b = w_down[0].astype(BF16)
    cb = conv_b[0].reshape(1, D_FF)

    rot = _rotary_tables(PREFIX + seq)
    rot_pre = tuple(t[:PREFIX] for t in rot)
    rot_seq = tuple(t[PREFIX:] for t in rot)
    h_pre = jnp.concatenate([jnp.zeros((N_PAD, D_MODEL), x.dtype), meta_tokens.astype(x.dtype)], axis=0)[None]

    zeros_c = jnp.zeros((SUBLANES, LANES), F32)
    rq0, rk0, rv0, rg0, fq0, fk0, fv0, c_pre = _inproj(h_pre, g1, w_main, w_ff, fb, rot_pre, zeros_c,
                                                        tl=PREFIX, n_invalid=N_PAD)
    r_zero = jnp.zeros((RET_HEADS, LANES, RET_DV), F32)
    or0, r_pre = _retention(rq0, rk0, rv0, rg0, r_zero, _retention_tables(PREFIX), ng, t=PREFIX)
    of0 = _fox(fq0, fk0, fv0, None, None, tq=PREFIX, tk=PREFIX)
    a_pre = _ffn_prefix(h_pre, or0, of0, w_out_b, g2, w_up_b)
    halo = a_pre[PREFIX - SUBLANES:]

    rq, rk, rv, rg, fq, fk, fv, _ = _inproj(x, g1, w_main, w_ff, fb, rot_seq, c_pre[0],
                                            tl=1024, n_invalid=0)
    o_r, _ = _retention(rq, rk, rv, rg, r_pre[0], _retention_tables(256), ng, t=256)
    o_f = _fox(fq, fk, fv, fk0, fv0, tq=2048, tk=512)
    return _ffn(x, o_r, o_f, w_out_b, g2, w_up_b, conv_w[0], cb, w_down_b, gf, halo, tm=512)
```

```python
import functools

import jax
import jax.numpy as jnp
from jax import lax
from jax.experimental import pallas as pl
from jax.experimental.pallas import tpu as pltpu

F32 = jnp.float32
BF16 = jnp.bfloat16

D_MODEL = 1024
CHUNK = 64
N_META = 16
PREFIX = 128
N_PAD = PREFIX - N_META
RET_HEADS = 4
RET_DK = 64
RET_DV = 128
FOX_HEADS = 8
FOX_DH = 64
D_FF = 2816
CONV_W = 3
ROPE_BASE = 10000.0
EPS = 1e-6
NEG = -1e30

RET_QK = RET_HEADS * RET_DK
RET_V = RET_HEADS * RET_DV
FOX_W = FOX_HEADS * FOX_DH
MAIN_W = 2 * RET_QK + 2 * RET_V + 3 * FOX_W
LANES = 128
SUBLANES = 8
N_EXTRA = 7
LOG2E = 1.4426950408889634
CUMSUM_ROWS = 256
VMEM_LIMIT = 56 * 1024 * 1024
FF_CHUNK = 256
RET_UNROLL = 4


def _rms(x):
    return lax.rsqrt(jnp.mean(x * x, axis=-1, keepdims=True) + EPS)


def _split3(c):
    hi = c.astype(BF16).astype(F32)
    r = c - hi
    mid = r.astype(BF16).astype(F32)
    return hi, mid, r - mid


def _inproj_kernel(x_ref, g_ref, wm_ref, wf_ref, fb_ref, cq_ref, sq_ref, ck_ref, sk_ref, c0_ref,
                   rq_ref, rk_ref, rv_ref, rg_ref, fq_ref, fk_ref, fv_ref, cl_ref, carry_ref,
                   *, tl, n_invalid):
    j = pl.program_id(1)

    @pl.when(j == 0)
    def _():
        carry_ref[...] = c0_ref[...]

    x = x_ref[0]
    hn = ((x * _rms(x)) * g_ref[...]).astype(BF16)

    def proj(lo, width):
        return jnp.dot(hn, wm_ref[:, lo:lo + width], preferred_element_type=F32)

    lane = lax.broadcasted_iota(jnp.int32, (tl, LANES), 1)
    first_half = (lane & 32) == 0

    rqk = proj(0, 2 * RET_QK)
    for base, c_ref, s_ref, o_ref in ((0, cq_ref, sq_ref, rq_ref), (RET_QK, ck_ref, sk_ref, rk_ref)):
        for blk in range(RET_QK // LANES):
            xb = rqk[:, base + LANES * blk:base + LANES * (blk + 1)]
            swapped = jnp.where(first_half, pltpu.roll(xb, LANES - 32, 1), pltpu.roll(xb, 32, 1))
            o_ref[0, :, LANES * blk:LANES * (blk + 1)] = (xb * c_ref[...] + swapped * s_ref[...]).astype(BF16)

    rv_ref[0] = proj(2 * RET_QK, RET_V).astype(BF16)
    rg = proj(2 * RET_QK + RET_V, RET_V)
    rg_ref[0] = (rg * jax.nn.sigmoid(rg)).astype(BF16)

    ff = jnp.dot(hn, wf_ref[...], preferred_element_type=F32) + fb_ref[...]
    lf = jnp.minimum(ff, 0.0) - jnp.log1p(jnp.exp(-jnp.abs(ff)))
    head_lane = lane < FOX_HEADS
    hi, mid, lo = (jnp.where(head_lane, part, 0.0) for part in _split3(lf))
    packed = (hi + pltpu.roll(mid, FOX_HEADS, 1) + pltpu.roll(lo, 2 * FOX_HEADS, 1)).astype(BF16)
    sub = min(tl, CUMSUM_ROWS)
    tri = lax.broadcasted_iota(jnp.int32, (sub, sub), 0) >= lax.broadcasted_iota(jnp.int32, (sub, sub), 1)
    tri = jnp.where(tri, 1.0, 0.0).astype(BF16)
    carry = carry_ref[0:1, :]
    c_blocks = []
    for r in range(tl // sub):
        c3 = jnp.dot(tri, packed[r * sub:(r + 1) * sub], preferred_element_type=F32)
        c_blk = (c3 + pltpu.roll(c3, LANES - FOX_HEADS, 1)) + pltpu.roll(c3, LANES - 2 * FOX_HEADS, 1) + carry
        carry = c_blk[sub - 1:sub, :]
        c_blocks.append(c_blk)
    c = jnp.concatenate(c_blocks, axis=0) if len(c_blocks) > 1 else c_blocks[0]
    carry_ref[...] = jnp.broadcast_to(carry, (SUBLANES, LANES))
    cl_ref[0] = carry_ref[...]

    if n_invalid:
        row = j * tl + lax.broadcasted_iota(jnp.int32, (tl, LANES), 0)
        key_mask = jnp.where(row >= n_invalid, 0.0, NEG)
    else:
        key_mask = jnp.zeros((tl, LANES), F32)

    c_parts = _split3(c * LOG2E)
    fox_base = 2 * RET_QK + 2 * RET_V
    fq_all = proj(fox_base, FOX_W) * (FOX_DH ** -0.5 * LOG2E)
    fk_all = proj(fox_base + FOX_W, FOX_W)
    fv_all = proj(fox_base + 2 * FOX_W, FOX_W)
    for pair in range(FOX_HEADS // 2):
        cols = slice(LANES * pair, LANES * (pair + 1))
        fq, fk, fv = fq_all[:, cols], fk_all[:, cols], fv_all[:, cols]
        for hh in range(2):
            h = 2 * pair + hh
            is_data = (lane < FOX_DH) if hh == 0 else (lane >= FOX_DH)
            e = (lane - FOX_DH) if hh == 0 else lane
            hi, mid, lo = (jnp.broadcast_to(part[:, h:h + 1], (tl, LANES)) for part in c_parts)
            ones3 = (e >= 0) & (e < 3)
            qx = jnp.where(e == 0, hi, jnp.where(e == 1, mid, jnp.where(e == 2, lo,
                 jnp.where((e >= 3) & (e < N_EXTRA), 1.0, 0.0))))
            kx = jnp.where(ones3, 1.0, jnp.where(e == 3, -hi, jnp.where(e == 4, -mid,
                 jnp.where(e == 5, -lo, jnp.where(e == 6, key_mask, 0.0)))))
            vx = jnp.where(e == 0, 1.0, 0.0)
            fq_ref[0, h] = jnp.where(is_data, fq, qx).astype(BF16)
            fk_ref[0, h] = jnp.where(is_data, fk, kx).astype(BF16)
            fv_ref[0, h] = jnp.where(is_data, fv, vx).astype(BF16)


def _inproj(x, g, w_main, w_ff, fb, tabs, c0, *, tl, n_invalid):
    nb, rows, _ = x.shape
    grid = (nb, rows // tl)
    row_blk = lambda w: pl.BlockSpec((1, tl, w), lambda b, j: (b, j, 0))
    head_blk = pl.BlockSpec((1, FOX_HEADS, tl, LANES), lambda b, j: (b, 0, j, 0))
    const = lambda shape: pl.BlockSpec(shape, lambda b, j: (0,) * len(shape), pipeline_mode=pl.Buffered(1))
    tab_blk = pl.BlockSpec((tl, LANES), lambda b, j: (j, 0))
    out_shape = (
        jax.ShapeDtypeStruct((nb, rows, RET_QK), BF16), jax.ShapeDtypeStruct((nb, rows, RET_QK), BF16),
        jax.ShapeDtypeStruct((nb, rows, RET_V), BF16), jax.ShapeDtypeStruct((nb, rows, RET_V), BF16),
        jax.ShapeDtypeStruct((nb, FOX_HEADS, rows, LANES), BF16),
        jax.ShapeDtypeStruct((nb, FOX_HEADS, rows, LANES), BF16),
        jax.ShapeDtypeStruct((nb, FOX_HEADS, rows, LANES), BF16),
        jax.ShapeDtypeStruct((nb, SUBLANES, LANES), F32),
    )
    return pl.pallas_call(
        functools.partial(_inproj_kernel, tl=tl, n_invalid=n_invalid),
        grid=grid,
        in_specs=[row_blk(D_MODEL), const((1, D_MODEL)), const((D_MODEL, MAIN_W)), const((D_MODEL, LANES)),
                  const((1, LANES)), tab_blk, tab_blk, tab_blk, tab_blk, const((SUBLANES, LANES))],
        out_specs=(row_blk(RET_QK), row_blk(RET_QK), row_blk(RET_V), row_blk(RET_V),
                   head_blk, head_blk, head_blk, pl.BlockSpec((1, SUBLANES, LANES), lambda b, j: (b, 0, 0))),
        out_shape=out_shape,
        scratch_shapes=[pltpu.VMEM((SUBLANES, LANES), F32)],
        compiler_params=pltpu.CompilerParams(dimension_semantics=("arbitrary", "arbitrary"),
                                             vmem_limit_bytes=VMEM_LIMIT),
        name=f"inproj_{rows}",
    )(x, g, w_main, w_ff, fb, *tabs, c0)


def _retention_kernel(q_ref, k_ref, v_ref, gate_ref, r0_ref, d_ref, wq_ref, wk_ref, gt_ref, ng_ref,
                      o_ref, rf_ref, state_ref, *, t, n_blocks):
    state_ref[...] = r0_ref[...]
    lane = lax.broadcasted_iota(jnp.int32, (t, LANES), 1)

    def block(i, carry):
        rows = pl.ds(pl.multiple_of(i * t, t), t)
        q = q_ref[0, rows, :]
        k = k_ref[0, rows, :]
        v = v_ref[0, rows, :]
        qw = (q.astype(F32) * wq_ref[0]).astype(BF16)
        kw = (k.astype(F32) * wk_ref[0]).astype(BF16)
        for hh in range(2):
            in_head = (lane < RET_DK) if hh == 0 else (lane >= RET_DK)
            vh = v[:, RET_DV * hh:RET_DV * (hh + 1)]
            qm = jnp.where(in_head, q, jnp.zeros_like(q))
            s = lax.dot_general(qm, k, (((1,), (1,)), ((), ())), preferred_element_type=F32)
            intra = jnp.dot((s * d_ref[hh]).astype(BF16), vh, preferred_element_type=F32)
            r = state_ref[hh]
            inter = jnp.dot(qw, r.astype(BF16), preferred_element_type=F32)
            kwm = jnp.where(in_head, kw, jnp.zeros_like(kw))
            u = lax.dot_general(kwm, vh, (((0,), (0,)), ((), ())), preferred_element_type=F32)
            state_ref[hh] = gt_ref[hh] * r + u
            o = intra + inter
            cols = slice(RET_DV * hh, RET_DV * (hh + 1))
            o = (o * _rms(o)) * ng_ref[:, cols] * gate_ref[0, rows, cols].astype(F32)
            o_ref[0, rows, cols] = o.astype(BF16)
        return carry

    lax.fori_loop(0, n_blocks, block, 0, unroll=min(n_blocks, RET_UNROLL))
    rf_ref[0] = state_ref[...]


def _retention(rq, rk, rv, gate, r0, tabs, norm_g, *, t):
    nb, rows, _ = rq.shape
    d_tab, wq_tab, wk_tab, gt_tab = tabs
    n_pairs = RET_HEADS // 2
    seq = lambda w: pl.BlockSpec((1, rows, w), lambda b, p: (b, 0, p))
    return pl.pallas_call(
        functools.partial(_retention_kernel, t=t, n_blocks=rows // t),
        grid=(nb, n_pairs),
        in_specs=[seq(LANES), seq(LANES), seq(2 * RET_DV), seq(2 * RET_DV),
                  pl.BlockSpec((2, LANES, RET_DV), lambda b, p: (p, 0, 0)),
                  pl.BlockSpec((2, t, t), lambda b, p: (p, 0, 0)),
                  pl.BlockSpec((1, t, LANES), lambda b, p: (p, 0, 0)),
                  pl.BlockSpec((1, t, LANES), lambda b, p: (p, 0, 0)),
                  pl.BlockSpec((2, 1, RET_DV), lambda b, p: (p, 0, 0)),
                  pl.BlockSpec((1, 2 * RET_DV), lambda b, p: (0, p))],
        out_specs=(seq(2 * RET_DV), pl.BlockSpec((1, 2, LANES, RET_DV), lambda b, p: (b, p, 0, 0))),
        out_shape=(jax.ShapeDtypeStruct((nb, rows, RET_V), BF16),
                   jax.ShapeDtypeStruct((nb, RET_HEADS, LANES, RET_DV), F32)),
        scratch_shapes=[pltpu.VMEM((2, LANES, RET_DV), F32)],
        compiler_params=pltpu.CompilerParams(dimension_semantics=("arbitrary", "arbitrary"),
                                             vmem_limit_bytes=VMEM_LIMIT),
        name=f"retention_{rows}",
    )(rq, rk, rv, gate, r0, d_tab, wq_tab, wk_tab, gt_tab, norm_g)


def _fox_kernel(*refs, tq, tk, n_tiles, has_prefix):
    if has_prefix:
        q_ref, k_ref, v_ref, kp_ref, vp_ref, o_ref, m_ref, acc_ref = refs
    else:
        q_ref, k_ref, v_ref, o_ref, m_ref, acc_ref = refs
    n_sub = tq // tk
    lane = lax.broadcasted_iota(jnp.int32, (tq, LANES), 1)
    causal = lax.broadcasted_iota(jnp.int32, (tk, tk), 1) <= lax.broadcasted_iota(jnp.int32, (tk, tk), 0)

    def scores(hh, rows, k):
        return lax.dot_general(q_ref[0, hh, rows, :], k, (((1,), (1,)), ((), ())), preferred_element_type=F32)

    def probs(s, m):
        return jnp.exp2((s - jnp.tile(m, (1, s.shape[1] // LANES))).astype(BF16))

    def q_tile(i, carry):
        row0 = pl.multiple_of(i * tq, tq)
        rows = pl.ds(row0, tq)
        def update(hh, first_row, s, v):
            m_old = m_ref[hh, first_row:tq]
            m_new = jnp.maximum(m_old, jnp.max(s, axis=1, keepdims=True))
            m_ref[hh, first_row:tq] = m_new
            pv = jnp.dot(probs(s, m_new), v, preferred_element_type=F32)
            acc_ref[hh, first_row:tq] = jnp.exp2(m_old - m_new) * acc_ref[hh, first_row:tq] + pv

        if has_prefix:
            for hh in range(2):
                s_pre = scores(hh, rows, kp_ref[0, hh])
                m = jnp.broadcast_to(jnp.max(s_pre, axis=1, keepdims=True), (tq, LANES))
                m_ref[hh] = m
                acc_ref[hh] = jnp.dot(probs(s_pre, m), vp_ref[0, hh], preferred_element_type=F32)
        else:
            m_ref[...] = jnp.full(m_ref.shape, NEG, F32)
            acc_ref[...] = jnp.zeros(acc_ref.shape, F32)

        for c in range(n_sub):
            for hh in range(2):
                sub_rows = pl.ds(row0 + c * tk, tq - c * tk)
                keys = pl.ds(row0 + c * tk, tk)
                s = scores(hh, sub_rows, k_ref[0, hh, keys, :])
                s_top = jnp.where(causal, s[:tk], NEG)
                s = jnp.concatenate([s_top, s[tk:]], axis=0) if c < n_sub - 1 else s_top
                update(hh, c * tk, s, v_ref[0, hh, keys, :])

        def k_chunk(jj, c):
            kr = pl.ds(pl.multiple_of(jj * tk, tk), tk)
            for hh in range(2):
                update(hh, 0, scores(hh, rows, k_ref[0, hh, kr, :]), v_ref[0, hh, kr, :])
            return c

        lax.fori_loop(0, i * n_sub, k_chunk, 0)
        outs = []
        for hh in range(2):
            acc = acc_ref[hh]
            sum_lane = FOX_DH if hh == 0 else 0
            outs.append(acc / acc[:, sum_lane:sum_lane + 1])
        o_ref[0, rows, :] = jnp.where(lane < FOX_DH, outs[0], outs[1]).astype(BF16)
        return carry

    lax.fori_loop(0, n_tiles, q_tile, 0)


def _fox(fq, fk, fv, kp, vp, *, tq, tk):
    nb, _, rows, _ = fq.shape
    has_prefix = kp is not None
    pair_blk = pl.BlockSpec((1, 2, rows, LANES), lambda b, p: (b, p, 0, 0))
    in_specs = [pair_blk, pair_blk, pair_blk]
    args = [fq, fk, fv]
    if has_prefix:
        pre_blk = pl.BlockSpec((1, 2, PREFIX, LANES), lambda b, p: (0, p, 0, 0))
        in_specs += [pre_blk, pre_blk]
        args += [kp, vp]
    return pl.pallas_call(
        functools.partial(_fox_kernel, tq=tq, tk=tk, n_tiles=rows // tq, has_prefix=has_prefix),
        grid=(nb, FOX_HEADS // 2),
        in_specs=in_specs,
        out_specs=pl.BlockSpec((1, rows, LANES), lambda b, p: (b, 0, p)),
        out_shape=jax.ShapeDtypeStruct((nb, rows, FOX_W), BF16),
        scratch_shapes=[pltpu.VMEM((2, tq, LANES), F32), pltpu.VMEM((2, tq, LANES), F32)],
        compiler_params=pltpu.CompilerParams(dimension_semantics=("arbitrary", "arbitrary"),
                                             vmem_limit_bytes=VMEM_LIMIT),
        name=f"fox_{rows}",
    )(*args)


def _mixer_residual(x_ref, or_ref, of_ref, wo_ref):
    y = jnp.dot(or_ref[0], wo_ref[0:RET_V, :], preferred_element_type=F32)
    y = y + jnp.dot(of_ref[0], wo_ref[RET_V:RET_V + FOX_W, :], preferred_element_type=F32)
    return x_ref[0] + y


def _ffn_prefix_kernel(x_ref, or_ref, of_ref, wo_ref, g2_ref, wa_ref, a_ref, *, n_invalid):
    h1 = _mixer_residual(x_ref, or_ref, of_ref, wo_ref)
    hn = ((h1 * _rms(h1)) * g2_ref[...]).astype(BF16)
    a = jnp.dot(hn, wa_ref[...], preferred_element_type=F32)
    row = lax.broadcasted_iota(jnp.int32, a.shape, 0)
    a_ref[...] = jnp.where(row >= n_invalid, a, 0.0)


def _ffn_prefix(x, o_r, o_f, w_out, g2, w_up):
    full = lambda shape: pl.BlockSpec(shape, lambda i: (0,) * len(shape))
    return pl.pallas_call(
        functools.partial(_ffn_prefix_kernel, n_invalid=N_PAD),
        grid=(1,),
        in_specs=[full((1, PREFIX, D_MODEL)), full((1, PREFIX, RET_V)), full((1, PREFIX, FOX_W)),
                  full((D_MODEL, D_MODEL)), full((1, D_MODEL)), full((D_MODEL, D_FF))],
        out_specs=full((PREFIX, D_FF)),
        out_shape=jax.ShapeDtypeStruct((PREFIX, D_FF), F32),
        compiler_params=pltpu.CompilerParams(vmem_limit_bytes=VMEM_LIMIT),
        name="ffn_prefix",
    )(x, o_r, o_f, w_out, g2, w_up)


def _ffn_kernel(x_ref, or_ref, of_ref, wo_ref, g2_ref, wu_ref, cw_ref, cb_ref, wd_ref, gf_ref, halo_ref,
                o_ref, carry_ref, abuf_ref, gated_ref, *, tm):
    @pl.when(pl.program_id(1) == 0)
    def _():
        carry_ref[...] = halo_ref[...]

    h1 = _mixer_residual(x_ref, or_ref, of_ref, wo_ref)
    hn = ((h1 * _rms(h1)) * g2_ref[...]).astype(BF16)
    for c in range(D_FF // FF_CHUNK):
        cols = slice(c * FF_CHUNK, (c + 1) * FF_CHUNK)
        a = jnp.dot(hn, wu_ref[:, cols], preferred_element_type=F32)
        b = jnp.dot(hn, wu_ref[:, D_FF + c * FF_CHUNK:D_FF + (c + 1) * FF_CHUNK], preferred_element_type=F32)
        abuf_ref[0:SUBLANES, :] = carry_ref[:, cols]
        abuf_ref[SUBLANES:SUBLANES + tm, :] = a
        carry_ref[:, cols] = a[tm - SUBLANES:tm, :]
        acc = cb_ref[:, cols] + abuf_ref[SUBLANES - 2:SUBLANES - 2 + tm, :] * cw_ref[0:1, cols]
        acc = acc + abuf_ref[SUBLANES - 1:SUBLANES - 1 + tm, :] * cw_ref[1:2, cols]
        acc = acc + a * cw_ref[2:3, cols]
        gated_ref[:, cols] = (acc * jax.nn.sigmoid(acc) * b).astype(BF16)
    h2 = h1 + jnp.dot(gated_ref[...], wd_ref[...], preferred_element_type=F32)
    o_ref[0] = (h2 * _rms(h2)) * gf_ref[...]


def _ffn(x, o_r, o_f, w_out, g2, w_up, conv_w, conv_b, w_down, gf, halo, *, tm):
    nb, rows, _ = x.shape
    row_blk = lambda w: pl.BlockSpec((1, tm, w), lambda b, j: (b, j, 0))
    const = lambda shape: pl.BlockSpec(shape, lambda b, j: (0,) * len(shape), pipeline_mode=pl.Buffered(1))
    return pl.pallas_call(
        functools.partial(_ffn_kernel, tm=tm),
        grid=(nb, rows // tm),
        in_specs=[row_blk(D_MODEL), row_blk(RET_V), row_blk(FOX_W), const((D_MODEL, D_MODEL)),
                  const((1, D_MODEL)), const((D_MODEL, 2 * D_FF)), const((CONV_W, D_FF)), const((1, D_FF)),
                  const((D_FF, D_MODEL)), const((1, D_MODEL)), const((SUBLANES, D_FF))],
        out_specs=row_blk(D_MODEL),
        out_shape=jax.ShapeDtypeStruct((nb, rows, D_MODEL), F32),
        scratch_shapes=[pltpu.VMEM((SUBLANES, D_FF), F32), pltpu.VMEM((SUBLANES + tm, FF_CHUNK), F32),
                        pltpu.VMEM((tm, D_FF), BF16)],
        compiler_params=pltpu.CompilerParams(dimension_semantics=("arbitrary", "arbitrary"),
                                             vmem_limit_bytes=VMEM_LIMIT),
        name="ffn",
    )(x, o_r, o_f, w_out, g2, w_up, conv_w, conv_b, w_down, gf, halo)


def _rotary_tables(length):
    half = RET_DK // 2
    inv = 1.0 / (ROPE_BASE ** (jnp.arange(half, dtype=F32) / half))
    ang = jnp.arange(length).astype(F32)[:, None] * inv[None, :]
    cos = jnp.tile(jnp.cos(ang), (1, LANES // half))
    sin = jnp.tile(jnp.concatenate([-jnp.sin(ang), jnp.sin(ang)], axis=1), (1, LANES // RET_DK))
    valid = (jnp.arange(length) >= N_PAD).astype(F32)[:, None]
    k_scale = (RET_DK ** -0.5) * valid
    return cos, sin, cos * k_scale, sin * k_scale


def _retention_tables(t):
    log_g = jnp.log1p(-jnp.exp2(-5.0 - jnp.arange(RET_HEADS, dtype=F32)))
    n = jnp.arange(t, dtype=F32)
    visible = (jnp.arange(t)[None, :] // CHUNK) <= (jnp.arange(t)[:, None] // CHUNK)
    d = jnp.exp(jnp.abs(n[:, None] - n[None, :])[None] * log_g[:, None, None]) * visible[None]
    lane_log_g = jnp.repeat(log_g, RET_DK).reshape(RET_HEADS // 2, 1, LANES)
    wq = jnp.exp((n + 1.0)[None, :, None] * lane_log_g)
    wk = jnp.exp((t - 1.0 - n)[None, :, None] * lane_log_g)
    gt = jnp.broadcast_to(jnp.exp(t * log_g)[:, None, None], (RET_HEADS, 1, RET_DV))
    return d, wq, wk, gt


def kernel(x, meta_tokens, attn_norm_g, w_in, fox_forget_b, ret_norm_g, w_out, ffn_norm_g, w_up, conv_w,
           conv_b, w_down, final_norm_g):
    assert w_in.shape[0] == 1, "single-layer block"
    batch, seq, _ = x.shape
    w_main = w_in[0, :, :MAIN_W].astype(BF16)
    w_ff = jnp.pad(w_in[0, :, MAIN_W:], ((0, 0), (0, LANES - FOX_HEADS))).astype(BF16)
    fb = jnp.pad(fox_forget_b[0], (0, LANES - FOX_HEADS)).reshape(1, LANES)
    g1 = attn_norm_g[0].reshape(1, D_MODEL)
    g2 = ffn_norm_g[0].reshape(1, D_MODEL)
    gf = final_norm_g.reshape(1, D_MODEL)
    ng = ret_norm_g[0].reshape(1, RET_V)
    w_out_b = w_out[0].astype(BF16)
    w_up_b = w_up[0].astype(BF16)
    w_down_b = w_down[0].astype(BF16)
    cb = conv_b[0].reshape(1, D_FF)

    rot = _rotary_tables(PREFIX + seq)
    rot_pre = tuple(t[:PREFIX] for t in rot)
    rot_seq = tuple(t[PREFIX:] for t in rot)
    h_pre = jnp.concatenate([jnp.zeros((N_PAD, D_MODEL), x.dtype), meta_tokens.astype(x.dtype)], axis=0)[None]

    zeros_c = jnp.zeros((SUBLANES, LANES), F32)
    rq0, rk0, rv0, rg0, fq0, fk0, fv0, c_pre = _inproj(h_pre, g1, w_main, w_ff, fb, rot_pre, zeros_c,
                                                        tl=PREFIX, n_invalid=N_PAD)
    r_zero = jnp.zeros((RET_HEADS, LANES, RET_DV), F32)
    or0, r_pre = _retention(rq0, rk0, rv0, rg0, r_zero, _retention_tables(PREFIX), ng, t=PREFIX)
    of0 = _fox(fq0, fk0, fv0, None, None, tq=PREFIX, tk=PREFIX)
    a_pre = _ffn_prefix(h_pre, or0, of0, w_out_b, g2, w_up_b)
    halo = a_pre[PREFIX - SUBLANES:]

    rq, rk, rv, rg, fq, fk, fv, _ = _inproj(x, g1, w_main, w_ff, fb, rot_seq, c_pre[0],
                                            tl=1024, n_invalid=0)
    o_r, _ = _retention(rq, rk, rv, rg, r_pre[0], _retention_tables(256), ng, t=256)
    o_f = _fox(fq, fk, fv, fk0, fv0, tq=2048, tk=512)
    return _ffn(x, o_r, o_f, w_out_b, g2, w_up_b, conv_w[0], cb, w_down_b, gf, halo, tm=1024)
```

```python
import functools

import jax
import jax.numpy as jnp
from jax import lax
from jax.experimental import pallas as pl
from jax.experimental.pallas import tpu as pltpu

F32 = jnp.float32
BF16 = jnp.bfloat16

D_MODEL = 1024
CHUNK = 64
N_META = 16
PREFIX = 128
N_PAD = PREFIX - N_META
RET_HEADS = 4
RET_DK = 64
RET_DV = 128
FOX_HEADS = 8
FOX_DH = 64
D_FF = 2816
CONV_W = 3
ROPE_BASE = 10000.0
EPS = 1e-6
NEG = -1e30

RET_QK = RET_HEADS * RET_DK
RET_V = RET_HEADS * RET_DV
FOX_W = FOX_HEADS * FOX_DH
MAIN_W = 2 * RET_QK + 2 * RET_V + 3 * FOX_W
LANES = 128
SUBLANES = 8
N_EXTRA = 7
GATE_STRIDE = 3
LOG2E = 1.4426950408889634
CUMSUM_ROWS = 256
VMEM_LIMIT = 56 * 1024 * 1024
FF_CHUNK = 256
RET_UNROLL = 4


def _rms(x):
    return lax.rsqrt(jnp.mean(x * x, axis=-1, keepdims=True) + EPS)


def _split3(c):
    hi = c.astype(BF16).astype(F32)
    r = c - hi
    mid = r.astype(BF16).astype(F32)
    return hi, mid, r - mid


def _pack3(c, gate_lane):
    hi, mid, lo = (jnp.where(gate_lane, part, 0.0) for part in _split3(c))
    return hi + pltpu.roll(mid, 1, 1) + pltpu.roll(lo, 2, 1)


def _inproj_kernel(x_ref, g_ref, wm_ref, wf_ref, fb_ref, cq_ref, sq_ref, ck_ref, sk_ref, c0_ref,
                   rq_ref, rk_ref, rv_ref, rg_ref, fq_ref, fk_ref, fv_ref, cl_ref, carry_ref,
                   *, tl, n_invalid):
    j = pl.program_id(1)

    @pl.when(j == 0)
    def _():
        carry_ref[...] = c0_ref[...]

    x = x_ref[0]
    hn = ((x * _rms(x)) * g_ref[...]).astype(BF16)

    def proj(lo, width):
        return jnp.dot(hn, wm_ref[:, lo:lo + width], preferred_element_type=F32)

    lane = lax.broadcasted_iota(jnp.int32, (1, LANES), 1)
    first_half = (lane & 32) == 0

    rqk = proj(0, 2 * RET_QK)
    for base, c_ref, s_ref, o_ref in ((0, cq_ref, sq_ref, rq_ref), (RET_QK, ck_ref, sk_ref, rk_ref)):
        for blk in range(RET_QK // LANES):
            xb = rqk[:, base + LANES * blk:base + LANES * (blk + 1)]
            swapped = jnp.where(first_half, pltpu.roll(xb, LANES - 32, 1), pltpu.roll(xb, 32, 1))
            o_ref[0, :, LANES * blk:LANES * (blk + 1)] = (xb * c_ref[...] + swapped * s_ref[...]).astype(BF16)

    rv_ref[0] = proj(2 * RET_QK, RET_V).astype(BF16)
    rg = proj(2 * RET_QK + RET_V, RET_V)
    rg_ref[0] = (rg * jax.nn.sigmoid(rg)).astype(BF16)

    gate_lane = functools.reduce(jnp.logical_or, [lane == GATE_STRIDE * h for h in range(FOX_HEADS)])
    ff = jnp.dot(hn, wf_ref[...], preferred_element_type=F32) + fb_ref[...]
    lf = jnp.minimum(ff, 0.0) - jnp.log(1.0 + jnp.exp(-jnp.abs(ff)))
    packed = _pack3(lf, gate_lane).astype(BF16)
    sub = min(tl, CUMSUM_ROWS)
    tri = lax.broadcasted_iota(jnp.int32, (sub, sub), 0) >= lax.broadcasted_iota(jnp.int32, (sub, sub), 1)
    tri = jnp.where(tri, 1.0, 0.0).astype(BF16)
    carry = carry_ref[0:1, :]
    c_blocks = []
    for r in range(tl // sub):
        c3 = jnp.dot(tri, packed[r * sub:(r + 1) * sub], preferred_element_type=F32)
        c_blk = (c3 + pltpu.roll(c3, LANES - 1, 1)) + pltpu.roll(c3, LANES - 2, 1) + carry
        carry = c_blk[sub - 1:sub, :]
        c_blocks.append(c_blk)
    c = jnp.concatenate(c_blocks, axis=0) if len(c_blocks) > 1 else c_blocks[0]
    carry_ref[...] = jnp.broadcast_to(carry, (SUBLANES, LANES))
    cl_ref[0] = carry_ref[...]

    c_triples = _pack3(c * LOG2E, gate_lane)
    neg_triples = -c_triples
    if n_invalid:
        row = j * tl + lax.broadcasted_iota(jnp.int32, (tl, LANES), 0)
        key_mask = jnp.where(row >= n_invalid, 0.0, NEG)
    fox_base = 2 * RET_QK + 2 * RET_V
    fq_all = proj(fox_base, FOX_W) * (FOX_DH ** -0.5 * LOG2E)
    fk_all = proj(fox_base + FOX_W, FOX_W)
    fv_all = proj(fox_base + 2 * FOX_W, FOX_W)
    for pair in range(FOX_HEADS // 2):
        cols = slice(LANES * pair, LANES * (pair + 1))
        fq, fk, fv = fq_all[:, cols], fk_all[:, cols], fv_all[:, cols]
        for hh in range(2):
            h = 2 * pair + hh
            is_data = (lane < FOX_DH) if hh == 0 else (lane >= FOX_DH)
            e = (lane - FOX_DH) if hh == 0 else lane
            e0 = FOX_DH if hh == 0 else 0
            c_lanes = (e >= 0) & (e < 3)
            negc_lanes = (e >= 3) & (e < 6)
            qx = jnp.where(c_lanes, pltpu.roll(c_triples, (e0 - GATE_STRIDE * h) % LANES, 1),
                           jnp.where((e >= 3) & (e < N_EXTRA), 1.0, 0.0))
            kx = jnp.where(negc_lanes, pltpu.roll(neg_triples, (e0 + 3 - GATE_STRIDE * h) % LANES, 1),
                           jnp.where(c_lanes, 1.0, 0.0))
            if n_invalid:
                kx = jnp.where(e == 6, key_mask, kx)
            vx = jnp.where(e == 0, 1.0, 0.0)
            fq_ref[0, h] = jnp.where(is_data, fq, qx).astype(BF16)
            fk_ref[0, h] = jnp.where(is_data, fk, kx).astype(BF16)
            fv_ref[0, h] = jnp.where(is_data, fv, vx).astype(BF16)


def _inproj(x, g, w_main, w_ff, fb, tabs, c0, *, tl, n_invalid):
    nb, rows, _ = x.shape
    grid = (nb, rows // tl)
    row_blk = lambda w: pl.BlockSpec((1, tl, w), lambda b, j: (b, j, 0))
    head_blk = pl.BlockSpec((1, FOX_HEADS, tl, LANES), lambda b, j: (b, 0, j, 0))
    const = lambda shape: pl.BlockSpec(shape, lambda b, j: (0,) * len(shape), pipeline_mode=pl.Buffered(1))
    tab_blk = pl.BlockSpec((tl, LANES), lambda b, j: (j, 0))
    out_shape = (
        jax.ShapeDtypeStruct((nb, rows, RET_QK), BF16), jax.ShapeDtypeStruct((nb, rows, RET_QK), BF16),
        jax.ShapeDtypeStruct((nb, rows, RET_V), BF16), jax.ShapeDtypeStruct((nb, rows, RET_V), BF16),
        jax.ShapeDtypeStruct((nb, FOX_HEADS, rows, LANES), BF16),
        jax.ShapeDtypeStruct((nb, FOX_HEADS, rows, LANES), BF16),
        jax.ShapeDtypeStruct((nb, FOX_HEADS, rows, LANES), BF16),
        jax.ShapeDtypeStruct((nb, SUBLANES, LANES), F32),
    )
    return pl.pallas_call(
        functools.partial(_inproj_kernel, tl=tl, n_invalid=n_invalid),
        grid=grid,
        in_specs=[row_blk(D_MODEL), const((1, D_MODEL)), const((D_MODEL, MAIN_W)), const((D_MODEL, LANES)),
                  const((1, LANES)), tab_blk, tab_blk, tab_blk, tab_blk, const((SUBLANES, LANES))],
        out_specs=(row_blk(RET_QK), row_blk(RET_QK), row_blk(RET_V), row_blk(RET_V),
                   head_blk, head_blk, head_blk, pl.BlockSpec((1, SUBLANES, LANES), lambda b, j: (b, 0, 0))),
        out_shape=out_shape,
        scratch_shapes=[pltpu.VMEM((SUBLANES, LANES), F32)],
        compiler_params=pltpu.CompilerParams(dimension_semantics=("arbitrary", "arbitrary"),
                                             vmem_limit_bytes=VMEM_LIMIT),
        name=f"inproj_{rows}",
    )(x, g, w_main, w_ff, fb, *tabs, c0)


def _retention_kernel(q_ref, k_ref, v_ref, gate_ref, r0_ref, d_ref, wq_ref, wk_ref, gt_ref, ng_ref,
                      o_ref, rf_ref, state_ref, *, t, n_blocks):
    state_ref[...] = r0_ref[...]
    lane = lax.broadcasted_iota(jnp.int32, (t, LANES), 1)

    def block(i, carry):
        rows = pl.ds(pl.multiple_of(i * t, t), t)
        q = q_ref[0, rows, :]
        k = k_ref[0, rows, :]
        v = v_ref[0, rows, :]
        qw = (q.astype(F32) * wq_ref[0]).astype(BF16)
        kw = (k.astype(F32) * wk_ref[0]).astype(BF16)
        for hh in range(2):
            in_head = (lane < RET_DK) if hh == 0 else (lane >= RET_DK)
            vh = v[:, RET_DV * hh:RET_DV * (hh + 1)]
            qm = jnp.where(in_head, q, jnp.zeros_like(q))
            s = lax.dot_general(qm, k, (((1,), (1,)), ((), ())), preferred_element_type=F32)
            intra = jnp.dot((s * d_ref[hh]).astype(BF16), vh, preferred_element_type=F32)
            r = state_ref[hh]
            inter = jnp.dot(qw, r.astype(BF16), preferred_element_type=F32)
            kwm = jnp.where(in_head, kw, jnp.zeros_like(kw))
            u = lax.dot_general(kwm, vh, (((0,), (0,)), ((), ())), preferred_element_type=F32)
            state_ref[hh] = gt_ref[hh] * r + u
            o = intra + inter
            cols = slice(RET_DV * hh, RET_DV * (hh + 1))
            o = (o * _rms(o)) * ng_ref[:, cols] * gate_ref[0, rows, cols].astype(F32)
            o_ref[0, rows, cols] = o.astype(BF16)
        return carry

    lax.fori_loop(0, n_blocks, block, 0, unroll=min(n_blocks, RET_UNROLL))
    rf_ref[0] = state_ref[...]


def _retention(rq, rk, rv, gate, r0, tabs, norm_g, *, t):
    nb, rows, _ = rq.shape
    d_tab, wq_tab, wk_tab, gt_tab = tabs
    n_pairs = RET_HEADS // 2
    seq = lambda w: pl.BlockSpec((1, rows, w), lambda b, p: (b, 0, p))
    return pl.pallas_call(
        functools.partial(_retention_kernel, t=t, n_blocks=rows // t),
        grid=(nb, n_pairs),
        in_specs=[seq(LANES), seq(LANES), seq(2 * RET_DV), seq(2 * RET_DV),
                  pl.BlockSpec((2, LANES, RET_DV), lambda b, p: (p, 0, 0)),
                  pl.BlockSpec((2, t, t), lambda b, p: (p, 0, 0)),
                  pl.BlockSpec((1, t, LANES), lambda b, p: (p, 0, 0)),
                  pl.BlockSpec((1, t, LANES), lambda b, p: (p, 0, 0)),
                  pl.BlockSpec((2, 1, RET_DV), lambda b, p: (p, 0, 0)),
                  pl.BlockSpec((1, 2 * RET_DV), lambda b, p: (0, p))],
        out_specs=(seq(2 * RET_DV), pl.BlockSpec((1, 2, LANES, RET_DV), lambda b, p: (b, p, 0, 0))),
        out_shape=(jax.ShapeDtypeStruct((nb, rows, RET_V), BF16),
                   jax.ShapeDtypeStruct((nb, RET_HEADS, LANES, RET_DV), F32)),
        scratch_shapes=[pltpu.VMEM((2, LANES, RET_DV), F32)],
        compiler_params=pltpu.CompilerParams(dimension_semantics=("arbitrary", "arbitrary"),
                                             vmem_limit_bytes=VMEM_LIMIT),
        name=f"retention_{rows}",
    )(rq, rk, rv, gate, r0, d_tab, wq_tab, wk_tab, gt_tab, norm_g)


def _fox_kernel(*refs, tq, tk, n_tiles, has_prefix):
    if has_prefix:
        q_ref, k_ref, v_ref, kp_ref, vp_ref, o_ref, m_ref, acc_ref = refs
    else:
        q_ref, k_ref, v_ref, o_ref, m_ref, acc_ref = refs
    n_sub = tq // tk
    lane = lax.broadcasted_iota(jnp.int32, (tq, LANES), 1)
    causal = lax.broadcasted_iota(jnp.int32, (tk, tk), 1) <= lax.broadcasted_iota(jnp.int32, (tk, tk), 0)

    def scores(hh, rows, k):
        return lax.dot_general(q_ref[0, hh, rows, :], k, (((1,), (1,)), ((), ())), preferred_element_type=F32)

    def probs(s, m):
        return jnp.exp2((s - jnp.tile(m, (1, s.shape[1] // LANES))).astype(BF16))

    def q_tile(i, carry):
        row0 = pl.multiple_of(i * tq, tq)
        rows = pl.ds(row0, tq)

        def update(hh, first_row, s, v):
            m_old = m_ref[hh, first_row:tq]
            m_new = jnp.maximum(m_old, jnp.max(s, axis=1, keepdims=True))
            m_ref[hh, first_row:tq] = m_new
            pv = jnp.dot(probs(s, m_new), v, preferred_element_type=F32)
            acc_ref[hh, first_row:tq] = jnp.exp2(m_old - m_new) * acc_ref[hh, first_row:tq] + pv

        if has_prefix:
            for hh in range(2):
                s_pre = scores(hh, rows, kp_ref[0, hh])
                m = jnp.broadcast_to(jnp.max(s_pre, axis=1, keepdims=True), (tq, LANES))
                m_ref[hh] = m
                acc_ref[hh] = jnp.dot(probs(s_pre, m), vp_ref[0, hh], preferred_element_type=F32)
        else:
            m_ref[...] = jnp.full(m_ref.shape, NEG, F32)
            acc_ref[...] = jnp.zeros(acc_ref.shape, F32)

        for c in range(n_sub):
            for hh in range(2):
                sub_rows = pl.ds(row0 + c * tk, tq - c * tk)
                keys = pl.ds(row0 + c * tk, tk)
                s = scores(hh, sub_rows, k_ref[0, hh, keys, :])
                s_top = jnp.where(causal, s[:tk], NEG)
                s = jnp.concatenate([s_top, s[tk:]], axis=0) if c < n_sub - 1 else s_top
                update(hh, c * tk, s, v_ref[0, hh, keys, :])

        def k_chunk(jj, c):
            kr = pl.ds(pl.multiple_of(jj * tk, tk), tk)
            for hh in range(2):
                update(hh, 0, scores(hh, rows, k_ref[0, hh, kr, :]), v_ref[0, hh, kr, :])
            return c

        lax.fori_loop(0, i * n_sub, k_chunk, 0)
        outs = []
        for hh in range(2):
            acc = acc_ref[hh]
            sum_lane = FOX_DH if hh == 0 else 0
            outs.append(acc / acc[:, sum_lane:sum_lane + 1])
        o_ref[0, rows, :] = jnp.where(lane < FOX_DH, outs[0], outs[1]).astype(BF16)
        return carry

    lax.fori_loop(0, n_tiles, q_tile, 0)


def _fox(fq, fk, fv, kp, vp, *, tq, tk):
    nb, _, rows, _ = fq.shape
    has_prefix = kp is not None
    pair_blk = pl.BlockSpec((1, 2, rows, LANES), lambda b, p: (b, p, 0, 0))
    in_specs = [pair_blk, pair_blk, pair_blk]
    args = [fq, fk, fv]
    if has_prefix:
        pre_blk = pl.BlockSpec((1, 2, PREFIX, LANES), lambda b, p: (0, p, 0, 0))
        in_specs += [pre_blk, pre_blk]
        args += [kp, vp]
    return pl.pallas_call(
        functools.partial(_fox_kernel, tq=tq, tk=tk, n_tiles=rows // tq, has_prefix=has_prefix),
        grid=(nb, FOX_HEADS // 2),
        in_specs=in_specs,
        out_specs=pl.BlockSpec((1, rows, LANES), lambda b, p: (b, 0, p)),
        out_shape=jax.ShapeDtypeStruct((nb, rows, FOX_W), BF16),
        scratch_shapes=[pltpu.VMEM((2, tq, LANES), F32), pltpu.VMEM((2, tq, LANES), F32)],
        compiler_params=pltpu.CompilerParams(dimension_semantics=("arbitrary", "arbitrary"),
                                             vmem_limit_bytes=VMEM_LIMIT),
        name=f"fox_{rows}",
    )(*args)


def _mixer_residual(x_ref, or_ref, of_ref, wo_ref):
    y = jnp.dot(or_ref[0], wo_ref[0:RET_V, :], preferred_element_type=F32)
    y = y + jnp.dot(of_ref[0], wo_ref[RET_V:RET_V + FOX_W, :], preferred_element_type=F32)
    return x_ref[0] + y


def _ffn_prefix_kernel(x_ref, or_ref, of_ref, wo_ref, g2_ref, wa_ref, a_ref, *, n_invalid):
    h1 = _mixer_residual(x_ref, or_ref, of_ref, wo_ref)
    hn = ((h1 * _rms(h1)) * g2_ref[...]).astype(BF16)
    a = jnp.dot(hn, wa_ref[...], preferred_element_type=F32)
    row = lax.broadcasted_iota(jnp.int32, a.shape, 0)
    a_ref[...] = jnp.where(row >= n_invalid, a, 0.0)


def _ffn_prefix(x, o_r, o_f, w_out, g2, w_up):
    full = lambda shape: pl.BlockSpec(shape, lambda i: (0,) * len(shape))
    return pl.pallas_call(
        functools.partial(_ffn_prefix_kernel, n_invalid=N_PAD),
        grid=(1,),
        in_specs=[full((1, PREFIX, D_MODEL)), full((1, PREFIX, RET_V)), full((1, PREFIX, FOX_W)),
                  full((D_MODEL, D_MODEL)), full((1, D_MODEL)), full((D_MODEL, D_FF))],
        out_specs=full((PREFIX, D_FF)),
        out_shape=jax.ShapeDtypeStruct((PREFIX, D_FF), F32),
        compiler_params=pltpu.CompilerParams(vmem_limit_bytes=VMEM_LIMIT),
        name="ffn_prefix",
    )(x, o_r, o_f, w_out, g2, w_up)


def _ffn_kernel(x_ref, or_ref, of_ref, wo_ref, g2_ref, wu_ref, cw_ref, cb_ref, wd_ref, gf_ref, halo_ref,
                o_ref, carry_ref, abuf_ref, gated_ref, *, tm):
    @pl.when(pl.program_id(1) == 0)
    def _():
        carry_ref[...] = halo_ref[...]

    h1 = _mixer_residual(x_ref, or_ref, of_ref, wo_ref)
    hn = ((h1 * _rms(h1)) * g2_ref[...]).astype(BF16)
    for c in range(D_FF // FF_CHUNK):
        cols = slice(c * FF_CHUNK, (c + 1) * FF_CHUNK)
        a = jnp.dot(hn, wu_ref[:, cols], preferred_element_type=F32)
        b = jnp.dot(hn, wu_ref[:, D_FF + c * FF_CHUNK:D_FF + (c + 1) * FF_CHUNK], preferred_element_type=F32)
        abuf_ref[0:SUBLANES, :] = carry_ref[:, cols]
        abuf_ref[SUBLANES:SUBLANES + tm, :] = a
        carry_ref[:, cols] = a[tm - SUBLANES:tm, :]
        acc = cb_ref[:, cols] + abuf_ref[SUBLANES - 2:SUBLANES - 2 + tm, :] * cw_ref[0:1, cols]
        acc = acc + abuf_ref[SUBLANES - 1:SUBLANES - 1 + tm, :] * cw_ref[1:2, cols]
        acc = acc + a * cw_ref[2:3, cols]
        gated_ref[:, cols] = (acc * jax.nn.sigmoid(acc) * b).astype(BF16)
    h2 = h1 + jnp.dot(gated_ref[...], wd_ref[...], preferred_element_type=F32)
    o_ref[0] = (h2 * _rms(h2)) * gf_ref[...]


def _ffn(x, o_r, o_f, w_out, g2, w_up, conv_w, conv_b, w_down, gf, halo, *, tm):
    nb, rows, _ = x.shape
    row_blk = lambda w: pl.BlockSpec((1, tm, w), lambda b, j: (b, j, 0))
    const = lambda shape: pl.BlockSpec(shape, lambda b, j: (0,) * len(shape), pipeline_mode=pl.Buffered(1))
    return pl.pallas_call(
        functools.partial(_ffn_kernel, tm=tm),
        grid=(nb, rows // tm),
        in_specs=[row_blk(D_MODEL), row_blk(RET_V), row_blk(FOX_W), const((D_MODEL, D_MODEL)),
                  const((1, D_MODEL)), const((D_MODEL, 2 * D_FF)), const((CONV_W, D_FF)), const((1, D_FF)),
                  const((D_FF, D_MODEL)), const((1, D_MODEL)), const((SUBLANES, D_FF))],
        out_specs=row_blk(D_MODEL),
        out_shape=jax.ShapeDtypeStruct((nb, rows, D_MODEL), F32),
        scratch_shapes=[pltpu.VMEM((SUBLANES, D_FF), F32), pltpu.VMEM((SUBLANES + tm, FF_CHUNK), F32),
                        pltpu.VMEM((tm, D_FF), BF16)],
        compiler_params=pltpu.CompilerParams(dimension_semantics=("arbitrary", "arbitrary"),
                                             vmem_limit_bytes=VMEM_LIMIT),
        name="ffn",
    )(x, o_r, o_f, w_out, g2, w_up, conv_w, conv_b, w_down, gf, halo)


def _rotary_tables(start, length):
    half = RET_DK // 2
    inv = 1.0 / (ROPE_BASE ** (jnp.arange(half, dtype=F32) / half))
    pos = start + jnp.arange(length)
    ang = pos.astype(F32)[:, None] * inv[None, :]
    cos = jnp.tile(jnp.cos(ang), (1, LANES // half))
    sin = jnp.tile(jnp.concatenate([-jnp.sin(ang), jnp.sin(ang)], axis=1), (1, LANES // RET_DK))
    k_scale = (RET_DK ** -0.5) * (pos >= N_PAD).astype(F32)[:, None]
    return cos, sin, cos * k_scale, sin * k_scale


def _retention_tables(t):
    log_g = jnp.log1p(-jnp.exp2(-5.0 - jnp.arange(RET_HEADS, dtype=F32)))
    n = jnp.arange(t, dtype=F32)
    visible = (jnp.arange(t)[None, :] // CHUNK) <= (jnp.arange(t)[:, None] // CHUNK)
    d = jnp.exp(jnp.abs(n[:, None] - n[None, :])[None] * log_g[:, None, None]) * visible[None]
    lane_log_g = jnp.repeat(log_g, RET_DK).reshape(RET_HEADS // 2, 1, LANES)
    wq = jnp.exp((n + 1.0)[None, :, None] * lane_log_g)
    wk = jnp.exp((t - 1.0 - n)[None, :, None] * lane_log_g)
    gt = jnp.broadcast_to(jnp.exp(t * log_g)[:, None, None], (RET_HEADS, 1, RET_DV))
    return d, wq, wk, gt


def _gate_lanes(v):
    spread = jnp.stack([v] + [jnp.zeros_like(v)] * (GATE_STRIDE - 1), axis=-1)
    spread = spread.reshape(v.shape[:-1] + (GATE_STRIDE * FOX_HEADS,))
    return jnp.pad(spread, [(0, 0)] * (v.ndim - 1) + [(0, LANES - GATE_STRIDE * FOX_HEADS)])


def kernel(x, meta_tokens, attn_norm_g, w_in, fox_forget_b, ret_norm_g, w_out, ffn_norm_g, w_up, conv_w,
           conv_b, w_down, final_norm_g):
    assert w_in.shape[0] == 1, "single-layer block"
    batch, seq, _ = x.shape
    w_main = w_in[0, :, :MAIN_W].astype(BF16)
    w_ff = _gate_lanes(w_in[0, :, MAIN_W:]).astype(BF16)
    fb = _gate_lanes(fox_forget_b[0]).reshape(1, LANES)
    g1 = attn_norm_g[0].reshape(1, D_MODEL)
    g2 = ffn_norm_g[0].reshape(1, D_MODEL)
    gf = final_norm_g.reshape(1, D_MODEL)
    ng = ret_norm_g[0].reshape(1, RET_V)
    w_out_b = w_out[0].astype(BF16)
    w_up_b = w_up[0].astype(BF16)
    w_down_b = w_down[0].astype(BF16)
    cb = conv_b[0].reshape(1, D_FF)
    h_pre = jnp.concatenate([jnp.zeros((N_PAD, D_MODEL), x.dtype), meta_tokens.astype(x.dtype)], axis=0)[None]

    zeros_c = jnp.zeros((SUBLANES, LANES), F32)
    rq0, rk0, rv0, rg0, fq0, fk0, fv0, c_pre = _inproj(h_pre, g1, w_main, w_ff, fb, _rotary_tables(0, PREFIX),
                                                        zeros_c, tl=PREFIX, n_invalid=N_PAD)
    r_zero = jnp.zeros((RET_HEADS, LANES, RET_DV), F32)
    or0, r_pre = _retention(rq0, rk0, rv0, rg0, r_zero, _retention_tables(PREFIX), ng, t=PREFIX)
    of0 = _fox(fq0, fk0, fv0, None, None, tq=PREFIX, tk=PREFIX)
    a_pre = _ffn_prefix(h_pre, or0, of0, w_out_b, g2, w_up_b)
    halo = a_pre[PREFIX - SUBLANES:]

    rq, rk, rv, rg, fq, fk, fv, _ = _inproj(x, g1, w_main, w_ff, fb, _rotary_tables(PREFIX, seq), c_pre[0],
                                            tl=1024, n_invalid=0)
    o_r, _ = _retention(rq, rk, rv, rg, r_pre[0], _retention_tables(256), ng, t=256)
    o_f = _fox(fq, fk, fv, fk0, fv0, tq=4096, tk=512)
    return _ffn(x, o_r, o_f, w_out_b, g2, w_up_b, conv_w[0], cb, w_down_b, gf, halo, tm=1024)
```

```python
import functools

import jax
import jax.numpy as jnp
import numpy as np
from jax import lax
from jax.experimental import pallas as pl
from jax.experimental.pallas import tpu as pltpu

F32 = jnp.float32
BF16 = jnp.bfloat16

D_MODEL = 1024
CHUNK = 64
N_META = 16
PREFIX = 128
N_PAD = PREFIX - N_META
RET_HEADS = 4
RET_DK = 64
RET_DV = 128
FOX_HEADS = 8
FOX_DH = 64
D_FF = 2816
CONV_W = 3
ROPE_BASE = 10000.0
EPS = 1e-6
NEG = -1e30

RET_QK = RET_HEADS * RET_DK
RET_V = RET_HEADS * RET_DV
FOX_W = FOX_HEADS * FOX_DH
MAIN_W = 2 * RET_QK + 2 * RET_V + 3 * FOX_W
LANES = 128
SUBLANES = 8
MXU_TILE = 256
N_EXTRA = 7
GATE_STRIDE = 3
LOG2E = 1.4426950408889634
CUMSUM_ROWS = 256
VMEM_LIMIT = 56 * 1024 * 1024
FF_CHUNK = 256
RET_UNROLL = 4


def _rms(x):
    return lax.rsqrt(jnp.mean(x * x, axis=-1, keepdims=True) + EPS)


def _split3(c):
    hi = c.astype(BF16).astype(F32)
    r = c - hi
    mid = r.astype(BF16).astype(F32)
    return hi, mid, r - mid


def _pack3(c, gate_lane):
    hi, mid, lo = (jnp.where(gate_lane, part, 0.0) for part in _split3(c))
    return hi + pltpu.roll(mid, 1, 1) + pltpu.roll(lo, 2, 1)


def _inproj_kernel(x_ref, g_ref, wm_ref, wf_ref, fb_ref, cq_ref, sq_ref, ck_ref, sk_ref, c0_ref,
                   rq_ref, rk_ref, rv_ref, rg_ref, fq_ref, fk_ref, fv_ref, cl_ref, carry_ref,
                   *, tl, n_invalid):
    j = pl.program_id(1)

    @pl.when(j == 0)
    def _():
        carry_ref[...] = c0_ref[...]

    x = x_ref[0]
    hn = ((x * _rms(x)) * g_ref[...]).astype(BF16)

    def proj(lo, width):
        return jnp.dot(hn, wm_ref[:, lo:lo + width], preferred_element_type=F32)

    lane = lax.broadcasted_iota(jnp.int32, (1, LANES), 1)
    first_half = (lane & 32) == 0

    rqk = proj(0, 2 * RET_QK)
    for base, c_ref, s_ref, o_ref in ((0, cq_ref, sq_ref, rq_ref), (RET_QK, ck_ref, sk_ref, rk_ref)):
        for blk in range(RET_QK // LANES):
            xb = rqk[:, base + LANES * blk:base + LANES * (blk + 1)]
            swapped = jnp.where(first_half, pltpu.roll(xb, LANES - 32, 1), pltpu.roll(xb, 32, 1))
            o_ref[0, :, LANES * blk:LANES * (blk + 1)] = (xb * c_ref[...] + swapped * s_ref[...]).astype(BF16)

    rv_ref[0] = proj(2 * RET_QK, RET_V).astype(BF16)
    rg = proj(2 * RET_QK + RET_V, RET_V)
    rg_ref[0] = (rg * jax.nn.sigmoid(rg)).astype(BF16)

    gate_lane = functools.reduce(jnp.logical_or, [lane == GATE_STRIDE * h for h in range(FOX_HEADS)])
    ff = jnp.dot(hn, wf_ref[...], preferred_element_type=F32) + fb_ref[...]
    lf = jnp.minimum(ff, 0.0) - jnp.log(1.0 + jnp.exp(-jnp.abs(ff)))
    packed = _pack3(lf, gate_lane).astype(BF16)
    sub = min(tl, CUMSUM_ROWS)
    tri = lax.broadcasted_iota(jnp.int32, (sub, sub), 0) >= lax.broadcasted_iota(jnp.int32, (sub, sub), 1)
    tri = jnp.where(tri, 1.0, 0.0).astype(BF16)
    carry = carry_ref[0:1, :]
    c_blocks = []
    for r in range(tl // sub):
        c3 = jnp.dot(tri, packed[r * sub:(r + 1) * sub], preferred_element_type=F32)
        c_blk = (c3 + pltpu.roll(c3, LANES - 1, 1)) + pltpu.roll(c3, LANES - 2, 1) + carry
        carry = c_blk[sub - 1:sub, :]
        c_blocks.append(c_blk)
    c = jnp.concatenate(c_blocks, axis=0) if len(c_blocks) > 1 else c_blocks[0]
    carry_ref[...] = jnp.broadcast_to(carry, (SUBLANES, LANES))
    cl_ref[0] = carry_ref[...]

    c_triples = _pack3(c * LOG2E, gate_lane)
    neg_triples = -c_triples
    if n_invalid:
        row = j * tl + lax.broadcasted_iota(jnp.int32, (tl, LANES), 0)
        key_mask = jnp.where(row >= n_invalid, 0.0, NEG)
    fox_base = 2 * RET_QK + 2 * RET_V
    fq_all = proj(fox_base, FOX_W) * (FOX_DH ** -0.5 * LOG2E)
    fk_all = proj(fox_base + FOX_W, FOX_W)
    fv_all = proj(fox_base + 2 * FOX_W, FOX_W)
    for pair in range(FOX_HEADS // 2):
        cols = slice(LANES * pair, LANES * (pair + 1))
        fq, fk, fv = fq_all[:, cols], fk_all[:, cols], fv_all[:, cols]
        for hh in range(2):
            h = 2 * pair + hh
            is_data = (lane < FOX_DH) if hh == 0 else (lane >= FOX_DH)
            e = (lane - FOX_DH) if hh == 0 else lane
            e0 = FOX_DH if hh == 0 else 0
            c_lanes = (e >= 0) & (e < 3)
            negc_lanes = (e >= 3) & (e < 6)
            qx = jnp.where(c_lanes, pltpu.roll(c_triples, (e0 - GATE_STRIDE * h) % LANES, 1),
                           jnp.where((e >= 3) & (e < N_EXTRA), 1.0, 0.0))
            kx = jnp.where(negc_lanes, pltpu.roll(neg_triples, (e0 + 3 - GATE_STRIDE * h) % LANES, 1),
                           jnp.where(c_lanes, 1.0, 0.0))
            if n_invalid:
                kx = jnp.where(e == 6, key_mask, kx)
            vx = jnp.where(e == 0, 1.0, 0.0)
            fq_ref[0, h] = jnp.where(is_data, fq, qx).astype(BF16)
            fk_ref[0, h] = jnp.where(is_data, fk, kx).astype(BF16)
            fv_ref[0, h] = jnp.where(is_data, fv, vx).astype(BF16)


def _inproj(x, g, w_main, w_ff, fb, tabs, c0, *, tl, n_invalid):
    nb, rows, _ = x.shape
    grid = (nb, rows // tl)
    row_blk = lambda w: pl.BlockSpec((1, tl, w), lambda b, j: (b, j, 0))
    head_blk = pl.BlockSpec((1, FOX_HEADS, tl, LANES), lambda b, j: (b, 0, j, 0))
    const = lambda shape: pl.BlockSpec(shape, lambda b, j: (0,) * len(shape), pipeline_mode=pl.Buffered(1))
    tab_blk = pl.BlockSpec((tl, LANES), lambda b, j: (j, 0))
    out_shape = (
        jax.ShapeDtypeStruct((nb, rows, RET_QK), BF16), jax.ShapeDtypeStruct((nb, rows, RET_QK), BF16),
        jax.ShapeDtypeStruct((nb, rows, RET_V), BF16), jax.ShapeDtypeStruct((nb, rows, RET_V), BF16),
        jax.ShapeDtypeStruct((nb, FOX_HEADS, rows, LANES), BF16),
        jax.ShapeDtypeStruct((nb, FOX_HEADS, rows, LANES), BF16),
        jax.ShapeDtypeStruct((nb, FOX_HEADS, rows, LANES), BF16),
        jax.ShapeDtypeStruct((nb, SUBLANES, LANES), F32),
    )
    return pl.pallas_call(
        functools.partial(_inproj_kernel, tl=tl, n_invalid=n_invalid),
        grid=grid,
        in_specs=[row_blk(D_MODEL), const((1, D_MODEL)), const((D_MODEL, MAIN_W)), const((D_MODEL, LANES)),
                  const((1, LANES)), tab_blk, tab_blk, tab_blk, tab_blk, const((SUBLANES, LANES))],
        out_specs=(row_blk(RET_QK), row_blk(RET_QK), row_blk(RET_V), row_blk(RET_V),
                   head_blk, head_blk, head_blk, pl.BlockSpec((1, SUBLANES, LANES), lambda b, j: (b, 0, 0))),
        out_shape=out_shape,
        scratch_shapes=[pltpu.VMEM((SUBLANES, LANES), F32)],
        compiler_params=pltpu.CompilerParams(dimension_semantics=("arbitrary", "arbitrary"),
                                             vmem_limit_bytes=VMEM_LIMIT),
        name=f"inproj_{rows}",
    )(x, g, w_main, w_ff, fb, *tabs, c0)


def _retention_kernel(q_ref, k_ref, v_ref, gate_ref, r0_ref, d_ref, wq_ref, wk_ref, gt_ref, ng_ref,
                      o_ref, rf_ref, state_ref, *, t, n_blocks):
    state_ref[...] = r0_ref[...]
    lane = lax.broadcasted_iota(jnp.int32, (t, LANES), 1)

    def block(i, carry):
        rows = pl.ds(pl.multiple_of(i * t, t), t)
        q = q_ref[0, rows, :]
        k = k_ref[0, rows, :]
        v = v_ref[0, rows, :]
        qw = (q.astype(F32) * wq_ref[0]).astype(BF16)
        kw = (k.astype(F32) * wk_ref[0]).astype(BF16)
        for hh in range(2):
            in_head = (lane < RET_DK) if hh == 0 else (lane >= RET_DK)
            vh = v[:, RET_DV * hh:RET_DV * (hh + 1)]
            qm = jnp.where(in_head, q, jnp.zeros_like(q))
            s = lax.dot_general(qm, k, (((1,), (1,)), ((), ())), preferred_element_type=F32)
            intra = jnp.dot((s * d_ref[hh]).astype(BF16), vh, preferred_element_type=F32)
            r = state_ref[hh]
            inter = jnp.dot(qw, r.astype(BF16), preferred_element_type=F32)
            kwm = jnp.where(in_head, kw, jnp.zeros_like(kw))
            u = lax.dot_general(kwm, vh, (((0,), (0,)), ((), ())), preferred_element_type=F32)
            state_ref[hh] = gt_ref[hh] * r + u
            o = intra + inter
            cols = slice(RET_DV * hh, RET_DV * (hh + 1))
            o = (o * _rms(o)) * ng_ref[:, cols] * gate_ref[0, rows, cols].astype(F32)
            o_ref[0, rows, cols] = o.astype(BF16)
        return carry

    lax.fori_loop(0, n_blocks, block, 0, unroll=min(n_blocks, RET_UNROLL))
    rf_ref[0] = state_ref[...]


def _retention(rq, rk, rv, gate, r0, tabs, norm_g, *, t):
    nb, rows, _ = rq.shape
    d_tab, wq_tab, wk_tab, gt_tab = tabs
    n_pairs = RET_HEADS // 2
    seq = lambda w: pl.BlockSpec((1, rows, w), lambda b, p: (b, 0, p))
    return pl.pallas_call(
        functools.partial(_retention_kernel, t=t, n_blocks=rows // t),
        grid=(nb, n_pairs),
        in_specs=[seq(LANES), seq(LANES), seq(2 * RET_DV), seq(2 * RET_DV),
                  pl.BlockSpec((2, LANES, RET_DV), lambda b, p: (p, 0, 0)),
                  pl.BlockSpec((2, t, t), lambda b, p: (p, 0, 0)),
                  pl.BlockSpec((1, t, LANES), lambda b, p: (p, 0, 0)),
                  pl.BlockSpec((1, t, LANES), lambda b, p: (p, 0, 0)),
                  pl.BlockSpec((2, 1, RET_DV), lambda b, p: (p, 0, 0)),
                  pl.BlockSpec((1, 2 * RET_DV), lambda b, p: (0, p))],
        out_specs=(seq(2 * RET_DV), pl.BlockSpec((1, 2, LANES, RET_DV), lambda b, p: (b, p, 0, 0))),
        out_shape=(jax.ShapeDtypeStruct((nb, rows, RET_V), BF16),
                   jax.ShapeDtypeStruct((nb, RET_HEADS, LANES, RET_DV), F32)),
        scratch_shapes=[pltpu.VMEM((2, LANES, RET_DV), F32)],
        compiler_params=pltpu.CompilerParams(dimension_semantics=("arbitrary", "arbitrary"),
                                             vmem_limit_bytes=VMEM_LIMIT),
        name=f"retention_{rows}",
    )(rq, rk, rv, gate, r0, d_tab, wq_tab, wk_tab, gt_tab, norm_g)


def _fox_kernel(*refs, rows, tk, has_prefix):
    if has_prefix:
        q_ref, k_ref, v_ref, kp_ref, vp_ref, o_ref, m_ref, acc_ref = refs
    else:
        q_ref, k_ref, v_ref, o_ref, m_ref, acc_ref = refs
    hk = tk // 2 if tk >= 2 * MXU_TILE else tk
    lane = lax.broadcasted_iota(jnp.int32, (rows, LANES), 1)
    causal = lax.broadcasted_iota(jnp.int32, (hk, hk), 1) <= lax.broadcasted_iota(jnp.int32, (hk, hk), 0)
    nt_dims = (((1,), (1,)), ((), ()))

    def scores(hh, lo, k):
        return lax.dot_general(q_ref[0, hh, lo:rows, :], k, nt_dims, preferred_element_type=F32)

    def probs(s, m):
        return jnp.exp2((s - jnp.tile(m, (1, s.shape[1] // LANES))).astype(BF16))

    def masked_top(s):
        top = jnp.where(causal, s[:hk], NEG)
        return jnp.concatenate([top, s[hk:]], axis=0) if s.shape[0] > hk else top

    def update(hh, lo, hi, s, v):
        m_old = m_ref[hh, lo:hi]
        m_new = jnp.maximum(m_old, jnp.max(s, axis=1, keepdims=True))
        m_ref[hh, lo:hi] = m_new
        pv = jnp.dot(probs(s, m_new), v, preferred_element_type=F32)
        acc_ref[hh, lo:hi] = jnp.exp2(m_old - m_new) * acc_ref[hh, lo:hi] + pv

    if has_prefix:
        q_pair = jnp.concatenate([q_ref[0, 0], q_ref[0, 1]], axis=1)
        s_pre = lax.dot_general(q_pair, kp_ref[0], nt_dims, preferred_element_type=F32)
        p_pre = []
        for hh in range(2):
            s_h = s_pre[:, LANES * hh:LANES * (hh + 1)]
            m = jnp.broadcast_to(jnp.max(s_h, axis=1, keepdims=True), (rows, LANES))
            m_ref[hh] = m
            p_pre.append(probs(s_h, m))
        acc_pair = jnp.dot(jnp.concatenate(p_pre, axis=1), vp_ref[0], preferred_element_type=F32)
        acc_ref[0] = acc_pair[:, :LANES]
        acc_ref[1] = acc_pair[:, LANES:]
    else:
        m_ref[...] = jnp.full(m_ref.shape, NEG, F32)
        acc_ref[...] = jnp.zeros(acc_ref.shape, F32)

    for c in range(rows // tk):
        lo = c * tk
        for hh in range(2):
            if hk == tk:
                s = masked_top(scores(hh, lo, k_ref[0, hh, lo:lo + tk, :]))
                update(hh, lo, rows, s, v_ref[0, hh, lo:lo + tk, :])
            else:
                mid = lo + hk
                s0 = scores(hh, lo, k_ref[0, hh, lo:mid, :])
                s1 = masked_top(scores(hh, mid, k_ref[0, hh, mid:lo + tk, :]))
                update(hh, lo, mid, jnp.where(causal, s0[:hk], NEG), v_ref[0, hh, lo:mid, :])
                update(hh, mid, rows, jnp.concatenate([s0[hk:], s1], axis=1), v_ref[0, hh, lo:lo + tk, :])

    outs = []
    for hh in range(2):
        acc = acc_ref[hh]
        sum_lane = FOX_DH if hh == 0 else 0
        outs.append(acc / acc[:, sum_lane:sum_lane + 1])
    o_ref[0] = jnp.where(lane < FOX_DH, outs[0], outs[1]).astype(BF16)


def _block_diag_pairs(a):
    a = a.reshape(FOX_HEADS // 2, 2, a.shape[-2], a.shape[-1])
    z = jnp.zeros_like(a[:, 0])
    return jnp.concatenate([jnp.concatenate([a[:, 0], z], axis=-1), jnp.concatenate([z, a[:, 1]], axis=-1)], axis=-2)


def _fox(fq, fk, fv, kp, vp, *, tk):
    nb, _, rows, _ = fq.shape
    has_prefix = kp is not None
    pair_blk = pl.BlockSpec((1, 2, rows, LANES), lambda b, p: (b, p, 0, 0))
    in_specs = [pair_blk, pair_blk, pair_blk]
    args = [fq, fk, fv]
    if has_prefix:
        pre_blk = pl.BlockSpec((1, 2 * PREFIX, 2 * LANES), lambda b, p: (p, 0, 0))
        in_specs += [pre_blk, pre_blk]
        args += [_block_diag_pairs(kp[0]), _block_diag_pairs(vp[0])]
    return pl.pallas_call(
        functools.partial(_fox_kernel, rows=rows, tk=tk, has_prefix=has_prefix),
        grid=(nb, FOX_HEADS // 2),
        in_specs=in_specs,
        out_specs=pl.BlockSpec((1, rows, LANES), lambda b, p: (b, 0, p)),
        out_shape=jax.ShapeDtypeStruct((nb, rows, FOX_W), BF16),
        scratch_shapes=[pltpu.VMEM((2, rows, LANES), F32), pltpu.VMEM((2, rows, LANES), F32)],
        compiler_params=pltpu.CompilerParams(dimension_semantics=("arbitrary", "arbitrary"),
                                             vmem_limit_bytes=VMEM_LIMIT),
        name=f"fox_{rows}",
    )(*args)


def _mixer_residual(x_ref, or_ref, of_ref, wo_ref):
    y = jnp.dot(or_ref[0], wo_ref[0:RET_V, :], preferred_element_type=F32)
    y = y + jnp.dot(of_ref[0], wo_ref[RET_V:RET_V + FOX_W, :], preferred_element_type=F32)
    return x_ref[0] + y


def _ffn_prefix_kernel(x_ref, or_ref, of_ref, wo_ref, g2_ref, wa_ref, a_ref, *, n_invalid):
    h1 = _mixer_residual(x_ref, or_ref, of_ref, wo_ref)
    hn = ((h1 * _rms(h1)) * g2_ref[...]).astype(BF16)
    a = jnp.dot(hn, wa_ref[...], preferred_element_type=F32)
    row = lax.broadcasted_iota(jnp.int32, a.shape, 0)
    a_ref[...] = jnp.where(row >= n_invalid, a, 0.0)


def _ffn_prefix(x, o_r, o_f, w_out, g2, w_up):
    full = lambda shape: pl.BlockSpec(shape, lambda i: (0,) * len(shape))
    return pl.pallas_call(
        functools.partial(_ffn_prefix_kernel, n_invalid=N_PAD),
        grid=(1,),
        in_specs=[full((1, PREFIX, D_MODEL)), full((1, PREFIX, RET_V)), full((1, PREFIX, FOX_W)),
                  full((D_MODEL, D_MODEL)), full((1, D_MODEL)), full((D_MODEL, D_FF))],
        out_specs=full((PREFIX, D_FF)),
        out_shape=jax.ShapeDtypeStruct((PREFIX, D_FF), F32),
        compiler_params=pltpu.CompilerParams(vmem_limit_bytes=VMEM_LIMIT),
        name="ffn_prefix",
    )(x, o_r, o_f, w_out, g2, w_up)


def _ffn_kernel(x_ref, or_ref, of_ref, wo_ref, g2_ref, wu_ref, cw_ref, cb_ref, wd_ref, gf_ref, halo_ref,
                o_ref, carry_ref, abuf_ref, gated_ref, *, tm):
    @pl.when(pl.program_id(1) == 0)
    def _():
        carry_ref[...] = halo_ref[...]

    h1 = _mixer_residual(x_ref, or_ref, of_ref, wo_ref)
    hn = ((h1 * _rms(h1)) * g2_ref[...]).astype(BF16)
    for c in range(D_FF // FF_CHUNK):
        cols = slice(c * FF_CHUNK, (c + 1) * FF_CHUNK)
        a = jnp.dot(hn, wu_ref[:, cols], preferred_element_type=F32)
        b = jnp.dot(hn, wu_ref[:, D_FF + c * FF_CHUNK:D_FF + (c + 1) * FF_CHUNK], preferred_element_type=F32)
        abuf_ref[0:SUBLANES, :] = carry_ref[:, cols]
        abuf_ref[SUBLANES:SUBLANES + tm, :] = a
        carry_ref[:, cols] = a[tm - SUBLANES:tm, :]
        acc = cb_ref[:, cols] + abuf_ref[SUBLANES - 2:SUBLANES - 2 + tm, :] * cw_ref[0:1, cols]
        acc = acc + abuf_ref[SUBLANES - 1:SUBLANES - 1 + tm, :] * cw_ref[1:2, cols]
        acc = acc + a * cw_ref[2:3, cols]
        gated_ref[:, cols] = (acc * jax.nn.sigmoid(acc) * b).astype(BF16)
    h2 = h1 + jnp.dot(gated_ref[...], wd_ref[...], preferred_element_type=F32)
    o_ref[0] = (h2 * _rms(h2)) * gf_ref[...]


def _ffn(x, o_r, o_f, w_out, g2, w_up, conv_w, conv_b, w_down, gf, halo, *, tm):
    nb, rows, _ = x.shape
    row_blk = lambda w: pl.BlockSpec((1, tm, w), lambda b, j: (b, j, 0))
    const = lambda shape: pl.BlockSpec(shape, lambda b, j: (0,) * len(shape), pipeline_mode=pl.Buffered(1))
    return pl.pallas_call(
        functools.partial(_ffn_kernel, tm=tm),
        grid=(nb, rows // tm),
        in_specs=[row_blk(D_MODEL), row_blk(RET_V), row_blk(FOX_W), const((D_MODEL, D_MODEL)),
                  const((1, D_MODEL)), const((D_MODEL, 2 * D_FF)), const((CONV_W, D_FF)), const((1, D_FF)),
                  const((D_FF, D_MODEL)), const((1, D_MODEL)), const((SUBLANES, D_FF))],
        out_specs=row_blk(D_MODEL),
        out_shape=jax.ShapeDtypeStruct((nb, rows, D_MODEL), F32),
        scratch_shapes=[pltpu.VMEM((SUBLANES, D_FF), F32), pltpu.VMEM((SUBLANES + tm, FF_CHUNK), F32),
                        pltpu.VMEM((tm, D_FF), BF16)],
        compiler_params=pltpu.CompilerParams(dimension_semantics=("arbitrary", "arbitrary"),
                                             vmem_limit_bytes=VMEM_LIMIT),
        name="ffn",
    )(x, o_r, o_f, w_out, g2, w_up, conv_w, conv_b, w_down, gf, halo)


def _rotary_tables(start, length):
    half = RET_DK // 2
    inv = (1.0 / (ROPE_BASE ** (np.arange(half, dtype=np.float32) / np.float32(half)))).astype(np.float32)
    pos = start + np.arange(length)
    ang = pos.astype(np.float32)[:, None] * inv[None, :]
    cos = np.tile(np.cos(ang), (1, LANES // half))
    sin = np.tile(np.concatenate([-np.sin(ang), np.sin(ang)], axis=1), (1, LANES // RET_DK))
    k_scale = (np.float32(RET_DK ** -0.5) * (pos >= N_PAD).astype(np.float32))[:, None]
    return cos, sin, cos * k_scale, sin * k_scale


def _retention_tables(t):
    f32 = np.float32
    log_g = np.log1p(-np.exp2(f32(-5.0) - np.arange(RET_HEADS, dtype=f32)))
    n = np.arange(t, dtype=f32)
    visible = (np.arange(t)[None, :] // CHUNK) <= (np.arange(t)[:, None] // CHUNK)
    d = np.exp(np.abs(n[:, None] - n[None, :])[None] * log_g[:, None, None]) * visible[None].astype(f32)
    lane_log_g = np.repeat(log_g, RET_DK).reshape(RET_HEADS // 2, 1, LANES)
    wq = np.exp((n + f32(1.0))[None, :, None] * lane_log_g)
    wk = np.exp((f32(t) - f32(1.0) - n)[None, :, None] * lane_log_g)
    gt = np.broadcast_to(np.exp(f32(t) * log_g)[:, None, None], (RET_HEADS, 1, RET_DV))
    return tuple(np.ascontiguousarray(a, dtype=f32) for a in (d, wq, wk, gt))


def _gate_lanes(v):
    spread = jnp.stack([v] + [jnp.zeros_like(v)] * (GATE_STRIDE - 1), axis=-1)
    spread = spread.reshape(v.shape[:-1] + (GATE_STRIDE * FOX_HEADS,))
    return jnp.pad(spread, [(0, 0)] * (v.ndim - 1) + [(0, LANES - GATE_STRIDE * FOX_HEADS)])


def kernel(x, meta_tokens, attn_norm_g, w_in, fox_forget_b, ret_norm_g, w_out, ffn_norm_g, w_up, conv_w,
           conv_b, w_down, final_norm_g):
    assert w_in.shape[0] == 1, "single-layer block"
    batch, seq, _ = x.shape
    w_main = w_in[0].astype(BF16)
    w_ff = _gate_lanes(w_in[0, :, MAIN_W:]).astype(BF16)
    fb = _gate_lanes(fox_forget_b[0]).reshape(1, LANES)
    g1 = attn_norm_g[0].reshape(1, D_MODEL)
    g2 = ffn_norm_g[0].reshape(1, D_MODEL)
    gf = final_norm_g.reshape(1, D_MODEL)
    ng = ret_norm_g[0].reshape(1, RET_V)
    w_out_b = w_out[0].astype(BF16)
    w_up_b = w_up[0].astype(BF16)
    w_down_b = w_down[0].astype(BF16)
    cb = conv_b[0].reshape(1, D_FF)
    h_pre = jnp.concatenate([jnp.zeros((N_PAD, D_MODEL), x.dtype), meta_tokens.astype(x.dtype)], axis=0)[None]

    zeros_c = jnp.zeros((SUBLANES, LANES), F32)
    rq0, rk0, rv0, rg0, fq0, fk0, fv0, c_pre = _inproj(h_pre, g1, w_main, w_ff, fb, _rotary_tables(0, PREFIX),
                                                        zeros_c, tl=PREFIX, n_invalid=N_PAD)
    r_zero = jnp.zeros((RET_HEADS, LANES, RET_DV), F32)
    or0, r_pre = _retention(rq0, rk0, rv0, rg0, r_zero, _retention_tables(PREFIX), ng, t=PREFIX)
    of0 = _fox(fq0, fk0, fv0, None, None, tk=PREFIX)
    a_pre = _ffn_prefix(h_pre, or0, of0, w_out_b, g2, w_up_b)
    halo = a_pre[PREFIX - SUBLANES:]

    rq, rk, rv, rg, fq, fk, fv, _ = _inproj(x, g1, w_main, w_ff, fb, _rotary_tables(PREFIX, seq), c_pre[0],
                                            tl=1024, n_invalid=0)
    o_r, _ = _retention(rq, rk, rv, rg, r_pre[0], _retention_tables(256), ng, t=256)
    o_f = _fox(fq, fk, fv, fk0, fv0, tk=512)
    return _ffn(x, o_r, o_f, w_out_b, g2, w_up_b, conv_w[0], cb, w_down_b, gf, halo, tm=1024)
```

```python
import functools

import jax
import jax.numpy as jnp
import numpy as np
from jax import lax
from jax.experimental import pallas as pl
from jax.experimental.pallas import tpu as pltpu

F32 = jnp.float32
BF16 = jnp.bfloat16

D_MODEL = 1024
CHUNK = 64
N_META = 16
PREFIX = 128
N_PAD = PREFIX - N_META
RET_HEADS = 4
RET_DK = 64
RET_DV = 128
FOX_HEADS = 8
FOX_DH = 64
D_FF = 2816
CONV_W = 3
ROPE_BASE = 10000.0
EPS = 1e-6
NEG = -1e30

RET_QK = RET_HEADS * RET_DK
RET_V = RET_HEADS * RET_DV
FOX_W = FOX_HEADS * FOX_DH
MAIN_W = 2 * RET_QK + 2 * RET_V + 3 * FOX_W
LANES = 128
SUBLANES = 8
N_EXTRA = 7
GATE_STRIDE = 3
LOG2E = 1.4426950408889634
CUMSUM_ROWS = 256
VMEM_LIMIT = 56 * 1024 * 1024
FF_CHUNK = 256
RET_UNROLL = 8


def _rms(x):
    return lax.rsqrt(jnp.mean(x * x, axis=-1, keepdims=True) + EPS)


def _split3(c):
    hi = c.astype(BF16).astype(F32)
    r = c - hi
    mid = r.astype(BF16).astype(F32)
    return hi, mid, r - mid


def _pack3(c, gate_lane):
    hi, mid, lo = (jnp.where(gate_lane, part, 0.0) for part in _split3(c))
    return hi + pltpu.roll(mid, 1, 1) + pltpu.roll(lo, 2, 1)


def _inproj_kernel(x_ref, g_ref, wm_ref, wf_ref, fb_ref, cq_ref, sq_ref, ck_ref, sk_ref, c0_ref,
                   rq_ref, rk_ref, rv_ref, rg_ref, fq_ref, fk_ref, fv_ref, cl_ref, carry_ref,
                   *, tl, n_invalid):
    j = pl.program_id(1)

    @pl.when(j == 0)
    def _():
        carry_ref[...] = c0_ref[...]

    x = x_ref[0]
    hn = ((x * _rms(x)) * g_ref[...]).astype(BF16)

    def proj(lo, width):
        return jnp.dot(hn, wm_ref[:, lo:lo + width], preferred_element_type=F32)

    lane = lax.broadcasted_iota(jnp.int32, (1, LANES), 1)
    first_half = (lane & 32) == 0

    gate_lane = functools.reduce(jnp.logical_or, [lane == GATE_STRIDE * h for h in range(FOX_HEADS)])
    ff = jnp.dot(hn, wf_ref[...], preferred_element_type=F32) + fb_ref[...]
    lf = jnp.minimum(ff, 0.0) - jnp.log(1.0 + jnp.exp(-jnp.abs(ff)))
    packed = _pack3(lf, gate_lane).astype(BF16)
    sub = min(tl, CUMSUM_ROWS)
    tri = lax.broadcasted_iota(jnp.int32, (sub, sub), 0) >= lax.broadcasted_iota(jnp.int32, (sub, sub), 1)
    tri = jnp.where(tri, 1.0, 0.0).astype(BF16)
    carry = carry_ref[0:1, :]
    c_blocks = []
    for r in range(tl // sub):
        c3 = jnp.dot(tri, packed[r * sub:(r + 1) * sub], preferred_element_type=F32)
        c_blk = (c3 + pltpu.roll(c3, LANES - 1, 1)) + pltpu.roll(c3, LANES - 2, 1) + carry
        carry = c_blk[sub - 1:sub, :]
        c_blocks.append(c_blk)
    c = jnp.concatenate(c_blocks, axis=0) if len(c_blocks) > 1 else c_blocks[0]
    carry_ref[...] = jnp.broadcast_to(carry, (SUBLANES, LANES))
    cl_ref[0] = carry_ref[...]

    c_triples = _pack3(c * LOG2E, gate_lane)
    neg_triples = -c_triples
    if n_invalid:
        row = j * tl + lax.broadcasted_iota(jnp.int32, (tl, LANES), 0)
        key_mask = jnp.where(row >= n_invalid, 0.0, NEG)
    fox_base = 2 * RET_QK + 2 * RET_V
    fq_all = proj(fox_base, FOX_W) * (FOX_DH ** -0.5 * LOG2E)
    fk_all = proj(fox_base + FOX_W, FOX_W)
    fv_all = proj(fox_base + 2 * FOX_W, FOX_W)
    for pair in range(FOX_HEADS // 2):
        cols = slice(LANES * pair, LANES * (pair + 1))
        fq, fk, fv = fq_all[:, cols], fk_all[:, cols], fv_all[:, cols]
        for hh in range(2):
            h = 2 * pair + hh
            is_data = (lane < FOX_DH) if hh == 0 else (lane >= FOX_DH)
            e = (lane - FOX_DH) if hh == 0 else lane
            e0 = FOX_DH if hh == 0 else 0
            c_lanes = (e >= 0) & (e < 3)
            negc_lanes = (e >= 3) & (e < 6)
            qx = jnp.where(c_lanes, pltpu.roll(c_triples, (e0 - GATE_STRIDE * h) % LANES, 1),
                           jnp.where((e >= 3) & (e < N_EXTRA), 1.0, 0.0))
            kx = jnp.where(negc_lanes, pltpu.roll(neg_triples, (e0 + 3 - GATE_STRIDE * h) % LANES, 1),
                           jnp.where(c_lanes, 1.0, 0.0))
            if n_invalid:
                kx = jnp.where(e == 6, key_mask, kx)
            vx = jnp.where(e == 0, 1.0, 0.0)
            fq_ref[0, h] = jnp.where(is_data, fq, qx).astype(BF16)
            fk_ref[0, h] = jnp.where(is_data, fk, kx).astype(BF16)
            fv_ref[0, h] = jnp.where(is_data, fv, vx).astype(BF16)

    rqk = proj(0, 2 * RET_QK)
    for base, c_ref, s_ref, o_ref in ((0, cq_ref, sq_ref, rq_ref), (RET_QK, ck_ref, sk_ref, rk_ref)):
        for blk in range(RET_QK // LANES):
            xb = rqk[:, base + LANES * blk:base + LANES * (blk + 1)]
            swapped = jnp.where(first_half, pltpu.roll(xb, LANES - 32, 1), pltpu.roll(xb, 32, 1))
            o_ref[0, :, LANES * blk:LANES * (blk + 1)] = (xb * c_ref[...] + swapped * s_ref[...]).astype(BF16)

    rv_ref[0] = proj(2 * RET_QK, RET_V).astype(BF16)
    rg = proj(2 * RET_QK + RET_V, RET_V)
    rg_ref[0] = (rg * jax.nn.sigmoid(rg)).astype(BF16)


def _inproj(x, g, w_main, w_ff, fb, tabs, c0, *, tl, n_invalid):
    nb, rows, _ = x.shape
    grid = (nb, rows // tl)
    row_blk = lambda w: pl.BlockSpec((1, tl, w), lambda b, j: (b, j, 0))
    head_blk = pl.BlockSpec((1, FOX_HEADS, tl, LANES), lambda b, j: (b, 0, j, 0))
    const = lambda shape: pl.BlockSpec(shape, lambda b, j: (0,) * len(shape), pipeline_mode=pl.Buffered(1))
    tab_blk = pl.BlockSpec((tl, LANES), lambda b, j: (j, 0))
    out_shape = (
        jax.ShapeDtypeStruct((nb, rows, RET_QK), BF16), jax.ShapeDtypeStruct((nb, rows, RET_QK), BF16),
        jax.ShapeDtypeStruct((nb, rows, RET_V), BF16), jax.ShapeDtypeStruct((nb, rows, RET_V), BF16),
        jax.ShapeDtypeStruct((nb, FOX_HEADS, rows, LANES), BF16),
        jax.ShapeDtypeStruct((nb, FOX_HEADS, rows, LANES), BF16),
        jax.ShapeDtypeStruct((nb, FOX_HEADS, rows, LANES), BF16),
        jax.ShapeDtypeStruct((nb, SUBLANES, LANES), F32),
    )
    return pl.pallas_call(
        functools.partial(_inproj_kernel, tl=tl, n_invalid=n_invalid),
        grid=grid,
        in_specs=[row_blk(D_MODEL), const((1, D_MODEL)), const((D_MODEL, MAIN_W)), const((D_MODEL, LANES)),
                  const((1, LANES)), tab_blk, tab_blk, tab_blk, tab_blk, const((SUBLANES, LANES))],
        out_specs=(row_blk(RET_QK), row_blk(RET_QK), row_blk(RET_V), row_blk(RET_V),
                   head_blk, head_blk, head_blk, pl.BlockSpec((1, SUBLANES, LANES), lambda b, j: (b, 0, 0))),
        out_shape=out_shape,
        scratch_shapes=[pltpu.VMEM((SUBLANES, LANES), F32)],
        compiler_params=pltpu.CompilerParams(dimension_semantics=("arbitrary", "arbitrary"),
                                             vmem_limit_bytes=VMEM_LIMIT),
        name=f"inproj_{rows}",
    )(x, g, w_main, w_ff, fb, *tabs, c0)


def _retention_kernel(q_ref, k_ref, v_ref, gate_ref, r0_ref, d_ref, wq_ref, wk_ref, gt_ref, ng_ref,
                      o_ref, rf_ref, state_ref, *, t, n_blocks):
    state_ref[...] = r0_ref[...]
    lane = lax.broadcasted_iota(jnp.int32, (t, LANES), 1)

    def block(i, carry):
        rows = pl.ds(pl.multiple_of(i * t, t), t)
        q = q_ref[0, rows, :]
        k = k_ref[0, rows, :]
        v = v_ref[0, rows, :]
        qw = (q.astype(F32) * wq_ref[0]).astype(BF16)
        kw = (k.astype(F32) * wk_ref[0]).astype(BF16)
        for hh in range(2):
            in_head = (lane < RET_DK) if hh == 0 else (lane >= RET_DK)
            vh = v[:, RET_DV * hh:RET_DV * (hh + 1)]
            qm = jnp.where(in_head, q, jnp.zeros_like(q))
            s = lax.dot_general(qm, k, (((1,), (1,)), ((), ())), preferred_element_type=F32)
            intra = jnp.dot((s * d_ref[hh]).astype(BF16), vh, preferred_element_type=F32)
            r = state_ref[hh]
            inter = jnp.dot(qw, r.astype(BF16), preferred_element_type=F32)
            kwm = jnp.where(in_head, kw, jnp.zeros_like(kw))
            u = lax.dot_general(kwm, vh, (((0,), (0,)), ((), ())), preferred_element_type=F32)
            state_ref[hh] = gt_ref[hh] * r + u
            o = intra + inter
            cols = slice(RET_DV * hh, RET_DV * (hh + 1))
            o = (o * _rms(o)) * ng_ref[:, cols] * gate_ref[0, rows, cols].astype(F32)
            o_ref[0, rows, cols] = o.astype(BF16)
        return carry

    lax.fori_loop(0, n_blocks, block, 0, unroll=min(n_blocks, RET_UNROLL))
    rf_ref[0] = state_ref[...]


def _retention(rq, rk, rv, gate, r0, tabs, norm_g, *, t):
    nb, rows, _ = rq.shape
    d_tab, wq_tab, wk_tab, gt_tab = tabs
    n_pairs = RET_HEADS // 2
    seq = lambda w: pl.BlockSpec((1, rows, w), lambda b, p: (b, 0, p))
    return pl.pallas_call(
        functools.partial(_retention_kernel, t=t, n_blocks=rows // t),
        grid=(nb, n_pairs),
        in_specs=[seq(LANES), seq(LANES), seq(2 * RET_DV), seq(2 * RET_DV),
                  pl.BlockSpec((2, LANES, RET_DV), lambda b, p: (p, 0, 0)),
                  pl.BlockSpec((2, t, t), lambda b, p: (p, 0, 0)),
                  pl.BlockSpec((1, t, LANES), lambda b, p: (p, 0, 0)),
                  pl.BlockSpec((1, t, LANES), lambda b, p: (p, 0, 0)),
                  pl.BlockSpec((2, 1, RET_DV), lambda b, p: (p, 0, 0)),
                  pl.BlockSpec((1, 2 * RET_DV), lambda b, p: (0, p))],
        out_specs=(seq(2 * RET_DV), pl.BlockSpec((1, 2, LANES, RET_DV), lambda b, p: (b, p, 0, 0))),
        out_shape=(jax.ShapeDtypeStruct((nb, rows, RET_V), BF16),
                   jax.ShapeDtypeStruct((nb, RET_HEADS, LANES, RET_DV), F32)),
        scratch_shapes=[pltpu.VMEM((2, LANES, RET_DV), F32)],
        compiler_params=pltpu.CompilerParams(dimension_semantics=("arbitrary", "arbitrary"),
                                             vmem_limit_bytes=VMEM_LIMIT),
        name=f"retention_{rows}",
    )(rq, rk, rv, gate, r0, d_tab, wq_tab, wk_tab, gt_tab, norm_g)


def _fox_kernel(*refs, rows, tk, has_prefix):
    if has_prefix:
        q_ref, k_ref, v_ref, kp_ref, vp_ref, o_ref, m_ref, acc_ref = refs
    else:
        q_ref, k_ref, v_ref, o_ref, m_ref, acc_ref = refs
    lane = lax.broadcasted_iota(jnp.int32, (rows, LANES), 1)
    causal = lax.broadcasted_iota(jnp.int32, (tk, tk), 1) <= lax.broadcasted_iota(jnp.int32, (tk, tk), 0)

    def scores(hh, lo, k):
        return lax.dot_general(q_ref[0, hh, lo:rows, :], k, (((1,), (1,)), ((), ())), preferred_element_type=F32)

    def probs(s, m):
        return jnp.exp2((s - jnp.tile(m, (1, s.shape[1] // LANES))).astype(BF16))

    if has_prefix:
        for hh in range(2):
            s_pre = scores(hh, 0, kp_ref[0, hh])
            m = jnp.broadcast_to(jnp.max(s_pre, axis=1, keepdims=True), (rows, LANES))
            m_ref[hh] = m
            acc_ref[hh] = jnp.dot(probs(s_pre, m), vp_ref[0, hh], preferred_element_type=F32)
    else:
        m_ref[...] = jnp.full(m_ref.shape, NEG, F32)
        acc_ref[...] = jnp.zeros(acc_ref.shape, F32)

    for c in range(rows // tk):
        lo = c * tk
        for hh in range(2):
            s = scores(hh, lo, k_ref[0, hh, lo:lo + tk, :])
            s_top = jnp.where(causal, s[:tk], NEG)
            s = jnp.concatenate([s_top, s[tk:]], axis=0) if lo + tk < rows else s_top
            m_old = m_ref[hh, lo:rows]
            m_new = jnp.maximum(m_old, jnp.max(s, axis=1, keepdims=True))
            m_ref[hh, lo:rows] = m_new
            pv = jnp.dot(probs(s, m_new), v_ref[0, hh, lo:lo + tk, :], preferred_element_type=F32)
            acc_ref[hh, lo:rows] = jnp.exp2(m_old - m_new) * acc_ref[hh, lo:rows] + pv

    outs = []
    for hh in range(2):
        acc = acc_ref[hh]
        sum_lane = FOX_DH if hh == 0 else 0
        outs.append(acc / acc[:, sum_lane:sum_lane + 1])
    o_ref[0] = jnp.where(lane < FOX_DH, outs[0], outs[1]).astype(BF16)


def _fox(fq, fk, fv, kp, vp, *, tk):
    nb, _, rows, _ = fq.shape
    has_prefix = kp is not None
    pair_blk = pl.BlockSpec((1, 2, rows, LANES), lambda b, p: (b, p, 0, 0))
    in_specs = [pair_blk, pair_blk, pair_blk]
    args = [fq, fk, fv]
    if has_prefix:
        pre_blk = pl.BlockSpec((1, 2, PREFIX, LANES), lambda b, p: (0, p, 0, 0))
        in_specs += [pre_blk, pre_blk]
        args += [kp, vp]
    return pl.pallas_call(
        functools.partial(_fox_kernel, rows=rows, tk=tk, has_prefix=has_prefix),
        grid=(nb, FOX_HEADS // 2),
        in_specs=in_specs,
        out_specs=pl.BlockSpec((1, rows, LANES), lambda b, p: (b, 0, p)),
        out_shape=jax.ShapeDtypeStruct((nb, rows, FOX_W), BF16),
        scratch_shapes=[pltpu.VMEM((2, rows, LANES), F32), pltpu.VMEM((2, rows, LANES), F32)],
        compiler_params=pltpu.CompilerParams(dimension_semantics=("arbitrary", "arbitrary"),
                                             vmem_limit_bytes=VMEM_LIMIT),
        name=f"fox_{rows}",
    )(*args)


def _mixer_residual(x_ref, or_ref, of_ref, wo_ref):
    y = jnp.dot(or_ref[0], wo_ref[0:RET_V, :], preferred_element_type=F32)
    y = y + jnp.dot(of_ref[0], wo_ref[RET_V:RET_V + FOX_W, :], preferred_element_type=F32)
    return x_ref[0] + y


def _ffn_prefix_kernel(x_ref, or_ref, of_ref, wo_ref, g2_ref, wa_ref, a_ref, *, n_invalid):
    h1 = _mixer_residual(x_ref, or_ref, of_ref, wo_ref)
    hn = ((h1 * _rms(h1)) * g2_ref[...]).astype(BF16)
    a = jnp.dot(hn, wa_ref[...], preferred_element_type=F32)
    row = lax.broadcasted_iota(jnp.int32, a.shape, 0)
    a_ref[...] = jnp.where(row >= n_invalid, a, 0.0)


def _ffn_prefix(x, o_r, o_f, w_out, g2, w_up):
    full = lambda shape: pl.BlockSpec(shape, lambda i: (0,) * len(shape))
    return pl.pallas_call(
        functools.partial(_ffn_prefix_kernel, n_invalid=N_PAD),
        grid=(1,),
        in_specs=[full((1, PREFIX, D_MODEL)), full((1, PREFIX, RET_V)), full((1, PREFIX, FOX_W)),
                  full((D_MODEL, D_MODEL)), full((1, D_MODEL)), full((D_MODEL, D_FF))],
        out_specs=full((PREFIX, D_FF)),
        out_shape=jax.ShapeDtypeStruct((PREFIX, D_FF), F32),
        compiler_params=pltpu.CompilerParams(vmem_limit_bytes=VMEM_LIMIT),
        name="ffn_prefix",
    )(x, o_r, o_f, w_out, g2, w_up)


def _ffn_kernel(x_ref, or_ref, of_ref, wo_ref, g2_ref, wu_ref, cw_ref, cb_ref, wd_ref, gf_ref, halo_ref,
                o_ref, carry_ref, abuf_ref, gated_ref, *, tm):
    @pl.when(pl.program_id(1) == 0)
    def _():
        carry_ref[...] = halo_ref[...]

    h1 = _mixer_residual(x_ref, or_ref, of_ref, wo_ref)
    hn = ((h1 * _rms(h1)) * g2_ref[...]).astype(BF16)
    for c in range(D_FF // FF_CHUNK):
        cols = slice(c * FF_CHUNK, (c + 1) * FF_CHUNK)
        a = jnp.dot(hn, wu_ref[:, cols], preferred_element_type=F32)
        b = jnp.dot(hn, wu_ref[:, D_FF + c * FF_CHUNK:D_FF + (c + 1) * FF_CHUNK], preferred_element_type=F32)
        abuf_ref[0:SUBLANES, :] = carry_ref[:, cols]
        abuf_ref[SUBLANES:SUBLANES + tm, :] = a
        carry_ref[:, cols] = a[tm - SUBLANES:tm, :]
        acc = cb_ref[:, cols] + abuf_ref[SUBLANES - 2:SUBLANES - 2 + tm, :] * cw_ref[0:1, cols]
        acc = acc + abuf_ref[SUBLANES - 1:SUBLANES - 1 + tm, :] * cw_ref[1:2, cols]
        acc = acc + a * cw_ref[2:3, cols]
        gated_ref[:, cols] = (acc * jax.nn.sigmoid(acc) * b).astype(BF16)
    h2 = h1 + jnp.dot(gated_ref[...], wd_ref[...], preferred_element_type=F32)
    o_ref[0] = (h2 * _rms(h2)) * gf_ref[...]


def _ffn(x, o_r, o_f, w_out, g2, w_up, conv_w, conv_b, w_down, gf, halo, *, tm):
    nb, rows, _ = x.shape
    row_blk = lambda w: pl.BlockSpec((1, tm, w), lambda b, j: (b, j, 0))
    const = lambda shape: pl.BlockSpec(shape, lambda b, j: (0,) * len(shape), pipeline_mode=pl.Buffered(1))
    return pl.pallas_call(
        functools.partial(_ffn_kernel, tm=tm),
        grid=(nb, rows // tm),
        in_specs=[row_blk(D_MODEL), row_blk(RET_V), row_blk(FOX_W), const((D_MODEL, D_MODEL)),
                  const((1, D_MODEL)), const((D_MODEL, 2 * D_FF)), const((CONV_W, D_FF)), const((1, D_FF)),
                  const((D_FF, D_MODEL)), const((1, D_MODEL)), const((SUBLANES, D_FF))],
        out_specs=row_blk(D_MODEL),
        out_shape=jax.ShapeDtypeStruct((nb, rows, D_MODEL), F32),
        scratch_shapes=[pltpu.VMEM((SUBLANES, D_FF), F32), pltpu.VMEM((SUBLANES + tm, FF_CHUNK), F32),
                        pltpu.VMEM((tm, D_FF), BF16)],
        compiler_params=pltpu.CompilerParams(dimension_semantics=("arbitrary", "arbitrary"),
                                             vmem_limit_bytes=VMEM_LIMIT),
        name="ffn",
    )(x, o_r, o_f, w_out, g2, w_up, conv_w, conv_b, w_down, gf, halo)


def _rotary_tables(start, length):
    half = RET_DK // 2
    inv = (1.0 / (ROPE_BASE ** (np.arange(half, dtype=np.float32) / np.float32(half)))).astype(np.float32)
    pos = start + np.arange(length)
    ang = pos.astype(np.float32)[:, None] * inv[None, :]
    cos = np.tile(np.cos(ang), (1, LANES // half))
    sin = np.tile(np.concatenate([-np.sin(ang), np.sin(ang)], axis=1), (1, LANES // RET_DK))
    k_scale = (np.float32(RET_DK ** -0.5) * (pos >= N_PAD).astype(np.float32))[:, None]
    return cos, sin, cos * k_scale, sin * k_scale


def _retention_tables(t):
    f32 = np.float32
    log_g = np.log1p(-np.exp2(f32(-5.0) - np.arange(RET_HEADS, dtype=f32)))
    n = np.arange(t, dtype=f32)
    visible = (np.arange(t)[None, :] // CHUNK) <= (np.arange(t)[:, None] // CHUNK)
    d = np.exp(np.abs(n[:, None] - n[None, :])[None] * log_g[:, None, None]) * visible[None].astype(f32)
    lane_log_g = np.repeat(log_g, RET_DK).reshape(RET_HEADS // 2, 1, LANES)
    wq = np.exp((n + f32(1.0))[None, :, None] * lane_log_g)
    wk = np.exp((f32(t) - f32(1.0) - n)[None, :, None] * lane_log_g)
    gt = np.broadcast_to(np.exp(f32(t) * log_g)[:, None, None], (RET_HEADS, 1, RET_DV))
    return tuple(np.ascontiguousarray(a, dtype=f32) for a in (d, wq, wk, gt))


def _gate_lanes(v):
    spread = jnp.stack([v] + [jnp.zeros_like(v)] * (GATE_STRIDE - 1), axis=-1)
    spread = spread.reshape(v.shape[:-1] + (GATE_STRIDE * FOX_HEADS,))
    return jnp.pad(spread, [(0, 0)] * (v.ndim - 1) + [(0, LANES - GATE_STRIDE * FOX_HEADS)])


def kernel(x, meta_tokens, attn_norm_g, w_in, fox_forget_b, ret_norm_g, w_out, ffn_norm_g, w_up, conv_w,
           conv_b, w_down, final_norm_g):
    assert w_in.shape[0] == 1, "single-layer block"
    batch, seq, _ = x.shape
    w_main = w_in[0].astype(BF16)
    w_ff = _gate_lanes(w_in[0, :, MAIN_W:]).astype(BF16)
    fb = _gate_lanes(fox_forget_b[0]).reshape(1, LANES)
    g1 = attn_norm_g[0].reshape(1, D_MODEL)
    g2 = ffn_norm_g[0].reshape(1, D_MODEL)
    gf = final_norm_g.reshape(1, D_MODEL)
    ng = ret_norm_g[0].reshape(1, RET_V)
    w_out_b = w_out[0].astype(BF16)
    w_up_b = w_up[0].astype(BF16)
    w_down_b = w_down[0].astype(BF16)
    cb = conv_b[0].reshape(1, D_FF)
    h_pre = jnp.concatenate([jnp.zeros((N_PAD, D_MODEL), x.dtype), meta_tokens.astype(x.dtype)], axis=0)[None]

    zeros_c = jnp.zeros((SUBLANES, LANES), F32)
    rq0, rk0, rv0, rg0, fq0, fk0, fv0, c_pre = _inproj(h_pre, g1, w_main, w_ff, fb, _rotary_tables(0, PREFIX),
                                                        zeros_c, tl=PREFIX, n_invalid=N_PAD)
    r_zero = jnp.zeros((RET_HEADS, LANES, RET_DV), F32)
    or0, r_pre = _retention(rq0, rk0, rv0, rg0, r_zero, _retention_tables(PREFIX), ng, t=PREFIX)
    of0 = _fox(fq0, fk0, fv0, None, None, tk=PREFIX)
    a_pre = _ffn_prefix(h_pre, or0, of0, w_out_b, g2, w_up_b)
    halo = a_pre[PREFIX - SUBLANES:]

    rq, rk, rv, rg, fq, fk, fv, _ = _inproj(x, g1, w_main, w_ff, fb, _rotary_tables(PREFIX, seq), c_pre[0],
                                            tl=1024, n_invalid=0)
    o_r, _ = _retention(rq, rk, rv, rg, r_pre[0], _retention_tables(256), ng, t=256)
    o_f = _fox(fq, fk, fv, fk0, fv0, tk=512)
    return _ffn(x, o_r, o_f, w_out_b, g2, w_up_b, conv_w[0], cb, w_down_b, gf, halo, tm=1024)
```

```python
import functools

import jax
import jax.numpy as jnp
import numpy as np
from jax import lax
from jax.experimental import pallas as pl
from jax.experimental.pallas import tpu as pltpu

F32 = jnp.float32
BF16 = jnp.bfloat16

D_MODEL = 1024
CHUNK = 64
N_META = 16
PREFIX = 128
N_PAD = PREFIX - N_META
RET_HEADS = 4
RET_DK = 64
RET_DV = 128
FOX_HEADS = 8
FOX_DH = 64
D_FF = 2816
CONV_W = 3
ROPE_BASE = 10000.0
EPS = 1e-6
NEG = -1e30

RET_QK = RET_HEADS * RET_DK
RET_V = RET_HEADS * RET_DV
FOX_W = FOX_HEADS * FOX_DH
MAIN_W = 2 * RET_QK + 2 * RET_V + 3 * FOX_W
LANES = 128
SUBLANES = 8
N_EXTRA = 7
GATE_STRIDE = 3
LOG2E = 1.4426950408889634
CUMSUM_ROWS = 256
VMEM_LIMIT = 56 * 1024 * 1024
FF_CHUNK = 256
FOX_ROW_PIECE = 1024
RET_UNROLL = 8


def _rms(x):
    return lax.rsqrt(jnp.mean(x * x, axis=-1, keepdims=True) + EPS)


def _split3(c):
    hi = c.astype(BF16).astype(F32)
    r = c - hi
    mid = r.astype(BF16).astype(F32)
    return hi, mid, r - mid


def _pack3(c, gate_lane):
    hi, mid, lo = (jnp.where(gate_lane, part, 0.0) for part in _split3(c))
    return hi + pltpu.roll(mid, 1, 1) + pltpu.roll(lo, 2, 1)


def _inproj_kernel(x_ref, g_ref, wm_ref, wf_ref, fb_ref, cq_ref, sq_ref, ck_ref, sk_ref, c0_ref,
                   rq_ref, rk_ref, rv_ref, rg_ref, fq_ref, fk_ref, fv_ref, cl_ref, carry_ref,
                   *, tl, n_invalid):
    j = pl.program_id(1)

    @pl.when(j == 0)
    def _():
        carry_ref[...] = c0_ref[...]

    x = x_ref[0]
    hn = ((x * _rms(x)) * g_ref[...]).astype(BF16)

    def proj(lo, width):
        return jnp.dot(hn, wm_ref[:, lo:lo + width], preferred_element_type=F32)

    lane = lax.broadcasted_iota(jnp.int32, (1, LANES), 1)
    first_half = (lane & 32) == 0

    gate_lane = functools.reduce(jnp.logical_or, [lane == GATE_STRIDE * h for h in range(FOX_HEADS)])
    ff = jnp.dot(hn, wf_ref[...], preferred_element_type=F32) + fb_ref[...]
    lf = jnp.minimum(ff, 0.0) - jnp.log(1.0 + jnp.exp(-jnp.abs(ff)))
    packed = _pack3(lf, gate_lane).astype(BF16)
    sub = min(tl, CUMSUM_ROWS)
    tri = lax.broadcasted_iota(jnp.int32, (sub, sub), 0) >= lax.broadcasted_iota(jnp.int32, (sub, sub), 1)
    tri = jnp.where(tri, 1.0, 0.0).astype(BF16)
    carry = carry_ref[0:1, :]
    c_blocks = []
    for r in range(tl // sub):
        c3 = jnp.dot(tri, packed[r * sub:(r + 1) * sub], preferred_element_type=F32)
        c_blk = (c3 + pltpu.roll(c3, LANES - 1, 1)) + pltpu.roll(c3, LANES - 2, 1) + carry
        carry = c_blk[sub - 1:sub, :]
        c_blocks.append(c_blk)
    c = jnp.concatenate(c_blocks, axis=0) if len(c_blocks) > 1 else c_blocks[0]
    carry_ref[...] = jnp.broadcast_to(carry, (SUBLANES, LANES))
    cl_ref[0] = carry_ref[...]

    c_triples = _pack3(c * LOG2E, gate_lane)
    neg_triples = -c_triples
    if n_invalid:
        row = j * tl + lax.broadcasted_iota(jnp.int32, (tl, LANES), 0)
        key_mask = jnp.where(row >= n_invalid, 0.0, NEG)
    fox_base = 2 * RET_QK + 2 * RET_V
    fq_all = proj(fox_base, FOX_W) * (FOX_DH ** -0.5 * LOG2E)
    fk_all = proj(fox_base + FOX_W, FOX_W)
    fv_all = proj(fox_base + 2 * FOX_W, FOX_W)
    for pair in range(FOX_HEADS // 2):
        cols = slice(LANES * pair, LANES * (pair + 1))
        fq, fk, fv = fq_all[:, cols], fk_all[:, cols], fv_all[:, cols]
        for hh in range(2):
            h = 2 * pair + hh
            is_data = (lane < FOX_DH) if hh == 0 else (lane >= FOX_DH)
            e = (lane - FOX_DH) if hh == 0 else lane
            e0 = FOX_DH if hh == 0 else 0
            c_lanes = (e >= 0) & (e < 3)
            negc_lanes = (e >= 3) & (e < 6)
            qx = jnp.where(c_lanes, pltpu.roll(c_triples, (e0 - GATE_STRIDE * h) % LANES, 1),
                           jnp.where((e >= 3) & (e < N_EXTRA), 1.0, 0.0))
            kx = jnp.where(negc_lanes, pltpu.roll(neg_triples, (e0 + 3 - GATE_STRIDE * h) % LANES, 1),
                           jnp.where(c_lanes, 1.0, 0.0))
            if n_invalid:
                kx = jnp.where(e == 6, key_mask, kx)
            vx = jnp.where(e == 0, 1.0, 0.0)
            fq_ref[0, h] = jnp.where(is_data, fq, qx).astype(BF16)
            fk_ref[0, h] = jnp.where(is_data, fk, kx).astype(BF16)
            fv_ref[0, h] = jnp.where(is_data, fv, vx).astype(BF16)

    rqk = proj(0, 2 * RET_QK)
    for base, c_ref, s_ref, o_ref in ((0, cq_ref, sq_ref, rq_ref), (RET_QK, ck_ref, sk_ref, rk_ref)):
        for blk in range(RET_QK // LANES):
            xb = rqk[:, base + LANES * blk:base + LANES * (blk + 1)]
            swapped = jnp.where(first_half, pltpu.roll(xb, LANES - 32, 1), pltpu.roll(xb, 32, 1))
            o_ref[0, :, LANES * blk:LANES * (blk + 1)] = (xb * c_ref[...] + swapped * s_ref[...]).astype(BF16)

    rv_ref[0] = proj(2 * RET_QK, RET_V).astype(BF16)
    rg = proj(2 * RET_QK + RET_V, RET_V)
    rg_ref[0] = (rg * jax.nn.sigmoid(rg)).astype(BF16)


def _inproj(x, g, w_main, w_ff, fb, tabs, c0, *, tl, n_invalid):
    nb, rows, _ = x.shape
    grid = (nb, rows // tl)
    row_blk = lambda w: pl.BlockSpec((1, tl, w), lambda b, j: (b, j, 0))
    head_blk = pl.BlockSpec((1, FOX_HEADS, tl, LANES), lambda b, j: (b, 0, j, 0))
    const = lambda shape: pl.BlockSpec(shape, lambda b, j: (0,) * len(shape), pipeline_mode=pl.Buffered(1))
    tab_blk = pl.BlockSpec((tl, LANES), lambda b, j: (j, 0))
    out_shape = (
        jax.ShapeDtypeStruct((nb, rows, RET_QK), BF16), jax.ShapeDtypeStruct((nb, rows, RET_QK), BF16),
        jax.ShapeDtypeStruct((nb, rows, RET_V), BF16), jax.ShapeDtypeStruct((nb, rows, RET_V), BF16),
        jax.ShapeDtypeStruct((nb, FOX_HEADS, rows, LANES), BF16),
        jax.ShapeDtypeStruct((nb, FOX_HEADS, rows, LANES), BF16),
        jax.ShapeDtypeStruct((nb, FOX_HEADS, rows, LANES), BF16),
        jax.ShapeDtypeStruct((nb, SUBLANES, LANES), F32),
    )
    return pl.pallas_call(
        functools.partial(_inproj_kernel, tl=tl, n_invalid=n_invalid),
        grid=grid,
        in_specs=[row_blk(D_MODEL), const((1, D_MODEL)), const((D_MODEL, MAIN_W)), const((D_MODEL, LANES)),
                  const((1, LANES)), tab_blk, tab_blk, tab_blk, tab_blk, const((SUBLANES, LANES))],
        out_specs=(row_blk(RET_QK), row_blk(RET_QK), row_blk(RET_V), row_blk(RET_V),
                   head_blk, head_blk, head_blk, pl.BlockSpec((1, SUBLANES, LANES), lambda b, j: (b, 0, 0))),
        out_shape=out_shape,
        scratch_shapes=[pltpu.VMEM((SUBLANES, LANES), F32)],
        compiler_params=pltpu.CompilerParams(dimension_semantics=("arbitrary", "arbitrary"),
                                             vmem_limit_bytes=VMEM_LIMIT),
        name=f"inproj_{rows}",
    )(x, g, w_main, w_ff, fb, *tabs, c0)


def _retention_kernel(q_ref, k_ref, v_ref, gate_ref, r0_ref, d_ref, wq_ref, wk_ref, gt_ref, ng_ref,
                      o_ref, rf_ref, state_ref, *, t, n_blocks):
    state_ref[...] = r0_ref[...]
    lane = lax.broadcasted_iota(jnp.int32, (t, LANES), 1)

    def block(i, carry):
        rows = pl.ds(pl.multiple_of(i * t, t), t)
        q = q_ref[0, rows, :]
        k = k_ref[0, rows, :]
        v = v_ref[0, rows, :]
        qw = (q.astype(F32) * wq_ref[0]).astype(BF16)
        kw = (k.astype(F32) * wk_ref[0]).astype(BF16)
        for hh in range(2):
            in_head = (lane < RET_DK) if hh == 0 else (lane >= RET_DK)
            vh = v[:, RET_DV * hh:RET_DV * (hh + 1)]
            qm = jnp.where(in_head, q, jnp.zeros_like(q))
            s = lax.dot_general(qm, k, (((1,), (1,)), ((), ())), preferred_element_type=F32)
            intra = jnp.dot((s * d_ref[hh]).astype(BF16), vh, preferred_element_type=F32)
            r = state_ref[hh]
            inter = jnp.dot(qw, r.astype(BF16), preferred_element_type=F32)
            kwm = jnp.where(in_head, kw, jnp.zeros_like(kw))
            u = lax.dot_general(kwm, vh, (((0,), (0,)), ((), ())), preferred_element_type=F32)
            state_ref[hh] = gt_ref[hh] * r + u
            o = intra + inter
            cols = slice(RET_DV * hh, RET_DV * (hh + 1))
            o = (o * _rms(o)) * ng_ref[:, cols] * gate_ref[0, rows, cols].astype(F32)
            o_ref[0, rows, cols] = o.astype(BF16)
        return carry

    lax.fori_loop(0, n_blocks, block, 0, unroll=min(n_blocks, RET_UNROLL))
    rf_ref[0] = state_ref[...]


def _retention(rq, rk, rv, gate, r0, tabs, norm_g, *, t):
    nb, rows, _ = rq.shape
    d_tab, wq_tab, wk_tab, gt_tab = tabs
    n_pairs = RET_HEADS // 2
    seq = lambda w: pl.BlockSpec((1, rows, w), lambda b, p: (b, 0, p))
    return pl.pallas_call(
        functools.partial(_retention_kernel, t=t, n_blocks=rows // t),
        grid=(nb, n_pairs),
        in_specs=[seq(LANES), seq(LANES), seq(2 * RET_DV), seq(2 * RET_DV),
                  pl.BlockSpec((2, LANES, RET_DV), lambda b, p: (p, 0, 0)),
                  pl.BlockSpec((2, t, t), lambda b, p: (p, 0, 0)),
                  pl.BlockSpec((1, t, LANES), lambda b, p: (p, 0, 0)),
                  pl.BlockSpec((1, t, LANES), lambda b, p: (p, 0, 0)),
                  pl.BlockSpec((2, 1, RET_DV), lambda b, p: (p, 0, 0)),
                  pl.BlockSpec((1, 2 * RET_DV), lambda b, p: (0, p))],
        out_specs=(seq(2 * RET_DV), pl.BlockSpec((1, 2, LANES, RET_DV), lambda b, p: (b, p, 0, 0))),
        out_shape=(jax.ShapeDtypeStruct((nb, rows, RET_V), BF16),
                   jax.ShapeDtypeStruct((nb, RET_HEADS, LANES, RET_DV), F32)),
        scratch_shapes=[pltpu.VMEM((2, LANES, RET_DV), F32)],
        compiler_params=pltpu.CompilerParams(dimension_semantics=("arbitrary", "arbitrary"),
                                             vmem_limit_bytes=VMEM_LIMIT),
        name=f"retention_{rows}",
    )(rq, rk, rv, gate, r0, d_tab, wq_tab, wk_tab, gt_tab, norm_g)


def _fox_kernel(*refs, rows, tk, has_prefix):
    if has_prefix:
        q_ref, k_ref, v_ref, kp_ref, vp_ref, o_ref, m_ref, acc_ref = refs
    else:
        q_ref, k_ref, v_ref, o_ref, m_ref, acc_ref = refs
    lane = lax.broadcasted_iota(jnp.int32, (rows, LANES), 1)
    causal = lax.broadcasted_iota(jnp.int32, (tk, tk), 1) <= lax.broadcasted_iota(jnp.int32, (tk, tk), 0)
    row_piece = min(rows, FOX_ROW_PIECE)

    def scores(hh, lo, k):
        return lax.dot_general(q_ref[0, hh, lo:rows, :], k, (((1,), (1,)), ((), ())), preferred_element_type=F32)

    def probs(s, m):
        return jnp.exp2((s - jnp.tile(m, (1, s.shape[1] // LANES))).astype(BF16))

    if has_prefix:
        for hh in range(2):
            s_pre = scores(hh, 0, kp_ref[0, hh])
            m = jnp.broadcast_to(jnp.max(s_pre, axis=1, keepdims=True), (rows, LANES))
            m_ref[hh] = m
            acc_ref[hh] = jnp.dot(probs(s_pre, m), vp_ref[0, hh], preferred_element_type=F32)
    else:
        m_ref[...] = jnp.full(m_ref.shape, NEG, F32)
        acc_ref[...] = jnp.zeros(acc_ref.shape, F32)

    for c in range(rows // tk):
        lo = c * tk
        for r0 in range(lo, rows, row_piece):
            r1 = min(r0 + row_piece, rows)
            for hh in range(2):
                s = lax.dot_general(q_ref[0, hh, r0:r1, :], k_ref[0, hh, lo:lo + tk, :],
                                    (((1,), (1,)), ((), ())), preferred_element_type=F32)
                if r0 == lo:
                    s_top = jnp.where(causal, s[:tk], NEG)
                    s = jnp.concatenate([s_top, s[tk:]], axis=0) if r1 - r0 > tk else s_top
                m_old = m_ref[hh, r0:r1]
                m_new = jnp.maximum(m_old, jnp.max(s, axis=1, keepdims=True))
                m_ref[hh, r0:r1] = m_new
                pv = jnp.dot(probs(s, m_new), v_ref[0, hh, lo:lo + tk, :], preferred_element_type=F32)
                acc_ref[hh, r0:r1] = jnp.exp2(m_old - m_new) * acc_ref[hh, r0:r1] + pv

    outs = []
    for hh in range(2):
        acc = acc_ref[hh]
        sum_lane = FOX_DH if hh == 0 else 0
        outs.append(acc / acc[:, sum_lane:sum_lane + 1])
    o_ref[0] = jnp.where(lane < FOX_DH, outs[0], outs[1]).astype(BF16)


def _fox(fq, fk, fv, kp, vp, *, tk):
    nb, _, rows, _ = fq.shape
    has_prefix = kp is not None
    pair_blk = pl.BlockSpec((1, 2, rows, LANES), lambda b, p: (b, p, 0, 0))
    in_specs = [pair_blk, pair_blk, pair_blk]
    args = [fq, fk, fv]
    if has_prefix:
        pre_blk = pl.BlockSpec((1, 2, PREFIX, LANES), lambda b, p: (0, p, 0, 0))
        in_specs += [pre_blk, pre_blk]
        args += [kp, vp]
    return pl.pallas_call(
        functools.partial(_fox_kernel, rows=rows, tk=tk, has_prefix=has_prefix),
        grid=(nb, FOX_HEADS // 2),
        in_specs=in_specs,
        out_specs=pl.BlockSpec((1, rows, LANES), lambda b, p: (b, 0, p)),
        out_shape=jax.ShapeDtypeStruct((nb, rows, FOX_W), BF16),
        scratch_shapes=[pltpu.VMEM((2, rows, LANES), F32), pltpu.VMEM((2, rows, LANES), F32)],
        compiler_params=pltpu.CompilerParams(dimension_semantics=("arbitrary", "arbitrary"),
                                             vmem_limit_bytes=VMEM_LIMIT),
        name=f"fox_{rows}",
    )(*args)


def _mixer_residual(x_ref, or_ref, of_ref, wo_ref):
    y = jnp.dot(or_ref[0], wo_ref[0:RET_V, :], preferred_element_type=F32)
    y = y + jnp.dot(of_ref[0], wo_ref[RET_V:RET_V + FOX_W, :], preferred_element_type=F32)
    return x_ref[0] + y


def _ffn_prefix_kernel(x_ref, or_ref, of_ref, wo_ref, g2_ref, wa_ref, a_ref, *, n_invalid):
    h1 = _mixer_residual(x_ref, or_ref, of_ref, wo_ref)
    hn = ((h1 * _rms(h1)) * g2_ref[...]).astype(BF16)
    a = jnp.dot(hn, wa_ref[...], preferred_element_type=F32)
    row = lax.broadcasted_iota(jnp.int32, a.shape, 0)
    a_ref[...] = jnp.where(row >= n_invalid, a, 0.0)


def _ffn_prefix(x, o_r, o_f, w_out, g2, w_up):
    full = lambda shape: pl.BlockSpec(shape, lambda i: (0,) * len(shape))
    return pl.pallas_call(
        functools.partial(_ffn_prefix_kernel, n_invalid=N_PAD),
        grid=(1,),
        in_specs=[full((1, PREFIX, D_MODEL)), full((1, PREFIX, RET_V)), full((1, PREFIX, FOX_W)),
                  full((D_MODEL, D_MODEL)), full((1, D_MODEL)), full((D_MODEL, D_FF))],
        out_specs=full((PREFIX, D_FF)),
        out_shape=jax.ShapeDtypeStruct((PREFIX, D_FF), F32),
        compiler_params=pltpu.CompilerParams(vmem_limit_bytes=VMEM_LIMIT),
        name="ffn_prefix",
    )(x, o_r, o_f, w_out, g2, w_up)


def _ffn_kernel(x_ref, or_ref, of_ref, wo_ref, g2_ref, wu_ref, cw_ref, cb_ref, wd_ref, gf_ref, halo_ref,
                o_ref, carry_ref, abuf_ref, gated_ref, *, tm):
    @pl.when(pl.program_id(1) == 0)
    def _():
        carry_ref[...] = halo_ref[...]

    h1 = _mixer_residual(x_ref, or_ref, of_ref, wo_ref)
    hn = ((h1 * _rms(h1)) * g2_ref[...]).astype(BF16)
    for c in range(D_FF // FF_CHUNK):
        cols = slice(c * FF_CHUNK, (c + 1) * FF_CHUNK)
        a = jnp.dot(hn, wu_ref[:, cols], preferred_element_type=F32)
        b = jnp.dot(hn, wu_ref[:, D_FF + c * FF_CHUNK:D_FF + (c + 1) * FF_CHUNK], preferred_element_type=F32)
        abuf_ref[0:SUBLANES, :] = carry_ref[:, cols]
        abuf_ref[SUBLANES:SUBLANES + tm, :] = a
        carry_ref[:, cols] = a[tm - SUBLANES:tm, :]
        acc = cb_ref[:, cols] + abuf_ref[SUBLANES - 2:SUBLANES - 2 + tm, :] * cw_ref[0:1, cols]
        acc = acc + abuf_ref[SUBLANES - 1:SUBLANES - 1 + tm, :] * cw_ref[1:2, cols]
        acc = acc + a * cw_ref[2:3, cols]
        gated_ref[:, cols] = (acc * jax.nn.sigmoid(acc) * b).astype(BF16)
    h2 = h1 + jnp.dot(gated_ref[...], wd_ref[...], preferred_element_type=F32)
    o_ref[0] = (h2 * _rms(h2)) * gf_ref[...]


def _ffn(x, o_r, o_f, w_out, g2, w_up, conv_w, conv_b, w_down, gf, halo, *, tm):
    nb, rows, _ = x.shape
    row_blk = lambda w: pl.BlockSpec((1, tm, w), lambda b, j: (b, j, 0))
    const = lambda shape: pl.BlockSpec(shape, lambda b, j: (0,) * len(shape), pipeline_mode=pl.Buffered(1))
    return pl.pallas_call(
        functools.partial(_ffn_kernel, tm=tm),
        grid=(nb, rows // tm),
        in_specs=[row_blk(D_MODEL), row_blk(RET_V), row_blk(FOX_W), const((D_MODEL, D_MODEL)),
                  const((1, D_MODEL)), const((D_MODEL, 2 * D_FF)), const((CONV_W, D_FF)), const((1, D_FF)),
                  const((D_FF, D_MODEL)), const((1, D_MODEL)), const((SUBLANES, D_FF))],
        out_specs=row_blk(D_MODEL),
        out_shape=jax.ShapeDtypeStruct((nb, rows, D_MODEL), F32),
        scratch_shapes=[pltpu.VMEM((SUBLANES, D_FF), F32), pltpu.VMEM((SUBLANES + tm, FF_CHUNK), F32),
                        pltpu.VMEM((tm, D_FF), BF16)],
        compiler_params=pltpu.CompilerParams(dimension_semantics=("arbitrary", "arbitrary"),
                                             vmem_limit_bytes=VMEM_LIMIT),
        name="ffn",
    )(x, o_r, o_f, w_out, g2, w_up, conv_w, conv_b, w_down, gf, halo)


def _rotary_tables(start, length):
    half = RET_DK // 2
    inv = (1.0 / (ROPE_BASE ** (np.arange(half, dtype=np.float32) / np.float32(half)))).astype(np.float32)
    pos = start + np.arange(length)
    ang = pos.astype(np.float32)[:, None] * inv[None, :]
    cos = np.tile(np.cos(ang), (1, LANES // half))
    sin = np.tile(np.concatenate([-np.sin(ang), np.sin(ang)], axis=1), (1, LANES // RET_DK))
    k_scale = (np.float32(RET_DK ** -0.5) * (pos >= N_PAD).astype(np.float32))[:, None]
    return cos, sin, cos * k_scale, sin * k_scale


def _retention_tables(t):
    f32 = np.float32
    log_g = np.log1p(-np.exp2(f32(-5.0) - np.arange(RET_HEADS, dtype=f32)))
    n = np.arange(t, dtype=f32)
    visible = (np.arange(t)[None, :] // CHUNK) <= (np.arange(t)[:, None] // CHUNK)
    d = np.exp(np.abs(n[:, None] - n[None, :])[None] * log_g[:, None, None]) * visible[None].astype(f32)
    lane_log_g = np.repeat(log_g, RET_DK).reshape(RET_HEADS // 2, 1, LANES)
    wq = np.exp((n + f32(1.0))[None, :, None] * lane_log_g)
    wk = np.exp((f32(t) - f32(1.0) - n)[None, :, None] * lane_log_g)
    gt = np.broadcast_to(np.exp(f32(t) * log_g)[:, None, None], (RET_HEADS, 1, RET_DV))
    return tuple(np.ascontiguousarray(a, dtype=f32) for a in (d, wq, wk, gt))


def _gate_lanes(v):
    spread = jnp.stack([v] + [jnp.zeros_like(v)] * (GATE_STRIDE - 1), axis=-1)
    spread = spread.reshape(v.shape[:-1] + (GATE_STRIDE * FOX_HEADS,))
    return jnp.pad(spread, [(0, 0)] * (v.ndim - 1) + [(0, LANES - GATE_STRIDE * FOX_HEADS)])


def kernel(x, meta_tokens, attn_norm_g, w_in, fox_forget_b, ret_norm_g, w_out, ffn_norm_g, w_up, conv_w,
           conv_b, w_down, final_norm_g):
    assert w_in.shape[0] == 1, "single-layer block"
    batch, seq, _ = x.shape
    w_main = w_in[0].astype(BF16)
    w_ff = _gate_lanes(w_in[0, :, MAIN_W:]).astype(BF16)
    fb = _gate_lanes(fox_forget_b[0]).reshape(1, LANES)
    g1 = attn_norm_g[0].reshape(1, D_MODEL)
    g2 = ffn_norm_g[0].reshape(1, D_MODEL)
    gf = final_norm_g.reshape(1, D_MODEL)
    ng = ret_norm_g[0].reshape(1, RET_V)
    w_out_b = w_out[0].astype(BF16)
    w_up_b = w_up[0].astype(BF16)
    w_down_b = w_down[0].astype(BF16)
    cb = conv_b[0].reshape(1, D_FF)
    h_pre = jnp.concatenate([jnp.zeros((N_PAD, D_MODEL), x.dtype), meta_tokens.astype(x.dtype)], axis=0)[None]

    zeros_c = jnp.zeros((SUBLANES, LANES), F32)
    rq0, rk0, rv0, rg0, fq0, fk0, fv0, c_pre = _inproj(h_pre, g1, w_main, w_ff, fb, _rotary_tables(0, PREFIX),
                                                        zeros_c, tl=PREFIX, n_invalid=N_PAD)
    r_zero = jnp.zeros((RET_HEADS, LANES, RET_DV), F32)
    or0, r_pre = _retention(rq0, rk0, rv0, rg0, r_zero, _retention_tables(PREFIX), ng, t=PREFIX)
    of0 = _fox(fq0, fk0, fv0, None, None, tk=PREFIX)
    a_pre = _ffn_prefix(h_pre, or0, of0, w_out_b, g2, w_up_b)
    halo = a_pre[PREFIX - SUBLANES:]

    rq, rk, rv, rg, fq, fk, fv, _ = _inproj(x, g1, w_main, w_ff, fb, _rotary_tables(PREFIX, seq), c_pre[0],
                                            tl=1024, n_invalid=0)
    o_r, _ = _retention(rq, rk, rv, rg, r_pre[0], _retention_tables(256), ng, t=256)
    o_f = _fox(fq, fk, fv, fk0, fv0, tk=512)
    return _ffn(x, o_r, o_f, w_out_b, g2, w_up_b, conv_w[0], cb, w_down_b, gf, halo, tm=1024)
```

```python
import functools

import jax
import jax.numpy as jnp
import numpy as np
from jax import lax
from jax.experimental import pallas as pl
from jax.experimental.pallas import tpu as pltpu

F32 = jnp.float32
BF16 = jnp.bfloat16

D_MODEL = 1024
CHUNK = 64
N_META = 16
PREFIX = 128
N_PAD = PREFIX - N_META
RET_HEADS = 4
RET_DK = 64
RET_DV = 128
FOX_HEADS = 8
FOX_DH = 64
D_FF = 2816
CONV_W = 3
ROPE_BASE = 10000.0
EPS = 1e-6
NEG = -1e30

RET_QK = RET_HEADS * RET_DK
RET_V = RET_HEADS * RET_DV
FOX_W = FOX_HEADS * FOX_DH
MAIN_W = 2 * RET_QK + 2 * RET_V + 3 * FOX_W
LANES = 128
SUBLANES = 8
N_EXTRA = 7
GATE_STRIDE = 3
LOG2E = 1.4426950408889634
CUMSUM_ROWS = 256
VMEM_LIMIT = 56 * 1024 * 1024
FF_CHUNK = 256
RET_UNROLL = 8


def _rms(x):
    return lax.rsqrt(jnp.mean(x * x, axis=-1, keepdims=True) + EPS)


def _split3(c):
    hi = c.astype(BF16).astype(F32)
    r = c - hi
    mid = r.astype(BF16).astype(F32)
    return hi, mid, r - mid


def _pack3(c, gate_lane):
    hi, mid, lo = (jnp.where(gate_lane, part, 0.0) for part in _split3(c))
    return hi + pltpu.roll(mid, 1, 1) + pltpu.roll(lo, 2, 1)


def _inproj_kernel(x_ref, g_ref, wm_ref, wf_ref, fb_ref, cq_ref, sq_ref, ck_ref, sk_ref, c0_ref,
                   rq_ref, rk_ref, rv_ref, rg_ref, fq_ref, fk_ref, fv_ref, cl_ref, carry_ref,
                   *, tl, n_invalid):
    j = pl.program_id(1)

    @pl.when(j == 0)
    def _():
        carry_ref[...] = c0_ref[...]

    x = x_ref[0]
    hn = ((x * _rms(x)) * g_ref[...]).astype(BF16)

    def proj(lo, width):
        return jnp.dot(hn, wm_ref[:, lo:lo + width], preferred_element_type=F32)

    lane = lax.broadcasted_iota(jnp.int32, (1, LANES), 1)
    first_half = (lane & 32) == 0

    gate_lane = functools.reduce(jnp.logical_or, [lane == GATE_STRIDE * h for h in range(FOX_HEADS)])
    ff = jnp.dot(hn, wf_ref[...], preferred_element_type=F32) + fb_ref[...]
    lf = jnp.minimum(ff, 0.0) - jnp.log(1.0 + jnp.exp(-jnp.abs(ff)))
    packed = _pack3(lf, gate_lane).astype(BF16)
    sub = min(tl, CUMSUM_ROWS)
    tri = lax.broadcasted_iota(jnp.int32, (sub, sub), 0) >= lax.broadcasted_iota(jnp.int32, (sub, sub), 1)
    tri = jnp.where(tri, 1.0, 0.0).astype(BF16)
    carry = carry_ref[0:1, :]
    c_blocks = []
    for r in range(tl // sub):
        c3 = jnp.dot(tri, packed[r * sub:(r + 1) * sub], preferred_element_type=F32)
        c_blk = (c3 + pltpu.roll(c3, LANES - 1, 1)) + pltpu.roll(c3, LANES - 2, 1) + carry
        carry = c_blk[sub - 1:sub, :]
        c_blocks.append(c_blk)
    c = jnp.concatenate(c_blocks, axis=0) if len(c_blocks) > 1 else c_blocks[0]
    carry_ref[...] = jnp.broadcast_to(carry, (SUBLANES, LANES))
    cl_ref[0] = carry_ref[...]

    c_triples = _pack3(c * LOG2E, gate_lane)
    neg_triples = -c_triples
    if n_invalid:
        row = j * tl + lax.broadcasted_iota(jnp.int32, (tl, LANES), 0)
        key_mask = jnp.where(row >= n_invalid, 0.0, NEG)
    fox_base = 2 * RET_QK + 2 * RET_V
    fq_all = proj(fox_base, FOX_W) * (FOX_DH ** -0.5 * LOG2E)
    fk_all = proj(fox_base + FOX_W, FOX_W)
    fv_all = proj(fox_base + 2 * FOX_W, FOX_W)
    for pair in range(FOX_HEADS // 2):
        cols = slice(LANES * pair, LANES * (pair + 1))
        fq, fk, fv = fq_all[:, cols], fk_all[:, cols], fv_all[:, cols]
        for hh in range(2):
            h = 2 * pair + hh
            is_data = (lane < FOX_DH) if hh == 0 else (lane >= FOX_DH)
            e = (lane - FOX_DH) if hh == 0 else lane
            e0 = FOX_DH if hh == 0 else 0
            c_lanes = (e >= 0) & (e < 3)
            negc_lanes = (e >= 3) & (e < 6)
            qx = jnp.where(c_lanes, pltpu.roll(c_triples, (e0 - GATE_STRIDE * h) % LANES, 1),
                           jnp.where((e >= 3) & (e < N_EXTRA), 1.0, 0.0))
            kx = jnp.where(negc_lanes, pltpu.roll(neg_triples, (e0 + 3 - GATE_STRIDE * h) % LANES, 1),
                           jnp.where(c_lanes, 1.0, 0.0))
            if n_invalid:
                kx = jnp.where(e == 6, key_mask, kx)
            vx = jnp.where(e == 0, 1.0, 0.0)
            fq_ref[0, h] = jnp.where(is_data, fq, qx).astype(BF16)
            fk_ref[0, h] = jnp.where(is_data, fk, kx).astype(BF16)
            fv_ref[0, h] = jnp.where(is_data, fv, vx).astype(BF16)

    rqk = proj(0, 2 * RET_QK)
    for base, c_ref, s_ref, o_ref in ((0, cq_ref, sq_ref, rq_ref), (RET_QK, ck_ref, sk_ref, rk_ref)):
        for blk in range(RET_QK // LANES):
            xb = rqk[:, base + LANES * blk:base + LANES * (blk + 1)]
            swapped = jnp.where(first_half, pltpu.roll(xb, LANES - 32, 1), pltpu.roll(xb, 32, 1))
            o_ref[0, :, LANES * blk:LANES * (blk + 1)] = (xb * c_ref[...] + swapped * s_ref[...]).astype(BF16)

    rv_ref[0] = proj(2 * RET_QK, RET_V).astype(BF16)
    rg = proj(2 * RET_QK + RET_V, RET_V)
    rg_ref[0] = (rg * jax.nn.sigmoid(rg)).astype(BF16)


def _inproj(x, g, w_main, w_ff, fb, tabs, c0, *, tl, n_invalid):
    nb, rows, _ = x.shape
    grid = (nb, rows // tl)
    row_blk = lambda w: pl.BlockSpec((1, tl, w), lambda b, j: (b, j, 0))
    head_blk = pl.BlockSpec((1, FOX_HEADS, tl, LANES), lambda b, j: (b, 0, j, 0))
    const = lambda shape: pl.BlockSpec(shape, lambda b, j: (0,) * len(shape), pipeline_mode=pl.Buffered(1))
    tab_blk = pl.BlockSpec((tl, LANES), lambda b, j: (j, 0))
    out_shape = (
        jax.ShapeDtypeStruct((nb, rows, RET_QK), BF16), jax.ShapeDtypeStruct((nb, rows, RET_QK), BF16),
        jax.ShapeDtypeStruct((nb, rows, RET_V), BF16), jax.ShapeDtypeStruct((nb, rows, RET_V), BF16),
        jax.ShapeDtypeStruct((nb, FOX_HEADS, rows, LANES), BF16),
        jax.ShapeDtypeStruct((nb, FOX_HEADS, rows, LANES), BF16),
        jax.ShapeDtypeStruct((nb, FOX_HEADS, rows, LANES), BF16),
        jax.ShapeDtypeStruct((nb, SUBLANES, LANES), F32),
    )
    return pl.pallas_call(
        functools.partial(_inproj_kernel, tl=tl, n_invalid=n_invalid),
        grid=grid,
        in_specs=[row_blk(D_MODEL), const((1, D_MODEL)), const((D_MODEL, MAIN_W)), const((D_MODEL, LANES)),
                  const((1, LANES)), tab_blk, tab_blk, tab_blk, tab_blk, const((SUBLANES, LANES))],
        out_specs=(row_blk(RET_QK), row_blk(RET_QK), row_blk(RET_V), row_blk(RET_V),
                   head_blk, head_blk, head_blk, pl.BlockSpec((1, SUBLANES, LANES), lambda b, j: (b, 0, 0))),
        out_shape=out_shape,
        scratch_shapes=[pltpu.VMEM((SUBLANES, LANES), F32)],
        compiler_params=pltpu.CompilerParams(dimension_semantics=("arbitrary", "arbitrary"),
                                             vmem_limit_bytes=VMEM_LIMIT),
        name=f"inproj_{rows}",
    )(x, g, w_main, w_ff, fb, *tabs, c0)


def _retention_kernel(q_ref, k_ref, v_ref, gate_ref, r0_ref, d_ref, wq_ref, wk_ref, gt_ref, ng_ref,
                      *rest, t, n_blocks, n_cast):
    cast_in, (o_ref, rf_ref), cast_out, state_ref = (rest[:n_cast], rest[n_cast:n_cast + 2],
                                                     rest[n_cast + 2:2 * n_cast + 2], rest[-1])
    for src, dst in zip(cast_in, cast_out):
        dst[...] = src[...].astype(BF16)
    state_ref[...] = r0_ref[...]
    lane = lax.broadcasted_iota(jnp.int32, (t, LANES), 1)

    def block(i, carry):
        rows = pl.ds(pl.multiple_of(i * t, t), t)
        q = q_ref[0, rows, :]
        k = k_ref[0, rows, :]
        v = v_ref[0, rows, :]
        qw = (q.astype(F32) * wq_ref[0]).astype(BF16)
        kw = (k.astype(F32) * wk_ref[0]).astype(BF16)
        for hh in range(2):
            in_head = (lane < RET_DK) if hh == 0 else (lane >= RET_DK)
            vh = v[:, RET_DV * hh:RET_DV * (hh + 1)]
            qm = jnp.where(in_head, q, jnp.zeros_like(q))
            s = lax.dot_general(qm, k, (((1,), (1,)), ((), ())), preferred_element_type=F32)
            intra = jnp.dot((s * d_ref[hh]).astype(BF16), vh, preferred_element_type=F32)
            r = state_ref[hh]
            inter = jnp.dot(qw, r.astype(BF16), preferred_element_type=F32)
            kwm = jnp.where(in_head, kw, jnp.zeros_like(kw))
            u = lax.dot_general(kwm, vh, (((0,), (0,)), ((), ())), preferred_element_type=F32)
            state_ref[hh] = gt_ref[hh] * r + u
            o = intra + inter
            cols = slice(RET_DV * hh, RET_DV * (hh + 1))
            o = (o * _rms(o)) * ng_ref[:, cols] * gate_ref[0, rows, cols].astype(F32)
            o_ref[0, rows, cols] = o.astype(BF16)
        return carry

    lax.fori_loop(0, n_blocks, block, 0, unroll=min(n_blocks, RET_UNROLL))
    rf_ref[0] = state_ref[...]


def _retention(rq, rk, rv, gate, r0, tabs, norm_g, to_bf16=(), *, t):
    nb, rows, _ = rq.shape
    d_tab, wq_tab, wk_tab, gt_tab = tabs
    n_pairs = RET_HEADS // 2
    n_steps = nb * n_pairs
    seq = lambda w: pl.BlockSpec((1, rows, w), lambda b, p: (b, 0, p))
    slab = lambda a: pl.BlockSpec((a.shape[0] // n_steps, a.shape[1]), lambda b, p: (b * n_pairs + p, 0))
    assert all(a.shape[0] % (2 * SUBLANES * n_steps) == 0 for a in to_bf16), "bf16 slabs need 16-row tiles"
    return pl.pallas_call(
        functools.partial(_retention_kernel, t=t, n_blocks=rows // t, n_cast=len(to_bf16)),
        grid=(nb, n_pairs),
        in_specs=[seq(LANES), seq(LANES), seq(2 * RET_DV), seq(2 * RET_DV),
                  pl.BlockSpec((2, LANES, RET_DV), lambda b, p: (p, 0, 0)),
                  pl.BlockSpec((2, t, t), lambda b, p: (p, 0, 0)),
                  pl.BlockSpec((1, t, LANES), lambda b, p: (p, 0, 0)),
                  pl.BlockSpec((1, t, LANES), lambda b, p: (p, 0, 0)),
                  pl.BlockSpec((2, 1, RET_DV), lambda b, p: (p, 0, 0)),
                  pl.BlockSpec((1, 2 * RET_DV), lambda b, p: (0, p))] + [slab(a) for a in to_bf16],
        out_specs=(seq(2 * RET_DV), pl.BlockSpec((1, 2, LANES, RET_DV), lambda b, p: (b, p, 0, 0)))
        + tuple(slab(a) for a in to_bf16),
        out_shape=(jax.ShapeDtypeStruct((nb, rows, RET_V), BF16),
                   jax.ShapeDtypeStruct((nb, RET_HEADS, LANES, RET_DV), F32))
        + tuple(jax.ShapeDtypeStruct(a.shape, BF16) for a in to_bf16),
        scratch_shapes=[pltpu.VMEM((2, LANES, RET_DV), F32)],
        compiler_params=pltpu.CompilerParams(dimension_semantics=("arbitrary", "arbitrary"),
                                             vmem_limit_bytes=VMEM_LIMIT),
        name=f"retention_{rows}",
    )(rq, rk, rv, gate, r0, d_tab, wq_tab, wk_tab, gt_tab, norm_g, *to_bf16)


def _fox_kernel(*refs, rows, tk, has_prefix):
    if has_prefix:
        q_ref, k_ref, v_ref, kp_ref, vp_ref, o_ref, m_ref, acc_ref = refs
    else:
        q_ref, k_ref, v_ref, o_ref, m_ref, acc_ref = refs
    lane = lax.broadcasted_iota(jnp.int32, (rows, LANES), 1)
    causal = lax.broadcasted_iota(jnp.int32, (tk, tk), 1) <= lax.broadcasted_iota(jnp.int32, (tk, tk), 0)

    def scores(hh, lo, k):
        return lax.dot_general(q_ref[0, hh, lo:rows, :], k, (((1,), (1,)), ((), ())), preferred_element_type=F32)

    def probs(s, m):
        return jnp.exp2((s - jnp.tile(m, (1, s.shape[1] // LANES))).astype(BF16))

    if has_prefix:
        for hh in range(2):
            s_pre = scores(hh, 0, kp_ref[0, hh])
            m = jnp.broadcast_to(jnp.max(s_pre, axis=1, keepdims=True), (rows, LANES))
            m_ref[hh] = m
            acc_ref[hh] = jnp.dot(probs(s_pre, m), vp_ref[0, hh], preferred_element_type=F32)
    else:
        m_ref[...] = jnp.full(m_ref.shape, NEG, F32)
        acc_ref[...] = jnp.zeros(acc_ref.shape, F32)

    for c in range(rows // tk):
        lo = c * tk
        for hh in range(2):
            s = scores(hh, lo, k_ref[0, hh, lo:lo + tk, :])
            s_top = jnp.where(causal, s[:tk], NEG)
            s = jnp.concatenate([s_top, s[tk:]], axis=0) if lo + tk < rows else s_top
            m_old = m_ref[hh, lo:rows]
            m_new = jnp.maximum(m_old, jnp.max(s, axis=1, keepdims=True))
            m_ref[hh, lo:rows] = m_new
            pv = jnp.dot(probs(s, m_new), v_ref[0, hh, lo:lo + tk, :], preferred_element_type=F32)
            acc_ref[hh, lo:rows] = jnp.exp2(m_old - m_new) * acc_ref[hh, lo:rows] + pv

    outs = []
    for hh in range(2):
        acc = acc_ref[hh]
        sum_lane = FOX_DH if hh == 0 else 0
        outs.append(acc / acc[:, sum_lane:sum_lane + 1])
    o_ref[0] = jnp.where(lane < FOX_DH, outs[0], outs[1]).astype(BF16)


def _fox(fq, fk, fv, kp, vp, *, tk):
    nb, _, rows, _ = fq.shape
    has_prefix = kp is not None
    pair_blk = pl.BlockSpec((1, 2, rows, LANES), lambda b, p: (b, p, 0, 0))
    in_specs = [pair_blk, pair_blk, pair_blk]
    args = [fq, fk, fv]
    if has_prefix:
        pre_blk = pl.BlockSpec((1, 2, PREFIX, LANES), lambda b, p: (0, p, 0, 0))
        in_specs += [pre_blk, pre_blk]
        args += [kp, vp]
    return pl.pallas_call(
        functools.partial(_fox_kernel, rows=rows, tk=tk, has_prefix=has_prefix),
        grid=(nb, FOX_HEADS // 2),
        in_specs=in_specs,
        out_specs=pl.BlockSpec((1, rows, LANES), lambda b, p: (b, 0, p)),
        out_shape=jax.ShapeDtypeStruct((nb, rows, FOX_W), BF16),
        scratch_shapes=[pltpu.VMEM((2, rows, LANES), F32), pltpu.VMEM((2, rows, LANES), F32)],
        compiler_params=pltpu.CompilerParams(dimension_semantics=("arbitrary", "arbitrary"),
                                             vmem_limit_bytes=VMEM_LIMIT),
        name=f"fox_{rows}",
    )(*args)


def _mixer_residual(x_ref, or_ref, of_ref, wo_ref):
    y = jnp.dot(or_ref[0], wo_ref[0:RET_V, :].astype(BF16), preferred_element_type=F32)
    y = y + jnp.dot(of_ref[0], wo_ref[RET_V:RET_V + FOX_W, :].astype(BF16), preferred_element_type=F32)
    return x_ref[0] + y


def _ffn_prefix_kernel(x_ref, or_ref, of_ref, wo_ref, g2_ref, wa_ref, a_ref, *, n_invalid):
    h1 = _mixer_residual(x_ref, or_ref, of_ref, wo_ref)
    hn = ((h1 * _rms(h1)) * g2_ref[...]).astype(BF16)
    a = jnp.dot(hn, wa_ref[...].astype(BF16), preferred_element_type=F32)
    row = lax.broadcasted_iota(jnp.int32, a.shape, 0)
    a_ref[...] = jnp.where(row >= n_invalid, a, 0.0)


def _ffn_prefix(x, o_r, o_f, w_out, g2, w_up):
    full = lambda shape: pl.BlockSpec(shape, lambda i: (0,) * len(shape))
    return pl.pallas_call(
        functools.partial(_ffn_prefix_kernel, n_invalid=N_PAD),
        grid=(1,),
        in_specs=[full((1, PREFIX, D_MODEL)), full((1, PREFIX, RET_V)), full((1, PREFIX, FOX_W)),
                  full((D_MODEL, D_MODEL)), full((1, D_MODEL)), full((D_MODEL, D_FF))],
        out_specs=full((PREFIX, D_FF)),
        out_shape=jax.ShapeDtypeStruct((PREFIX, D_FF), F32),
        compiler_params=pltpu.CompilerParams(vmem_limit_bytes=VMEM_LIMIT),
        name="ffn_prefix",
    )(x, o_r, o_f, w_out, g2, w_up)


def _ffn_kernel(x_ref, or_ref, of_ref, wo_ref, g2_ref, wu_ref, cw_ref, cb_ref, wd_ref, gf_ref, halo_ref,
                o_ref, carry_ref, abuf_ref, gated_ref, *, tm):
    @pl.when(pl.program_id(1) == 0)
    def _():
        carry_ref[...] = halo_ref[...]

    h1 = _mixer_residual(x_ref, or_ref, of_ref, wo_ref)
    hn = ((h1 * _rms(h1)) * g2_ref[...]).astype(BF16)
    for c in range(D_FF // FF_CHUNK):
        cols = slice(c * FF_CHUNK, (c + 1) * FF_CHUNK)
        a = jnp.dot(hn, wu_ref[:, cols], preferred_element_type=F32)
        b = jnp.dot(hn, wu_ref[:, D_FF + c * FF_CHUNK:D_FF + (c + 1) * FF_CHUNK], preferred_element_type=F32)
        abuf_ref[0:SUBLANES, :] = carry_ref[:, cols]
        abuf_ref[SUBLANES:SUBLANES + tm, :] = a
        carry_ref[:, cols] = a[tm - SUBLANES:tm, :]
        acc = cb_ref[:, cols] + abuf_ref[SUBLANES - 2:SUBLANES - 2 + tm, :] * cw_ref[0:1, cols]
        acc = acc + abuf_ref[SUBLANES - 1:SUBLANES - 1 + tm, :] * cw_ref[1:2, cols]
        acc = acc + a * cw_ref[2:3, cols]
        gated_ref[:, cols] = (acc * jax.nn.sigmoid(acc) * b).astype(BF16)
    h2 = h1 + jnp.dot(gated_ref[...], wd_ref[...], preferred_element_type=F32)
    o_ref[0] = (h2 * _rms(h2)) * gf_ref[...]


def _ffn(x, o_r, o_f, w_out, g2, w_up, conv_w, conv_b, w_down, gf, halo, *, tm):
    nb, rows, _ = x.shape
    row_blk = lambda w: pl.BlockSpec((1, tm, w), lambda b, j: (b, j, 0))
    const = lambda shape: pl.BlockSpec(shape, lambda b, j: (0,) * len(shape), pipeline_mode=pl.Buffered(1))
    return pl.pallas_call(
        functools.partial(_ffn_kernel, tm=tm),
        grid=(nb, rows // tm),
        in_specs=[row_blk(D_MODEL), row_blk(RET_V), row_blk(FOX_W), const((D_MODEL, D_MODEL)),
                  const((1, D_MODEL)), const((D_MODEL, 2 * D_FF)), const((CONV_W, D_FF)), const((1, D_FF)),
                  const((D_FF, D_MODEL)), const((1, D_MODEL)), const((SUBLANES, D_FF))],
        out_specs=row_blk(D_MODEL),
        out_shape=jax.ShapeDtypeStruct((nb, rows, D_MODEL), F32),
        scratch_shapes=[pltpu.VMEM((SUBLANES, D_FF), F32), pltpu.VMEM((SUBLANES + tm, FF_CHUNK), F32),
                        pltpu.VMEM((tm, D_FF), BF16)],
        compiler_params=pltpu.CompilerParams(dimension_semantics=("arbitrary", "arbitrary"),
                                             vmem_limit_bytes=VMEM_LIMIT),
        name="ffn",
    )(x, o_r, o_f, w_out, g2, w_up, conv_w, conv_b, w_down, gf, halo)


def _rotary_tables(start, length):
    half = RET_DK // 2
    inv = (1.0 / (ROPE_BASE ** (np.arange(half, dtype=np.float32) / np.float32(half)))).astype(np.float32)
    pos = start + np.arange(length)
    ang = pos.astype(np.float32)[:, None] * inv[None, :]
    cos = np.tile(np.cos(ang), (1, LANES // half))
    sin = np.tile(np.concatenate([-np.sin(ang), np.sin(ang)], axis=1), (1, LANES // RET_DK))
    k_scale = (np.float32(RET_DK ** -0.5) * (pos >= N_PAD).astype(np.float32))[:, None]
    return cos, sin, cos * k_scale, sin * k_scale


def _retention_tables(t):
    f32 = np.float32
    log_g = np.log1p(-np.exp2(f32(-5.0) - np.arange(RET_HEADS, dtype=f32)))
    n = np.arange(t, dtype=f32)
    visible = (np.arange(t)[None, :] // CHUNK) <= (np.arange(t)[:, None] // CHUNK)
    d = np.exp(np.abs(n[:, None] - n[None, :])[None] * log_g[:, None, None]) * visible[None].astype(f32)
    lane_log_g = np.repeat(log_g, RET_DK).reshape(RET_HEADS // 2, 1, LANES)
    wq = np.exp((n + f32(1.0))[None, :, None] * lane_log_g)
    wk = np.exp((f32(t) - f32(1.0) - n)[None, :, None] * lane_log_g)
    gt = np.broadcast_to(np.exp(f32(t) * log_g)[:, None, None], (RET_HEADS, 1, RET_DV))
    return tuple(np.ascontiguousarray(a, dtype=f32) for a in (d, wq, wk, gt))


def _gate_lanes(v):
    spread = jnp.stack([v] + [jnp.zeros_like(v)] * (GATE_STRIDE - 1), axis=-1)
    spread = spread.reshape(v.shape[:-1] + (GATE_STRIDE * FOX_HEADS,))
    return jnp.pad(spread, [(0, 0)] * (v.ndim - 1) + [(0, LANES - GATE_STRIDE * FOX_HEADS)])


def kernel(x, meta_tokens, attn_norm_g, w_in, fox_forget_b, ret_norm_g, w_out, ffn_norm_g, w_up, conv_w,
           conv_b, w_down, final_norm_g):
    assert w_in.shape[0] == 1, "single-layer block"
    batch, seq, _ = x.shape
    w_main = w_in[0].astype(BF16)
    w_ff = _gate_lanes(w_in[0, :, MAIN_W:]).astype(BF16)
    fb = _gate_lanes(fox_forget_b[0]).reshape(1, LANES)
    g1 = attn_norm_g[0].reshape(1, D_MODEL)
    g2 = ffn_norm_g[0].reshape(1, D_MODEL)
    gf = final_norm_g.reshape(1, D_MODEL)
    ng = ret_norm_g[0].reshape(1, RET_V)
    cb = conv_b[0].reshape(1, D_FF)
    h_pre = jnp.concatenate([jnp.zeros((N_PAD, D_MODEL), x.dtype), meta_tokens.astype(x.dtype)], axis=0)[None]

    zeros_c = jnp.zeros((SUBLANES, LANES), F32)
    rq0, rk0, rv0, rg0, fq0, fk0, fv0, c_pre = _inproj(h_pre, g1, w_main, w_ff, fb, _rotary_tables(0, PREFIX),
                                                        zeros_c, tl=PREFIX, n_invalid=N_PAD)
    r_zero = jnp.zeros((RET_HEADS, LANES, RET_DV), F32)
    or0, r_pre = _retention(rq0, rk0, rv0, rg0, r_zero, _retention_tables(PREFIX), ng, t=PREFIX)
    of0 = _fox(fq0, fk0, fv0, None, None, tk=PREFIX)
    a_pre = _ffn_prefix(h_pre, or0, of0, w_out[0], g2, w_up[0])
    halo = a_pre[PREFIX - SUBLANES:]

    rq, rk, rv, rg, fq, fk, fv, _ = _inproj(x, g1, w_main, w_ff, fb, _rotary_tables(PREFIX, seq), c_pre[0],
                                            tl=1024, n_invalid=0)
    o_r, _, w_out_b, w_up_b, w_down_b = _retention(rq, rk, rv, rg, r_pre[0], _retention_tables(256), ng,
                                                   (w_out[0], w_up[0], w_down[0]), t=256)
    o_f = _fox(fq, fk, fv, fk0, fv0, tk=512)
    return _ffn(x, o_r, o_f, w_out_b, g2, w_up_b, conv_w[0], cb, w_down_b, gf, halo, tm=1024)
```

```python
import functools

import jax
import jax.numpy as jnp
import numpy as np
from jax import lax
from jax.experimental import pallas as pl
from jax.experimental.pallas import tpu as pltpu

F32 = jnp.float32
BF16 = jnp.bfloat16

D_MODEL = 1024
CHUNK = 64
N_META = 16
PREFIX = 128
N_PAD = PREFIX - N_META
RET_HEADS = 4
RET_DK = 64
RET_DV = 128
FOX_HEADS = 8
FOX_DH = 64
D_FF = 2816
CONV_W = 3
ROPE_BASE = 10000.0
EPS = 1e-6
NEG = -1e30

RET_QK = RET_HEADS * RET_DK
RET_V = RET_HEADS * RET_DV
FOX_W = FOX_HEADS * FOX_DH
MAIN_W = 2 * RET_QK + 2 * RET_V + 3 * FOX_W
LANES = 128
SUBLANES = 8
N_EXTRA = 7
GATE_STRIDE = 3
LOG2E = 1.4426950408889634
CUMSUM_ROWS = 256
VMEM_LIMIT = 56 * 1024 * 1024
FF_CHUNK = 256
RET_UNROLL = 8


def _rms(x):
    return lax.rsqrt(jnp.mean(x * x, axis=-1, keepdims=True) + EPS)


def _split3(c):
    hi = c.astype(BF16).astype(F32)
    r = c - hi
    mid = r.astype(BF16).astype(F32)
    return hi, mid, r - mid


def _pack3(c, gate_lane):
    hi, mid, lo = (jnp.where(gate_lane, part, 0.0) for part in _split3(c))
    return hi + pltpu.roll(mid, 1, 1) + pltpu.roll(lo, 2, 1)


def _inproj_kernel(x_ref, g_ref, wm_ref, wf_ref, fb_ref, cq_ref, sq_ref, ck_ref, sk_ref, c0_ref,
                   rq_ref, rk_ref, rv_ref, rg_ref, fq_ref, fk_ref, fv_ref, cl_ref, carry_ref,
                   *, tl, n_invalid):
    j = pl.program_id(1)

    @pl.when(j == 0)
    def _():
        carry_ref[...] = c0_ref[...]

    x = x_ref[0]
    hn = ((x * _rms(x)) * g_ref[...]).astype(BF16)

    def proj(lo, width):
        return jnp.dot(hn, wm_ref[:, lo:lo + width], preferred_element_type=F32)

    lane = lax.broadcasted_iota(jnp.int32, (1, LANES), 1)
    first_half = (lane & 32) == 0

    gate_lane = functools.reduce(jnp.logical_or, [lane == GATE_STRIDE * h for h in range(FOX_HEADS)])
    ff = jnp.dot(hn, wf_ref[...], preferred_element_type=F32) + fb_ref[...]
    lf = jnp.minimum(ff, 0.0) - jnp.log(1.0 + jnp.exp(-jnp.abs(ff)))
    packed = _pack3(lf, gate_lane).astype(BF16)
    sub = min(tl, CUMSUM_ROWS)
    tri = lax.broadcasted_iota(jnp.int32, (sub, sub), 0) >= lax.broadcasted_iota(jnp.int32, (sub, sub), 1)
    tri = jnp.where(tri, 1.0, 0.0).astype(BF16)
    carry = carry_ref[0:1, :]
    c_blocks = []
    for r in range(tl // sub):
        c3 = jnp.dot(tri, packed[r * sub:(r + 1) * sub], preferred_element_type=F32)
        c_blk = (c3 + pltpu.roll(c3, LANES - 1, 1)) + pltpu.roll(c3, LANES - 2, 1) + carry
        carry = c_blk[sub - 1:sub, :]
        c_blocks.append(c_blk)
    c = jnp.concatenate(c_blocks, axis=0) if len(c_blocks) > 1 else c_blocks[0]
    carry_ref[...] = jnp.broadcast_to(carry, (SUBLANES, LANES))
    cl_ref[0] = carry_ref[...]

    c_triples = _pack3(c * LOG2E, gate_lane)
    neg_triples = -c_triples
    if n_invalid:
        row = j * tl + lax.broadcasted_iota(jnp.int32, (tl, LANES), 0)
        key_mask = jnp.where(row >= n_invalid, 0.0, NEG)
    fox_base = 2 * RET_QK + 2 * RET_V
    fq_all = proj(fox_base, FOX_W) * (FOX_DH ** -0.5 * LOG2E)
    fk_all = proj(fox_base + FOX_W, FOX_W)
    fv_all = proj(fox_base + 2 * FOX_W, FOX_W)
    for pair in range(FOX_HEADS // 2):
        cols = slice(LANES * pair, LANES * (pair + 1))
        fq, fk, fv = fq_all[:, cols], fk_all[:, cols], fv_all[:, cols]
        for hh in range(2):
            h = 2 * pair + hh
            is_data = (lane < FOX_DH) if hh == 0 else (lane >= FOX_DH)
            e = (lane - FOX_DH) if hh == 0 else lane
            e0 = FOX_DH if hh == 0 else 0
            c_lanes = (e >= 0) & (e < 3)
            negc_lanes = (e >= 3) & (e < 6)
            qx = jnp.where(c_lanes, pltpu.roll(c_triples, (e0 - GATE_STRIDE * h) % LANES, 1),
                           jnp.where((e >= 3) & (e < N_EXTRA), 1.0, 0.0))
            kx = jnp.where(negc_lanes, pltpu.roll(neg_triples, (e0 + 3 - GATE_STRIDE * h) % LANES, 1),
                           jnp.where(c_lanes, 1.0, 0.0))
            if n_invalid:
                kx = jnp.where(e == 6, key_mask, kx)
            vx = jnp.where(e == 0, 1.0, 0.0)
            fq_ref[0, h] = jnp.where(is_data, fq, qx).astype(BF16)
            fk_ref[0, h] = jnp.where(is_data, fk, kx).astype(BF16)
            fv_ref[0, h] = jnp.where(is_data, fv, vx).astype(BF16)

    rqk = proj(0, 2 * RET_QK)
    for base, c_ref, s_ref, o_ref in ((0, cq_ref, sq_ref, rq_ref), (RET_QK, ck_ref, sk_ref, rk_ref)):
        for blk in range(RET_QK // LANES):
            xb = rqk[:, base + LANES * blk:base + LANES * (blk + 1)]
            swapped = jnp.where(first_half, pltpu.roll(xb, LANES - 32, 1), pltpu.roll(xb, 32, 1))
            o_ref[0, :, LANES * blk:LANES * (blk + 1)] = (xb * c_ref[...] + swapped * s_ref[...]).astype(BF16)

    rv_ref[0] = proj(2 * RET_QK, RET_V).astype(BF16)
    rg = proj(2 * RET_QK + RET_V, RET_V)
    rg_ref[0] = (rg * jax.nn.sigmoid(rg)).astype(BF16)


def _inproj(x, g, w_main, w_ff, fb, tabs, c0, *, tl, n_invalid):
    nb, rows, _ = x.shape
    grid = (nb, rows // tl)
    row_blk = lambda w: pl.BlockSpec((1, tl, w), lambda b, j: (b, j, 0))
    head_blk = pl.BlockSpec((1, FOX_HEADS, tl, LANES), lambda b, j: (b, 0, j, 0))
    const = lambda shape: pl.BlockSpec(shape, lambda b, j: (0,) * len(shape), pipeline_mode=pl.Buffered(1))
    tab_blk = pl.BlockSpec((tl, LANES), lambda b, j: (j, 0))
    out_shape = (
        jax.ShapeDtypeStruct((nb, rows, RET_QK), BF16), jax.ShapeDtypeStruct((nb, rows, RET_QK), BF16),
        jax.ShapeDtypeStruct((nb, rows, RET_V), BF16), jax.ShapeDtypeStruct((nb, rows, RET_V), BF16),
        jax.ShapeDtypeStruct((nb, FOX_HEADS, rows, LANES), BF16),
        jax.ShapeDtypeStruct((nb, FOX_HEADS, rows, LANES), BF16),
        jax.ShapeDtypeStruct((nb, FOX_HEADS, rows, LANES), BF16),
        jax.ShapeDtypeStruct((nb, SUBLANES, LANES), F32),
    )
    return pl.pallas_call(
        functools.partial(_inproj_kernel, tl=tl, n_invalid=n_invalid),
        grid=grid,
        in_specs=[row_blk(D_MODEL), const((1, D_MODEL)), const((D_MODEL, MAIN_W)), const((D_MODEL, LANES)),
                  const((1, LANES)), tab_blk, tab_blk, tab_blk, tab_blk, const((SUBLANES, LANES))],
        out_specs=(row_blk(RET_QK), row_blk(RET_QK), row_blk(RET_V), row_blk(RET_V),
                   head_blk, head_blk, head_blk, pl.BlockSpec((1, SUBLANES, LANES), lambda b, j: (b, 0, 0))),
        out_shape=out_shape,
        scratch_shapes=[pltpu.VMEM((SUBLANES, LANES), F32)],
        compiler_params=pltpu.CompilerParams(dimension_semantics=("arbitrary", "arbitrary"),
                                             vmem_limit_bytes=VMEM_LIMIT),
        name=f"inproj_{rows}",
    )(x, g, w_main, w_ff, fb, *tabs, c0)


def _retention_kernel(q_ref, k_ref, v_ref, gate_ref, r0_ref, d_ref, wq_ref, wk_ref, gt_ref, ng_ref,
                      o_ref, rf_ref, state_ref, *, t, n_blocks):
    state_ref[...] = r0_ref[...]
    lane = lax.broadcasted_iota(jnp.int32, (t, LANES), 1)

    def block(i, carry):
        rows = pl.ds(pl.multiple_of(i * t, t), t)
        q = q_ref[0, rows, :]
        k = k_ref[0, rows, :]
        v = v_ref[0, rows, :]
        qw = (q.astype(F32) * wq_ref[0]).astype(BF16)
        kw = (k.astype(F32) * wk_ref[0]).astype(BF16)
        for hh in range(2):
            in_head = (lane < RET_DK) if hh == 0 else (lane >= RET_DK)
            vh = v[:, RET_DV * hh:RET_DV * (hh + 1)]
            qm = jnp.where(in_head, q, jnp.zeros_like(q))
            s = lax.dot_general(qm, k, (((1,), (1,)), ((), ())), preferred_element_type=F32)
            intra = jnp.dot((s * d_ref[hh]).astype(BF16), vh, preferred_element_type=F32)
            r = state_ref[hh]
            inter = jnp.dot(qw, r.astype(BF16), preferred_element_type=F32)
            kwm = jnp.where(in_head, kw, jnp.zeros_like(kw))
            u = lax.dot_general(kwm, vh, (((0,), (0,)), ((), ())), preferred_element_type=F32)
            state_ref[hh] = gt_ref[hh] * r + u
            o = intra + inter
            cols = slice(RET_DV * hh, RET_DV * (hh + 1))
            o = (o * _rms(o)) * ng_ref[:, cols] * gate_ref[0, rows, cols].astype(F32)
            o_ref[0, rows, cols] = o.astype(BF16)
        return carry

    lax.fori_loop(0, n_blocks, block, 0, unroll=min(n_blocks, RET_UNROLL))
    rf_ref[0] = state_ref[...]


def _retention(rq, rk, rv, gate, r0, tabs, norm_g, *, t):
    nb, rows, _ = rq.shape
    d_tab, wq_tab, wk_tab, gt_tab = tabs
    n_pairs = RET_HEADS // 2
    seq = lambda w: pl.BlockSpec((1, rows, w), lambda b, p: (b, 0, p))
    return pl.pallas_call(
        functools.partial(_retention_kernel, t=t, n_blocks=rows // t),
        grid=(nb, n_pairs),
        in_specs=[seq(LANES), seq(LANES), seq(2 * RET_DV), seq(2 * RET_DV),
                  pl.BlockSpec((2, LANES, RET_DV), lambda b, p: (p, 0, 0)),
                  pl.BlockSpec((2, t, t), lambda b, p: (p, 0, 0)),
                  pl.BlockSpec((1, t, LANES), lambda b, p: (p, 0, 0)),
                  pl.BlockSpec((1, t, LANES), lambda b, p: (p, 0, 0)),
                  pl.BlockSpec((2, 1, RET_DV), lambda b, p: (p, 0, 0)),
                  pl.BlockSpec((1, 2 * RET_DV), lambda b, p: (0, p))],
        out_specs=(seq(2 * RET_DV), pl.BlockSpec((1, 2, LANES, RET_DV), lambda b, p: (b, p, 0, 0))),
        out_shape=(jax.ShapeDtypeStruct((nb, rows, RET_V), BF16),
                   jax.ShapeDtypeStruct((nb, RET_HEADS, LANES, RET_DV), F32)),
        scratch_shapes=[pltpu.VMEM((2, LANES, RET_DV), F32)],
        compiler_params=pltpu.CompilerParams(dimension_semantics=("arbitrary", "arbitrary"),
                                             vmem_limit_bytes=VMEM_LIMIT),
        name=f"retention_{rows}",
    )(rq, rk, rv, gate, r0, d_tab, wq_tab, wk_tab, gt_tab, norm_g)


def _fox_kernel(*refs, rows, tk, has_prefix, n_cast):
    n_in = 5 if has_prefix else 3
    cast_in, cast_out = refs[n_in:n_in + n_cast], refs[n_in + n_cast + 1:n_in + 2 * n_cast + 1]
    for src, dst in zip(cast_in, cast_out):
        dst[...] = src[...].astype(BF16)
    o_ref, (m_ref, acc_ref) = refs[n_in + n_cast], refs[-2:]
    if has_prefix:
        q_ref, k_ref, v_ref, kp_ref, vp_ref = refs[:n_in]
    else:
        q_ref, k_ref, v_ref = refs[:n_in]
    lane = lax.broadcasted_iota(jnp.int32, (rows, LANES), 1)
    causal = lax.broadcasted_iota(jnp.int32, (tk, tk), 1) <= lax.broadcasted_iota(jnp.int32, (tk, tk), 0)

    def scores(hh, lo, k):
        return lax.dot_general(q_ref[0, hh, lo:rows, :], k, (((1,), (1,)), ((), ())), preferred_element_type=F32)

    def probs(s, m):
        return jnp.exp2((s - jnp.tile(m, (1, s.shape[1] // LANES))).astype(BF16))

    if has_prefix:
        for hh in range(2):
            s_pre = scores(hh, 0, kp_ref[0, hh])
            m = jnp.broadcast_to(jnp.max(s_pre, axis=1, keepdims=True), (rows, LANES))
            m_ref[hh] = m
            acc_ref[hh] = jnp.dot(probs(s_pre, m), vp_ref[0, hh], preferred_element_type=F32)
    else:
        m_ref[...] = jnp.full(m_ref.shape, NEG, F32)
        acc_ref[...] = jnp.zeros(acc_ref.shape, F32)

    for c in range(rows // tk):
        lo = c * tk
        for hh in range(2):
            s = scores(hh, lo, k_ref[0, hh, lo:lo + tk, :])
            s_top = jnp.where(causal, s[:tk], NEG)
            s = jnp.concatenate([s_top, s[tk:]], axis=0) if lo + tk < rows else s_top
            m_old = m_ref[hh, lo:rows]
            m_new = jnp.maximum(m_old, jnp.max(s, axis=1, keepdims=True))
            m_ref[hh, lo:rows] = m_new
            pv = jnp.dot(probs(s, m_new), v_ref[0, hh, lo:lo + tk, :], preferred_element_type=F32)
            acc_ref[hh, lo:rows] = jnp.exp2(m_old - m_new) * acc_ref[hh, lo:rows] + pv

    outs = []
    for hh in range(2):
        acc = acc_ref[hh]
        sum_lane = FOX_DH if hh == 0 else 0
        outs.append(acc / acc[:, sum_lane:sum_lane + 1])
    o_ref[0] = jnp.where(lane < FOX_DH, outs[0], outs[1]).astype(BF16)


def _fox(fq, fk, fv, kp, vp, to_bf16=(), *, tk):
    nb, _, rows, _ = fq.shape
    has_prefix = kp is not None
    n_pairs = FOX_HEADS // 2
    n_steps = nb * n_pairs
    pair_blk = pl.BlockSpec((1, 2, rows, LANES), lambda b, p: (b, p, 0, 0))
    in_specs = [pair_blk, pair_blk, pair_blk]
    args = [fq, fk, fv]
    if has_prefix:
        pre_blk = pl.BlockSpec((1, 2, PREFIX, LANES), lambda b, p: (0, p, 0, 0))
        in_specs += [pre_blk, pre_blk]
        args += [kp, vp]

    def slab(a):
        n = n_steps
        while a.shape[0] % (n * 2 * SUBLANES):
            n //= 2
        return pl.BlockSpec((a.shape[0] // n, a.shape[1]), lambda b, p: ((b * n_pairs + p) // (n_steps // n), 0))

    cast_specs = [slab(a) for a in to_bf16]
    outs = pl.pallas_call(
        functools.partial(_fox_kernel, rows=rows, tk=tk, has_prefix=has_prefix, n_cast=len(to_bf16)),
        grid=(nb, n_pairs),
        in_specs=in_specs + cast_specs,
        out_specs=[pl.BlockSpec((1, rows, LANES), lambda b, p: (b, 0, p))] + cast_specs,
        out_shape=[jax.ShapeDtypeStruct((nb, rows, FOX_W), BF16)]
        + [jax.ShapeDtypeStruct(a.shape, BF16) for a in to_bf16],
        scratch_shapes=[pltpu.VMEM((2, rows, LANES), F32), pltpu.VMEM((2, rows, LANES), F32)],
        compiler_params=pltpu.CompilerParams(dimension_semantics=("arbitrary", "arbitrary"),
                                             vmem_limit_bytes=VMEM_LIMIT),
        name=f"fox_{rows}",
    )(*args, *to_bf16)
    return outs if to_bf16 else outs[0]


def _mixer_residual(x_ref, or_ref, of_ref, wo_ref):
    y = jnp.dot(or_ref[0], wo_ref[0:RET_V, :], preferred_element_type=F32)
    y = y + jnp.dot(of_ref[0], wo_ref[RET_V:RET_V + FOX_W, :], preferred_element_type=F32)
    return x_ref[0] + y


def _ffn_prefix_kernel(x_ref, or_ref, of_ref, wo_ref, g2_ref, wa_ref, a_ref, *, n_invalid):
    h1 = _mixer_residual(x_ref, or_ref, of_ref, wo_ref)
    hn = ((h1 * _rms(h1)) * g2_ref[...]).astype(BF16)
    a = jnp.dot(hn, wa_ref[...], preferred_element_type=F32)
    row = lax.broadcasted_iota(jnp.int32, a.shape, 0)
    a_ref[...] = jnp.where(row >= n_invalid, a, 0.0)


def _ffn_prefix(x, o_r, o_f, w_out, g2, w_up):
    full = lambda shape: pl.BlockSpec(shape, lambda i: (0,) * len(shape))
    return pl.pallas_call(
        functools.partial(_ffn_prefix_kernel, n_invalid=N_PAD),
        grid=(1,),
        in_specs=[full((1, PREFIX, D_MODEL)), full((1, PREFIX, RET_V)), full((1, PREFIX, FOX_W)),
                  full((D_MODEL, D_MODEL)), full((1, D_MODEL)), full((D_MODEL, D_FF))],
        out_specs=full((PREFIX, D_FF)),
        out_shape=jax.ShapeDtypeStruct((PREFIX, D_FF), F32),
        compiler_params=pltpu.CompilerParams(vmem_limit_bytes=VMEM_LIMIT),
        name="ffn_prefix",
    )(x, o_r, o_f, w_out, g2, w_up)


def _ffn_kernel(x_ref, or_ref, of_ref, wo_ref, g2_ref, wu_ref, cw_ref, cb_ref, wd_ref, gf_ref, halo_ref,
                o_ref, carry_ref, abuf_ref, gated_ref, *, tm):
    @pl.when(pl.program_id(1) == 0)
    def _():
        carry_ref[...] = halo_ref[...]

    h1 = _mixer_residual(x_ref, or_ref, of_ref, wo_ref)
    hn = ((h1 * _rms(h1)) * g2_ref[...]).astype(BF16)
    for c in range(D_FF // FF_CHUNK):
        cols = slice(c * FF_CHUNK, (c + 1) * FF_CHUNK)
        a = jnp.dot(hn, wu_ref[:, cols], preferred_element_type=F32)
        b = jnp.dot(hn, wu_ref[:, D_FF + c * FF_CHUNK:D_FF + (c + 1) * FF_CHUNK], preferred_element_type=F32)
        abuf_ref[0:SUBLANES, :] = carry_ref[:, cols]
        abuf_ref[SUBLANES:SUBLANES + tm, :] = a
        carry_ref[:, cols] = a[tm - SUBLANES:tm, :]
        acc = cb_ref[:, cols] + abuf_ref[SUBLANES - 2:SUBLANES - 2 + tm, :] * cw_ref[0:1, cols]
        acc = acc + abuf_ref[SUBLANES - 1:SUBLANES - 1 + tm, :] * cw_ref[1:2, cols]
        acc = acc + a * cw_ref[2:3, cols]
        gated_ref[:, cols] = (acc * jax.nn.sigmoid(acc) * b).astype(BF16)
    h2 = h1 + jnp.dot(gated_ref[...], wd_ref[...], preferred_element_type=F32)
    o_ref[0] = (h2 * _rms(h2)) * gf_ref[...]


def _ffn(x, o_r, o_f, w_out, g2, w_up, conv_w, conv_b, w_down, gf, halo, *, tm):
    nb, rows, _ = x.shape
    row_blk = lambda w: pl.BlockSpec((1, tm, w), lambda b, j: (b, j, 0))
    const = lambda shape: pl.BlockSpec(shape, lambda b, j: (0,) * len(shape), pipeline_mode=pl.Buffered(1))
    return pl.pallas_call(
        functools.partial(_ffn_kernel, tm=tm),
        grid=(nb, rows // tm),
        in_specs=[row_blk(D_MODEL), row_blk(RET_V), row_blk(FOX_W), const((D_MODEL, D_MODEL)),
                  const((1, D_MODEL)), const((D_MODEL, 2 * D_FF)), const((CONV_W, D_FF)), const((1, D_FF)),
                  const((D_FF, D_MODEL)), const((1, D_MODEL)), const((SUBLANES, D_FF))],
        out_specs=row_blk(D_MODEL),
        out_shape=jax.ShapeDtypeStruct((nb, rows, D_MODEL), F32),
        scratch_shapes=[pltpu.VMEM((SUBLANES, D_FF), F32), pltpu.VMEM((SUBLANES + tm, FF_CHUNK), F32),
                        pltpu.VMEM((tm, D_FF), BF16)],
        compiler_params=pltpu.CompilerParams(dimension_semantics=("arbitrary", "arbitrary"),
                                             vmem_limit_bytes=VMEM_LIMIT),
        name="ffn",
    )(x, o_r, o_f, w_out, g2, w_up, conv_w, conv_b, w_down, gf, halo)


def _rotary_tables(start, length):
    half = RET_DK // 2
    inv = (1.0 / (ROPE_BASE ** (np.arange(half, dtype=np.float32) / np.float32(half)))).astype(np.float32)
    pos = start + np.arange(length)
    ang = pos.astype(np.float32)[:, None] * inv[None, :]
    cos = np.tile(np.cos(ang), (1, LANES // half))
    sin = np.tile(np.concatenate([-np.sin(ang), np.sin(ang)], axis=1), (1, LANES // RET_DK))
    k_scale = (np.float32(RET_DK ** -0.5) * (pos >= N_PAD).astype(np.float32))[:, None]
    return cos, sin, cos * k_scale, sin * k_scale


def _retention_tables(t):
    f32 = np.float32
    log_g = np.log1p(-np.exp2(f32(-5.0) - np.arange(RET_HEADS, dtype=f32)))
    n = np.arange(t, dtype=f32)
    visible = (np.arange(t)[None, :] // CHUNK) <= (np.arange(t)[:, None] // CHUNK)
    d = np.exp(np.abs(n[:, None] - n[None, :])[None] * log_g[:, None, None]) * visible[None].astype(f32)
    lane_log_g = np.repeat(log_g, RET_DK).reshape(RET_HEADS // 2, 1, LANES)
    wq = np.exp((n + f32(1.0))[None, :, None] * lane_log_g)
    wk = np.exp((f32(t) - f32(1.0) - n)[None, :, None] * lane_log_g)
    gt = np.broadcast_to(np.exp(f32(t) * log_g)[:, None, None], (RET_HEADS, 1, RET_DV))
    return tuple(np.ascontiguousarray(a, dtype=f32) for a in (d, wq, wk, gt))


def _gate_lanes(v):
    spread = jnp.stack([v] + [jnp.zeros_like(v)] * (GATE_STRIDE - 1), axis=-1)
    spread = spread.reshape(v.shape[:-1] + (GATE_STRIDE * FOX_HEADS,))
    return jnp.pad(spread, [(0, 0)] * (v.ndim - 1) + [(0, LANES - GATE_STRIDE * FOX_HEADS)])


def kernel(x, meta_tokens, attn_norm_g, w_in, fox_forget_b, ret_norm_g, w_out, ffn_norm_g, w_up, conv_w,
           conv_b, w_down, final_norm_g):
    assert w_in.shape[0] == 1, "single-layer block"
    batch, seq, _ = x.shape
    w_main = w_in[0].astype(BF16)
    w_ff = _gate_lanes(w_in[0, :, MAIN_W:]).astype(BF16)
    fb = _gate_lanes(fox_forget_b[0]).reshape(1, LANES)
    g1 = attn_norm_g[0].reshape(1, D_MODEL)
    g2 = ffn_norm_g[0].reshape(1, D_MODEL)
    gf = final_norm_g.reshape(1, D_MODEL)
    ng = ret_norm_g[0].reshape(1, RET_V)
    cb = conv_b[0].reshape(1, D_FF)
    h_pre = jnp.concatenate([jnp.zeros((N_PAD, D_MODEL), x.dtype), meta_tokens.astype(x.dtype)], axis=0)[None]

    zeros_c = jnp.zeros((SUBLANES, LANES), F32)
    rq0, rk0, rv0, rg0, fq0, fk0, fv0, c_pre = _inproj(h_pre, g1, w_main, w_ff, fb, _rotary_tables(0, PREFIX),
                                                        zeros_c, tl=PREFIX, n_invalid=N_PAD)
    r_zero = jnp.zeros((RET_HEADS, LANES, RET_DV), F32)
    or0, r_pre = _retention(rq0, rk0, rv0, rg0, r_zero, _retention_tables(PREFIX), ng, t=PREFIX)
    of0 = _fox(fq0, fk0, fv0, None, None, tk=PREFIX)

    rq, rk, rv, rg, fq, fk, fv, _ = _inproj(x, g1, w_main, w_ff, fb, _rotary_tables(PREFIX, seq), c_pre[0],
                                            tl=1024, n_invalid=0)
    o_r, _ = _retention(rq, rk, rv, rg, r_pre[0], _retention_tables(256), ng, t=256)
    o_f, w_out_b, w_up_b, w_down_b = _fox(fq, fk, fv, fk0, fv0, (w_out[0], w_up[0], w_down[0]), tk=512)
    a_pre = _ffn_prefix(h_pre, or0, of0, w_out_b, g2, w_up_b)
    halo = a_pre[PREFIX - SUBLANES:]
    return _ffn(x, o_r, o_f, w_out_b, g2, w_up_b, conv_w[0], cb, w_down_b, gf, halo, tm=1024)
```

```python
import functools

import jax
import jax.numpy as jnp
import numpy as np
from jax import lax
from jax.experimental import pallas as pl
from jax.experimental.pallas import tpu as pltpu

F32 = jnp.float32
BF16 = jnp.bfloat16

D_MODEL = 1024
CHUNK = 64
N_META = 16
PREFIX = 128
N_PAD = PREFIX - N_META
RET_HEADS = 4
RET_DK = 64
RET_DV = 128
FOX_HEADS = 8
FOX_DH = 64
D_FF = 2816
CONV_W = 3
ROPE_BASE = 10000.0
EPS = 1e-6
NEG = -1e30

RET_QK = RET_HEADS * RET_DK
RET_V = RET_HEADS * RET_DV
FOX_W = FOX_HEADS * FOX_DH
MAIN_W = 2 * RET_QK + 2 * RET_V + 3 * FOX_W
LANES = 128
SUBLANES = 8
N_EXTRA = 7
GATE_STRIDE = 3
LOG2E = 1.4426950408889634
CUMSUM_ROWS = 256
VMEM_LIMIT = 56 * 1024 * 1024
FF_CHUNK = 256
RET_UNROLL = 4


def _rms(x):
    return lax.rsqrt(jnp.mean(x * x, axis=-1, keepdims=True) + EPS)


def _split3(c):
    hi = c.astype(BF16).astype(F32)
    r = c - hi
    mid = r.astype(BF16).astype(F32)
    return hi, mid, r - mid


def _pack3(c, gate_lane):
    hi, mid, lo = (jnp.where(gate_lane, part, 0.0) for part in _split3(c))
    return hi + pltpu.roll(mid, 1, 1) + pltpu.roll(lo, 2, 1)


def _inproj_kernel(x_ref, g_ref, wm_ref, wf_ref, fb_ref, cq_ref, sq_ref, ck_ref, sk_ref, c0_ref,
                   rq_ref, rk_ref, rv_ref, rg_ref, fq_ref, fk_ref, fv_ref, cl_ref, carry_ref,
                   *, tl, n_invalid):
    j = pl.program_id(1)

    @pl.when(j == 0)
    def _():
        carry_ref[...] = c0_ref[...]

    x = x_ref[0]
    hn = ((x * _rms(x)) * g_ref[...]).astype(BF16)

    def proj(lo, width):
        return jnp.dot(hn, wm_ref[:, lo:lo + width], preferred_element_type=F32)

    lane = lax.broadcasted_iota(jnp.int32, (1, LANES), 1)
    first_half = (lane & 32) == 0

    gate_lane = functools.reduce(jnp.logical_or, [lane == GATE_STRIDE * h for h in range(FOX_HEADS)])
    ff = jnp.dot(hn, wf_ref[...], preferred_element_type=F32) + fb_ref[...]
    lf = jnp.minimum(ff, 0.0) - jnp.log(1.0 + jnp.exp(-jnp.abs(ff)))
    packed = _pack3(lf, gate_lane).astype(BF16)
    sub = min(tl, CUMSUM_ROWS)
    tri = lax.broadcasted_iota(jnp.int32, (sub, sub), 0) >= lax.broadcasted_iota(jnp.int32, (sub, sub), 1)
    tri = jnp.where(tri, 1.0, 0.0).astype(BF16)
    carry = carry_ref[0:1, :]
    c_blocks = []
    for r in range(tl // sub):
        c3 = jnp.dot(tri, packed[r * sub:(r + 1) * sub], preferred_element_type=F32)
        c_blk = (c3 + pltpu.roll(c3, LANES - 1, 1)) + pltpu.roll(c3, LANES - 2, 1) + carry
        carry = c_blk[sub - 1:sub, :]
        c_blocks.append(c_blk)
    c = jnp.concatenate(c_blocks, axis=0) if len(c_blocks) > 1 else c_blocks[0]
    carry_ref[...] = jnp.broadcast_to(carry, (SUBLANES, LANES))
    cl_ref[0] = carry_ref[...]

    c_triples = _pack3(c * LOG2E, gate_lane)
    neg_triples = -c_triples
    if n_invalid:
        row = j * tl + lax.broadcasted_iota(jnp.int32, (tl, LANES), 0)
        key_mask = jnp.where(row >= n_invalid, 0.0, NEG)
    fox_base = 2 * RET_QK + 2 * RET_V
    fq_all = proj(fox_base, FOX_W) * (FOX_DH ** -0.5 * LOG2E)
    fk_all = proj(fox_base + FOX_W, FOX_W)
    fv_all = proj(fox_base + 2 * FOX_W, FOX_W)
    for pair in range(FOX_HEADS // 2):
        cols = slice(LANES * pair, LANES * (pair + 1))
        fq, fk, fv = fq_all[:, cols], fk_all[:, cols], fv_all[:, cols]
        for hh in range(2):
            h = 2 * pair + hh
            is_data = (lane < FOX_DH) if hh == 0 else (lane >= FOX_DH)
            e = (lane - FOX_DH) if hh == 0 else lane
            e0 = FOX_DH if hh == 0 else 0
            c_lanes = (e >= 0) & (e < 3)
            negc_lanes = (e >= 3) & (e < 6)
            qx = jnp.where(c_lanes, pltpu.roll(c_triples, (e0 - GATE_STRIDE * h) % LANES, 1),
                           jnp.where((e >= 3) & (e < N_EXTRA), 1.0, 0.0))
            kx = jnp.where(negc_lanes, pltpu.roll(neg_triples, (e0 + 3 - GATE_STRIDE * h) % LANES, 1),
                           jnp.where(c_lanes, 1.0, 0.0))
            if n_invalid:
                kx = jnp.where(e == 6, key_mask, kx)
            vx = jnp.where(e == 0, 1.0, 0.0)
            fq_ref[0, h] = jnp.where(is_data, fq, qx).astype(BF16)
            fk_ref[0, h] = jnp.where(is_data, fk, kx).astype(BF16)
            fv_ref[0, h] = jnp.where(is_data, fv, vx).astype(BF16)

    rqk = proj(0, 2 * RET_QK)
    for base, c_ref, s_ref, o_ref in ((0, cq_ref, sq_ref, rq_ref), (RET_QK, ck_ref, sk_ref, rk_ref)):
        for blk in range(RET_QK // LANES):
            xb = rqk[:, base + LANES * blk:base + LANES * (blk + 1)]
            swapped = jnp.where(first_half, pltpu.roll(xb, LANES - 32, 1), pltpu.roll(xb, 32, 1))
            o_ref[0, :, LANES * blk:LANES * (blk + 1)] = (xb * c_ref[...] + swapped * s_ref[...]).astype(BF16)

    rv_ref[0] = proj(2 * RET_QK, RET_V).astype(BF16)
    rg = proj(2 * RET_QK + RET_V, RET_V)
    rg_ref[0] = (rg * jax.nn.sigmoid(rg)).astype(BF16)


def _inproj(x, g, w_main, w_ff, fb, tabs, c0, *, tl, n_invalid):
    nb, rows, _ = x.shape
    grid = (nb, rows // tl)
    row_blk = lambda w: pl.BlockSpec((1, tl, w), lambda b, j: (b, j, 0))
    head_blk = pl.BlockSpec((1, FOX_HEADS, tl, LANES), lambda b, j: (b, 0, j, 0))
    const = lambda shape: pl.BlockSpec(shape, lambda b, j: (0,) * len(shape), pipeline_mode=pl.Buffered(1))
    tab_blk = pl.BlockSpec((tl, LANES), lambda b, j: (j, 0))
    out_shape = (
        jax.ShapeDtypeStruct((nb, rows, RET_QK), BF16), jax.ShapeDtypeStruct((nb, rows, RET_QK), BF16),
        jax.ShapeDtypeStruct((nb, rows, RET_V), BF16), jax.ShapeDtypeStruct((nb, rows, RET_V), BF16),
        jax.ShapeDtypeStruct((nb, FOX_HEADS, rows, LANES), BF16),
        jax.ShapeDtypeStruct((nb, FOX_HEADS, rows, LANES), BF16),
        jax.ShapeDtypeStruct((nb, FOX_HEADS, rows, LANES), BF16),
        jax.ShapeDtypeStruct((nb, SUBLANES, LANES), F32),
    )
    return pl.pallas_call(
        functools.partial(_inproj_kernel, tl=tl, n_invalid=n_invalid),
        grid=grid,
        in_specs=[row_blk(D_MODEL), const((1, D_MODEL)), const((D_MODEL, MAIN_W)), const((D_MODEL, LANES)),
                  const((1, LANES)), tab_blk, tab_blk, tab_blk, tab_blk, const((SUBLANES, LANES))],
        out_specs=(row_blk(RET_QK), row_blk(RET_QK), row_blk(RET_V), row_blk(RET_V),
                   head_blk, head_blk, head_blk, pl.BlockSpec((1, SUBLANES, LANES), lambda b, j: (b, 0, 0))),
        out_shape=out_shape,
        scratch_shapes=[pltpu.VMEM((SUBLANES, LANES), F32)],
        compiler_params=pltpu.CompilerParams(dimension_semantics=("arbitrary", "arbitrary"),
                                             vmem_limit_bytes=VMEM_LIMIT),
        name=f"inproj_{rows}",
    )(x, g, w_main, w_ff, fb, *tabs, c0)


def _retention_kernel(q_ref, k_ref, v_ref, gate_ref, r0_ref, d_ref, wq_ref, wk_ref, gt_ref, ng_ref,
                      o_ref, rf_ref, state_ref, *, t, n_blocks):
    state_ref[...] = r0_ref[...]
    lane = lax.broadcasted_iota(jnp.int32, (t, LANES), 1)

    def block(i, carry):
        rows = pl.ds(pl.multiple_of(i * t, t), t)
        for pair in range(RET_HEADS // 2):
            q = q_ref[0, rows, LANES * pair:LANES * (pair + 1)]
            k = k_ref[0, rows, LANES * pair:LANES * (pair + 1)]
            qw = (q.astype(F32) * wq_ref[pair]).astype(BF16)
            kw = (k.astype(F32) * wk_ref[pair]).astype(BF16)
            for hh in range(2):
                h = 2 * pair + hh
                cols = slice(RET_DV * h, RET_DV * (h + 1))
                in_head = (lane < RET_DK) if hh == 0 else (lane >= RET_DK)
                vh = v_ref[0, rows, cols]
                qm = jnp.where(in_head, q, jnp.zeros_like(q))
                s = lax.dot_general(qm, k, (((1,), (1,)), ((), ())), preferred_element_type=F32)
                intra = jnp.dot((s * d_ref[h]).astype(BF16), vh, preferred_element_type=F32)
                r = state_ref[h]
                inter = jnp.dot(qw, r.astype(BF16), preferred_element_type=F32)
                kwm = jnp.where(in_head, kw, jnp.zeros_like(kw))
                u = lax.dot_general(kwm, vh, (((0,), (0,)), ((), ())), preferred_element_type=F32)
                state_ref[h] = gt_ref[h] * r + u
                o = intra + inter
                o = (o * _rms(o)) * ng_ref[:, cols] * gate_ref[0, rows, cols].astype(F32)
                o_ref[0, rows, cols] = o.astype(BF16)
        return carry

    lax.fori_loop(0, n_blocks, block, 0, unroll=min(n_blocks, RET_UNROLL))
    rf_ref[0] = state_ref[...]


def _retention(rq, rk, rv, gate, r0, tabs, norm_g, *, t):
    nb, rows, _ = rq.shape
    d_tab, wq_tab, wk_tab, gt_tab = tabs
    seq = lambda w: pl.BlockSpec((1, rows, w), lambda b: (b, 0, 0))
    const = lambda shape: pl.BlockSpec(shape, lambda b: (0,) * len(shape))
    state_shape = (RET_HEADS, LANES, RET_DV)
    return pl.pallas_call(
        functools.partial(_retention_kernel, t=t, n_blocks=rows // t),
        grid=(nb,),
        in_specs=[seq(RET_QK), seq(RET_QK), seq(RET_V), seq(RET_V), const(state_shape),
                  const((RET_HEADS, t, t)), const((RET_HEADS // 2, t, LANES)), const((RET_HEADS // 2, t, LANES)),
                  const((RET_HEADS, 1, RET_DV)), const((1, RET_V))],
        out_specs=(seq(RET_V), pl.BlockSpec((1,) + state_shape, lambda b: (b, 0, 0, 0))),
        out_shape=(jax.ShapeDtypeStruct((nb, rows, RET_V), BF16),
                   jax.ShapeDtypeStruct((nb,) + state_shape, F32)),
        scratch_shapes=[pltpu.VMEM(state_shape, F32)],
        compiler_params=pltpu.CompilerParams(dimension_semantics=("arbitrary",),
                                             vmem_limit_bytes=VMEM_LIMIT),
        name=f"retention_{rows}",
    )(rq, rk, rv, gate, r0, d_tab, wq_tab, wk_tab, gt_tab, norm_g)


def _fox_kernel(*refs, rows, tk, has_prefix):
    if has_prefix:
        q_ref, k_ref, v_ref, kp_ref, vp_ref, o_ref, m_ref, acc_ref = refs
    else:
        q_ref, k_ref, v_ref, o_ref, m_ref, acc_ref = refs
    lane = lax.broadcasted_iota(jnp.int32, (rows, LANES), 1)
    causal = lax.broadcasted_iota(jnp.int32, (tk, tk), 1) <= lax.broadcasted_iota(jnp.int32, (tk, tk), 0)

    def scores(hh, lo, k):
        return lax.dot_general(q_ref[0, hh, lo:rows, :], k, (((1,), (1,)), ((), ())), preferred_element_type=F32)

    def probs(s, m):
        return jnp.exp2((s - jnp.tile(m, (1, s.shape[1] // LANES))).astype(BF16))

    if has_prefix:
        for hh in range(2):
            s_pre = scores(hh, 0, kp_ref[0, hh])
            m = jnp.broadcast_to(jnp.max(s_pre, axis=1, keepdims=True), (rows, LANES))
            m_ref[hh] = m
            acc_ref[hh] = jnp.dot(probs(s_pre, m), vp_ref[0, hh], preferred_element_type=F32)
    else:
        m_ref[...] = jnp.full(m_ref.shape, NEG, F32)
        acc_ref[...] = jnp.zeros(acc_ref.shape, F32)

    for c in range(rows // tk):
        lo = c * tk
        for hh in range(2):
            s = scores(hh, lo, k_ref[0, hh, lo:lo + tk, :])
            s_top = jnp.where(causal, s[:tk], NEG)
            s = jnp.concatenate([s_top, s[tk:]], axis=0) if lo + tk < rows else s_top
            m_old = m_ref[hh, lo:rows]
            m_new = jnp.maximum(m_old, jnp.max(s, axis=1, keepdims=True))
            m_ref[hh, lo:rows] = m_new
            pv = jnp.dot(probs(s, m_new), v_ref[0, hh, lo:lo + tk, :], preferred_element_type=F32)
            acc_ref[hh, lo:rows] = jnp.exp2(m_old - m_new) * acc_ref[hh, lo:rows] + pv

    outs = []
    for hh in range(2):
        acc = acc_ref[hh]
        sum_lane = FOX_DH if hh == 0 else 0
        outs.append(acc / acc[:, sum_lane:sum_lane + 1])
    o_ref[0] = jnp.where(lane < FOX_DH, outs[0], outs[1]).astype(BF16)


def _fox(fq, fk, fv, kp, vp, *, tk):
    nb, _, rows, _ = fq.shape
    has_prefix = kp is not None
    pair_blk = pl.BlockSpec((1, 2, rows, LANES), lambda b, p: (b, p, 0, 0))
    in_specs = [pair_blk, pair_blk, pair_blk]
    args = [fq, fk, fv]
    if has_prefix:
        pre_blk = pl.BlockSpec((1, 2, PREFIX, LANES), lambda b, p: (0, p, 0, 0))
        in_specs += [pre_blk, pre_blk]
        args += [kp, vp]
    return pl.pallas_call(
        functools.partial(_fox_kernel, rows=rows, tk=tk, has_prefix=has_prefix),
        grid=(nb, FOX_HEADS // 2),
        in_specs=in_specs,
        out_specs=pl.BlockSpec((1, rows, LANES), lambda b, p: (b, 0, p)),
        out_shape=jax.ShapeDtypeStruct((nb, rows, FOX_W), BF16),
        scratch_shapes=[pltpu.VMEM((2, rows, LANES), F32), pltpu.VMEM((2, rows, LANES), F32)],
        compiler_params=pltpu.CompilerParams(dimension_semantics=("arbitrary", "arbitrary"),
                                             vmem_limit_bytes=VMEM_LIMIT),
        name=f"fox_{rows}",
    )(*args)


def _mixer_residual(x_ref, or_ref, of_ref, wo_ref):
    y = jnp.dot(or_ref[0], wo_ref[0:RET_V, :], preferred_element_type=F32)
    y = y + jnp.dot(of_ref[0], wo_ref[RET_V:RET_V + FOX_W, :], preferred_element_type=F32)
    return x_ref[0] + y


def _ffn_prefix_kernel(x_ref, or_ref, of_ref, wo_ref, g2_ref, wa_ref, a_ref, *, n_invalid):
    h1 = _mixer_residual(x_ref, or_ref, of_ref, wo_ref)
    hn = ((h1 * _rms(h1)) * g2_ref[...]).astype(BF16)
    a = jnp.dot(hn, wa_ref[...], preferred_element_type=F32)
    row = lax.broadcasted_iota(jnp.int32, a.shape, 0)
    a_ref[...] = jnp.where(row >= n_invalid, a, 0.0)


def _ffn_prefix(x, o_r, o_f, w_out, g2, w_up):
    full = lambda shape: pl.BlockSpec(shape, lambda i: (0,) * len(shape))
    return pl.pallas_call(
        functools.partial(_ffn_prefix_kernel, n_invalid=N_PAD),
        grid=(1,),
        in_specs=[full((1, PREFIX, D_MODEL)), full((1, PREFIX, RET_V)), full((1, PREFIX, FOX_W)),
                  full((D_MODEL, D_MODEL)), full((1, D_MODEL)), full((D_MODEL, D_FF))],
        out_specs=full((PREFIX, D_FF)),
        out_shape=jax.ShapeDtypeStruct((PREFIX, D_FF), F32),
        compiler_params=pltpu.CompilerParams(vmem_limit_bytes=VMEM_LIMIT),
        name="ffn_prefix",
    )(x, o_r, o_f, w_out, g2, w_up)


def _ffn_kernel(x_ref, or_ref, of_ref, wo_ref, g2_ref, wu_ref, cw_ref, cb_ref, wd_ref, gf_ref, halo_ref,
                o_ref, carry_ref, abuf_ref, gated_ref, *, tm):
    @pl.when(pl.program_id(1) == 0)
    def _():
        carry_ref[...] = halo_ref[...]

    h1 = _mixer_residual(x_ref, or_ref, of_ref, wo_ref)
    hn = ((h1 * _rms(h1)) * g2_ref[...]).astype(BF16)
    for c in range(D_FF // FF_CHUNK):
        cols = slice(c * FF_CHUNK, (c + 1) * FF_CHUNK)
        a = jnp.dot(hn, wu_ref[:, cols], preferred_element_type=F32)
        b = jnp.dot(hn, wu_ref[:, D_FF + c * FF_CHUNK:D_FF + (c + 1) * FF_CHUNK], preferred_element_type=F32)
        abuf_ref[0:SUBLANES, :] = carry_ref[:, cols]
        abuf_ref[SUBLANES:SUBLANES + tm, :] = a
        carry_ref[:, cols] = a[tm - SUBLANES:tm, :]
        acc = cb_ref[:, cols] + abuf_ref[SUBLANES - 2:SUBLANES - 2 + tm, :] * cw_ref[0:1, cols]
        acc = acc + abuf_ref[SUBLANES - 1:SUBLANES - 1 + tm, :] * cw_ref[1:2, cols]
        acc = acc + a * cw_ref[2:3, cols]
        gated_ref[:, cols] = (acc * jax.nn.sigmoid(acc) * b).astype(BF16)
    h2 = h1 + jnp.dot(gated_ref[...], wd_ref[...], preferred_element_type=F32)
    o_ref[0] = (h2 * _rms(h2)) * gf_ref[...]


def _ffn(x, o_r, o_f, w_out, g2, w_up, conv_w, conv_b, w_down, gf, halo, *, tm):
    nb, rows, _ = x.shape
    row_blk = lambda w: pl.BlockSpec((1, tm, w), lambda b, j: (b, j, 0))
    const = lambda shape: pl.BlockSpec(shape, lambda b, j: (0,) * len(shape), pipeline_mode=pl.Buffered(1))
    return pl.pallas_call(
        functools.partial(_ffn_kernel, tm=tm),
        grid=(nb, rows // tm),
        in_specs=[row_blk(D_MODEL), row_blk(RET_V), row_blk(FOX_W), const((D_MODEL, D_MODEL)),
                  const((1, D_MODEL)), const((D_MODEL, 2 * D_FF)), const((CONV_W, D_FF)), const((1, D_FF)),
                  const((D_FF, D_MODEL)), const((1, D_MODEL)), const((SUBLANES, D_FF))],
        out_specs=row_blk(D_MODEL),
        out_shape=jax.ShapeDtypeStruct((nb, rows, D_MODEL), F32),
        scratch_shapes=[pltpu.VMEM((SUBLANES, D_FF), F32), pltpu.VMEM((SUBLANES + tm, FF_CHUNK), F32),
                        pltpu.VMEM((tm, D_FF), BF16)],
        compiler_params=pltpu.CompilerParams(dimension_semantics=("arbitrary", "arbitrary"),
                                             vmem_limit_bytes=VMEM_LIMIT),
        name="ffn",
    )(x, o_r, o_f, w_out, g2, w_up, conv_w, conv_b, w_down, gf, halo)


def _rotary_tables(start, length):
    half = RET_DK // 2
    inv = (1.0 / (ROPE_BASE ** (np.arange(half, dtype=np.float32) / np.float32(half)))).astype(np.float32)
    pos = start + np.arange(length)
    ang = pos.astype(np.float32)[:, None] * inv[None, :]
    cos = np.tile(np.cos(ang), (1, LANES // half))
    sin = np.tile(np.concatenate([-np.sin(ang), np.sin(ang)], axis=1), (1, LANES // RET_DK))
    k_scale = (np.float32(RET_DK ** -0.5) * (pos >= N_PAD).astype(np.float32))[:, None]
    return cos, sin, cos * k_scale, sin * k_scale


def _retention_tables(t):
    f32 = np.float32
    log_g = np.log1p(-np.exp2(f32(-5.0) - np.arange(RET_HEADS, dtype=f32)))
    n = np.arange(t, dtype=f32)
    visible = (np.arange(t)[None, :] // CHUNK) <= (np.arange(t)[:, None] // CHUNK)
    d = np.exp(np.abs(n[:, None] - n[None, :])[None] * log_g[:, None, None]) * visible[None].astype(f32)
    lane_log_g = np.repeat(log_g, RET_DK).reshape(RET_HEADS // 2, 1, LANES)
    wq = np.exp((n + f32(1.0))[None, :, None] * lane_log_g)
    wk = np.exp((f32(t) - f32(1.0) - n)[None, :, None] * lane_log_g)
    gt = np.broadcast_to(np.exp(f32(t) * log_g)[:, None, None], (RET_HEADS, 1, RET_DV))
    return tuple(np.ascontiguousarray(a, dtype=f32) for a in (d, wq, wk, gt))


def _gate_lanes(v):
    spread = jnp.stack([v] + [jnp.zeros_like(v)] * (GATE_STRIDE - 1), axis=-1)
    spread = spread.reshape(v.shape[:-1] + (GATE_STRIDE * FOX_HEADS,))
    return jnp.pad(spread, [(0, 0)] * (v.ndim - 1) + [(0, LANES - GATE_STRIDE * FOX_HEADS)])


def kernel(x, meta_tokens, attn_norm_g, w_in, fox_forget_b, ret_norm_g, w_out, ffn_norm_g, w_up, conv_w,
           conv_b, w_down, final_norm_g):
    assert w_in.shape[0] == 1, "single-layer block"
    batch, seq, _ = x.shape
    w_main = w_in[0].astype(BF16)
    w_ff = _gate_lanes(w_in[0, :, MAIN_W:]).astype(BF16)
    fb = _gate_lanes(fox_forget_b[0]).reshape(1, LANES)
    g1 = attn_norm_g[0].reshape(1, D_MODEL)
    g2 = ffn_norm_g[0].reshape(1, D_MODEL)
    gf = final_norm_g.reshape(1, D_MODEL)
    ng = ret_norm_g[0].reshape(1, RET_V)
    w_out_b = w_out[0].astype(BF16)
    w_up_b = w_up[0].astype(BF16)
    w_down_b = w_down[0].astype(BF16)
    cb = conv_b[0].reshape(1, D_FF)
    h_pre = jnp.concatenate([jnp.zeros((N_PAD, D_MODEL), x.dtype), meta_tokens.astype(x.dtype)], axis=0)[None]

    zeros_c = jnp.zeros((SUBLANES, LANES), F32)
    rq0, rk0, rv0, rg0, fq0, fk0, fv0, c_pre = _inproj(h_pre, g1, w_main, w_ff, fb, _rotary_tables(0, PREFIX),
                                                        zeros_c, tl=PREFIX, n_invalid=N_PAD)
    r_zero = jnp.zeros((RET_HEADS, LANES, RET_DV), F32)
    or0, r_pre = _retention(rq0, rk0, rv0, rg0, r_zero, _retention_tables(PREFIX), ng, t=PREFIX)
    of0 = _fox(fq0, fk0, fv0, None, None, tk=PREFIX)
    a_pre = _ffn_prefix(h_pre, or0, of0, w_out_b, g2, w_up_b)
    halo = a_pre[PREFIX - SUBLANES:]

    rq, rk, rv, rg, fq, fk, fv, _ = _inproj(x, g1, w_main, w_ff, fb, _rotary_tables(PREFIX, seq), c_pre[0],
                                            tl=1024, n_invalid=0)
    o_r, _ = _retention(rq, rk, rv, rg, r_pre[0], _retention_tables(256), ng, t=256)
    o_f = _fox(fq, fk, fv, fk0, fv0, tk=512)
    return _ffn(x, o_r, o_f, w_out_b, g2, w_up_b, conv_w[0], cb, w_down_b, gf, halo, tm=1024)
```

```python
import functools

import jax
import jax.numpy as jnp
import numpy as np
from jax import lax
from jax.experimental import pallas as pl
from jax.experimental.pallas import tpu as pltpu

F32 = jnp.float32
BF16 = jnp.bfloat16

D_MODEL = 1024
CHUNK = 64
N_META = 16
PREFIX = 128
N_PAD = PREFIX - N_META
RET_HEADS = 4
RET_DK = 64
RET_DV = 128
FOX_HEADS = 8
FOX_DH = 64
D_FF = 2816
CONV_W = 3
ROPE_BASE = 10000.0
EPS = 1e-6
NEG = -1e30

RET_QK = RET_HEADS * RET_DK
RET_V = RET_HEADS * RET_DV
FOX_W = FOX_HEADS * FOX_DH
MAIN_W = 2 * RET_QK + 2 * RET_V + 3 * FOX_W
LANES = 128
SUBLANES = 8
N_EXTRA = 7
GATE_STRIDE = 3
LOG2E = 1.4426950408889634
CUMSUM_ROWS = 256
VMEM_LIMIT = 56 * 1024 * 1024
FF_CHUNK = 256
RET_UNROLL = 8


def _rms(x):
    return lax.rsqrt(jnp.mean(x * x, axis=-1, keepdims=True) + EPS)


def _split3(c):
    hi = c.astype(BF16).astype(F32)
    r = c - hi
    mid = r.astype(BF16).astype(F32)
    return hi, mid, r - mid


def _pack3(c, gate_lane):
    hi, mid, lo = (jnp.where(gate_lane, part, 0.0) for part in _split3(c))
    return hi + pltpu.roll(mid, 1, 1) + pltpu.roll(lo, 2, 1)


def _inproj_kernel(x_ref, g_ref, wm_ref, wf_ref, fb_ref, cq_ref, sq_ref, ck_ref, sk_ref, c0_ref,
                   *rest, tl, n_invalid, n_cast):
    cast_in, outs, cast_out, carry_ref = (rest[:n_cast], rest[n_cast:n_cast + 8],
                                          rest[n_cast + 8:2 * n_cast + 8], rest[-1])
    for src, dst in zip(cast_in, cast_out):
        dst[...] = src[...].astype(BF16)
    rq_ref, rk_ref, rv_ref, rg_ref, fq_ref, fk_ref, fv_ref, cl_ref = outs
    j = pl.program_id(1)

    @pl.when(j == 0)
    def _():
        carry_ref[...] = c0_ref[...]

    x = x_ref[0]
    hn = ((x * _rms(x)) * g_ref[...]).astype(BF16)

    def proj(lo, width):
        return jnp.dot(hn, wm_ref[:, lo:lo + width], preferred_element_type=F32)

    lane = lax.broadcasted_iota(jnp.int32, (1, LANES), 1)
    first_half = (lane & 32) == 0

    gate_lane = functools.reduce(jnp.logical_or, [lane == GATE_STRIDE * h for h in range(FOX_HEADS)])
    ff = jnp.dot(hn, wf_ref[...], preferred_element_type=F32) + fb_ref[...]
    lf = jnp.minimum(ff, 0.0) - jnp.log(1.0 + jnp.exp(-jnp.abs(ff)))
    packed = _pack3(lf, gate_lane).astype(BF16)
    sub = min(tl, CUMSUM_ROWS)
    tri = lax.broadcasted_iota(jnp.int32, (sub, sub), 0) >= lax.broadcasted_iota(jnp.int32, (sub, sub), 1)
    tri = jnp.where(tri, 1.0, 0.0).astype(BF16)
    carry = carry_ref[0:1, :]
    c_blocks = []
    for r in range(tl // sub):
        c3 = jnp.dot(tri, packed[r * sub:(r + 1) * sub], preferred_element_type=F32)
        c_blk = (c3 + pltpu.roll(c3, LANES - 1, 1)) + pltpu.roll(c3, LANES - 2, 1) + carry
        carry = c_blk[sub - 1:sub, :]
        c_blocks.append(c_blk)
    c = jnp.concatenate(c_blocks, axis=0) if len(c_blocks) > 1 else c_blocks[0]
    carry_ref[...] = jnp.broadcast_to(carry, (SUBLANES, LANES))
    cl_ref[0] = carry_ref[...]

    c_triples = _pack3(c * LOG2E, gate_lane)
    neg_triples = -c_triples
    if n_invalid:
        row = j * tl + lax.broadcasted_iota(jnp.int32, (tl, LANES), 0)
        key_mask = jnp.where(row >= n_invalid, 0.0, NEG)
    fox_base = 2 * RET_QK + 2 * RET_V
    fq_all = proj(fox_base, FOX_W) * (FOX_DH ** -0.5 * LOG2E)
    fk_all = proj(fox_base + FOX_W, FOX_W)
    fv_all = proj(fox_base + 2 * FOX_W, FOX_W)
    for pair in range(FOX_HEADS // 2):
        cols = slice(LANES * pair, LANES * (pair + 1))
        fq, fk, fv = fq_all[:, cols], fk_all[:, cols], fv_all[:, cols]
        for hh in range(2):
            h = 2 * pair + hh
            is_data = (lane < FOX_DH) if hh == 0 else (lane >= FOX_DH)
            e = (lane - FOX_DH) if hh == 0 else lane
            e0 = FOX_DH if hh == 0 else 0
            c_lanes = (e >= 0) & (e < 3)
            negc_lanes = (e >= 3) & (e < 6)
            qx = jnp.where(c_lanes, pltpu.roll(c_triples, (e0 - GATE_STRIDE * h) % LANES, 1),
                           jnp.where((e >= 3) & (e < N_EXTRA), 1.0, 0.0))
            kx = jnp.where(negc_lanes, pltpu.roll(neg_triples, (e0 + 3 - GATE_STRIDE * h) % LANES, 1),
                           jnp.where(c_lanes, 1.0, 0.0))
            if n_invalid:
                kx = jnp.where(e == 6, key_mask, kx)
            vx = jnp.where(e == 0, 1.0, 0.0)
            fq_ref[0, h] = jnp.where(is_data, fq, qx).astype(BF16)
            fk_ref[0, h] = jnp.where(is_data, fk, kx).astype(BF16)
            fv_ref[0, h] = jnp.where(is_data, fv, vx).astype(BF16)

    rqk = proj(0, 2 * RET_QK)
    for base, c_ref, s_ref, o_ref in ((0, cq_ref, sq_ref, rq_ref), (RET_QK, ck_ref, sk_ref, rk_ref)):
        for blk in range(RET_QK // LANES):
            xb = rqk[:, base + LANES * blk:base + LANES * (blk + 1)]
            swapped = jnp.where(first_half, pltpu.roll(xb, LANES - 32, 1), pltpu.roll(xb, 32, 1))
            o_ref[0, :, LANES * blk:LANES * (blk + 1)] = (xb * c_ref[...] + swapped * s_ref[...]).astype(BF16)

    rv_ref[0] = proj(2 * RET_QK, RET_V).astype(BF16)
    rg = proj(2 * RET_QK + RET_V, RET_V)
    rg_ref[0] = (rg * jax.nn.sigmoid(rg)).astype(BF16)


def _inproj(x, g, w_main, w_ff, fb, tabs, c0, to_bf16=(), *, tl, n_invalid):
    nb, rows, _ = x.shape
    grid = (nb, rows // tl)
    n_steps = nb * (rows // tl)

    def slab(a):
        n = n_steps
        while a.shape[0] % (n * 2 * SUBLANES):
            n //= 2
        return pl.BlockSpec((a.shape[0] // n, a.shape[1]),
                            lambda b, j: ((b * (rows // tl) + j) // (n_steps // n), 0))

    cast_specs = [slab(a) for a in to_bf16]
    row_blk = lambda w: pl.BlockSpec((1, tl, w), lambda b, j: (b, j, 0))
    head_blk = pl.BlockSpec((1, FOX_HEADS, tl, LANES), lambda b, j: (b, 0, j, 0))
    const = lambda shape: pl.BlockSpec(shape, lambda b, j: (0,) * len(shape), pipeline_mode=pl.Buffered(1))
    tab_blk = pl.BlockSpec((tl, LANES), lambda b, j: (j, 0))
    out_shape = (
        jax.ShapeDtypeStruct((nb, rows, RET_QK), BF16), jax.ShapeDtypeStruct((nb, rows, RET_QK), BF16),
        jax.ShapeDtypeStruct((nb, rows, RET_V), BF16), jax.ShapeDtypeStruct((nb, rows, RET_V), BF16),
        jax.ShapeDtypeStruct((nb, FOX_HEADS, rows, LANES), BF16),
        jax.ShapeDtypeStruct((nb, FOX_HEADS, rows, LANES), BF16),
        jax.ShapeDtypeStruct((nb, FOX_HEADS, rows, LANES), BF16),
        jax.ShapeDtypeStruct((nb, SUBLANES, LANES), F32),
    )
    return pl.pallas_call(
        functools.partial(_inproj_kernel, tl=tl, n_invalid=n_invalid, n_cast=len(to_bf16)),
        grid=grid,
        in_specs=[row_blk(D_MODEL), const((1, D_MODEL)), const((D_MODEL, MAIN_W)), const((D_MODEL, LANES)),
                  const((1, LANES)), tab_blk, tab_blk, tab_blk, tab_blk, const((SUBLANES, LANES))] + cast_specs,
        out_specs=(row_blk(RET_QK), row_blk(RET_QK), row_blk(RET_V), row_blk(RET_V),
                   head_blk, head_blk, head_blk, pl.BlockSpec((1, SUBLANES, LANES), lambda b, j: (b, 0, 0)))
        + tuple(cast_specs),
        out_shape=out_shape + tuple(jax.ShapeDtypeStruct(a.shape, BF16) for a in to_bf16),
        scratch_shapes=[pltpu.VMEM((SUBLANES, LANES), F32)],
        compiler_params=pltpu.CompilerParams(dimension_semantics=("arbitrary", "arbitrary"),
                                             vmem_limit_bytes=VMEM_LIMIT),
        name=f"inproj_{rows}",
    )(x, g, w_main, w_ff, fb, *tabs, c0, *to_bf16)


def _retention_kernel(q_ref, k_ref, v_ref, gate_ref, r0_ref, d_ref, wq_ref, wk_ref, gt_ref, ng_ref,
                      o_ref, rf_ref, state_ref, *, t, n_blocks):
    state_ref[...] = r0_ref[...]
    lane = lax.broadcasted_iota(jnp.int32, (t, LANES), 1)

    def block(i, carry):
        rows = pl.ds(pl.multiple_of(i * t, t), t)
        q = q_ref[0, rows, :]
        k = k_ref[0, rows, :]
        v = v_ref[0, rows, :]
        qw = (q.astype(F32) * wq_ref[0]).astype(BF16)
        kw = (k.astype(F32) * wk_ref[0]).astype(BF16)
        for hh in range(2):
            in_head = (lane < RET_DK) if hh == 0 else (lane >= RET_DK)
            vh = v[:, RET_DV * hh:RET_DV * (hh + 1)]
            qm = jnp.where(in_head, q, jnp.zeros_like(q))
            s = lax.dot_general(qm, k, (((1,), (1,)), ((), ())), preferred_element_type=F32)
            intra = jnp.dot((s * d_ref[hh]).astype(BF16), vh, preferred_element_type=F32)
            r = state_ref[hh]
            inter = jnp.dot(qw, r.astype(BF16), preferred_element_type=F32)
            kwm = jnp.where(in_head, kw, jnp.zeros_like(kw))
            u = lax.dot_general(kwm, vh, (((0,), (0,)), ((), ())), preferred_element_type=F32)
            state_ref[hh] = gt_ref[hh] * r + u
            o = intra + inter
            cols = slice(RET_DV * hh, RET_DV * (hh + 1))
            o = (o * _rms(o)) * ng_ref[:, cols] * gate_ref[0, rows, cols].astype(F32)
            o_ref[0, rows, cols] = o.astype(BF16)
        return carry

    lax.fori_loop(0, n_blocks, block, 0, unroll=min(n_blocks, RET_UNROLL))
    rf_ref[0] = state_ref[...]


def _retention(rq, rk, rv, gate, r0, tabs, norm_g, *, t):
    nb, rows, _ = rq.shape
    d_tab, wq_tab, wk_tab, gt_tab = tabs
    n_pairs = RET_HEADS // 2
    seq = lambda w: pl.BlockSpec((1, rows, w), lambda b, p: (b, 0, p))
    return pl.pallas_call(
        functools.partial(_retention_kernel, t=t, n_blocks=rows // t),
        grid=(nb, n_pairs),
        in_specs=[seq(LANES), seq(LANES), seq(2 * RET_DV), seq(2 * RET_DV),
                  pl.BlockSpec((2, LANES, RET_DV), lambda b, p: (p, 0, 0)),
                  pl.BlockSpec((2, t, t), lambda b, p: (p, 0, 0)),
                  pl.BlockSpec((1, t, LANES), lambda b, p: (p, 0, 0)),
                  pl.BlockSpec((1, t, LANES), lambda b, p: (p, 0, 0)),
                  pl.BlockSpec((2, 1, RET_DV), lambda b, p: (p, 0, 0)),
                  pl.BlockSpec((1, 2 * RET_DV), lambda b, p: (0, p))],
        out_specs=(seq(2 * RET_DV), pl.BlockSpec((1, 2, LANES, RET_DV), lambda b, p: (b, p, 0, 0))),
        out_shape=(jax.ShapeDtypeStruct((nb, rows, RET_V), BF16),
                   jax.ShapeDtypeStruct((nb, RET_HEADS, LANES, RET_DV), F32)),
        scratch_shapes=[pltpu.VMEM((2, LANES, RET_DV), F32)],
        compiler_params=pltpu.CompilerParams(dimension_semantics=("arbitrary", "arbitrary"),
                                             vmem_limit_bytes=VMEM_LIMIT),
        name=f"retention_{rows}",
    )(rq, rk, rv, gate, r0, d_tab, wq_tab, wk_tab, gt_tab, norm_g)


def _fox_kernel(*refs, rows, tk, has_prefix):
    if has_prefix:
        q_ref, k_ref, v_ref, kp_ref, vp_ref, o_ref, m_ref, acc_ref = refs
    else:
        q_ref, k_ref, v_ref, o_ref, m_ref, acc_ref = refs
    lane = lax.broadcasted_iota(jnp.int32, (rows, LANES), 1)
    causal = lax.broadcasted_iota(jnp.int32, (tk, tk), 1) <= lax.broadcasted_iota(jnp.int32, (tk, tk), 0)

    def scores(hh, lo, k):
        return lax.dot_general(q_ref[0, hh, lo:rows, :], k, (((1,), (1,)), ((), ())), preferred_element_type=F32)

    def probs(s, m):
        return jnp.exp2((s - jnp.tile(m, (1, s.shape[1] // LANES))).astype(BF16))

    if has_prefix:
        for hh in range(2):
            s_pre = scores(hh, 0, kp_ref[0, hh])
            m = jnp.broadcast_to(jnp.max(s_pre, axis=1, keepdims=True), (rows, LANES))
            m_ref[hh] = m
            acc_ref[hh] = jnp.dot(probs(s_pre, m), vp_ref[0, hh], preferred_element_type=F32)
    else:
        m_ref[...] = jnp.full(m_ref.shape, NEG, F32)
        acc_ref[...] = jnp.zeros(acc_ref.shape, F32)

    for c in range(rows // tk):
        lo = c * tk
        for hh in range(2):
            s = scores(hh, lo, k_ref[0, hh, lo:lo + tk, :])
            s_top = jnp.where(causal, s[:tk], NEG)
            s = jnp.concatenate([s_top, s[tk:]], axis=0) if lo + tk < rows else s_top
            m_old = m_ref[hh, lo:rows]
            m_new = jnp.maximum(m_old, jnp.max(s, axis=1, keepdims=True))
            m_ref[hh, lo:rows] = m_new
            pv = jnp.dot(probs(s, m_new), v_ref[0, hh, lo:lo + tk, :], preferred_element_type=F32)
            acc_ref[hh, lo:rows] = jnp.exp2(m_old - m_new) * acc_ref[hh, lo:rows] + pv

    outs = []
    for hh in range(2):
        acc = acc_ref[hh]
        sum_lane = FOX_DH if hh == 0 else 0
        outs.append(acc / acc[:, sum_lane:sum_lane + 1])
    o_ref[0] = jnp.where(lane < FOX_DH, outs[0], outs[1]).astype(BF16)


def _fox(fq, fk, fv, kp, vp, *, tk):
    nb, _, rows, _ = fq.shape
    has_prefix = kp is not None
    pair_blk = pl.BlockSpec((1, 2, rows, LANES), lambda b, p: (b, p, 0, 0))
    in_specs = [pair_blk, pair_blk, pair_blk]
    args = [fq, fk, fv]
    if has_prefix:
        pre_blk = pl.BlockSpec((1, 2, PREFIX, LANES), lambda b, p: (0, p, 0, 0))
        in_specs += [pre_blk, pre_blk]
        args += [kp, vp]
    return pl.pallas_call(
        functools.partial(_fox_kernel, rows=rows, tk=tk, has_prefix=has_prefix),
        grid=(nb, FOX_HEADS // 2),
        in_specs=in_specs,
        out_specs=pl.BlockSpec((1, rows, LANES), lambda b, p: (b, 0, p)),
        out_shape=jax.ShapeDtypeStruct((nb, rows, FOX_W), BF16),
        scratch_shapes=[pltpu.VMEM((2, rows, LANES), F32), pltpu.VMEM((2, rows, LANES), F32)],
        compiler_params=pltpu.CompilerParams(dimension_semantics=("arbitrary", "arbitrary"),
                                             vmem_limit_bytes=VMEM_LIMIT),
        name=f"fox_{rows}",
    )(*args)


def _mixer_residual(x_ref, or_ref, of_ref, wo_ref):
    y = jnp.dot(or_ref[0], wo_ref[0:RET_V, :], preferred_element_type=F32)
    y = y + jnp.dot(of_ref[0], wo_ref[RET_V:RET_V + FOX_W, :], preferred_element_type=F32)
    return x_ref[0] + y


def _ffn_prefix_kernel(x_ref, or_ref, of_ref, wo_ref, g2_ref, wa_ref, a_ref, *, n_invalid):
    h1 = _mixer_residual(x_ref, or_ref, of_ref, wo_ref)
    hn = ((h1 * _rms(h1)) * g2_ref[...]).astype(BF16)
    a = jnp.dot(hn, wa_ref[...], preferred_element_type=F32)
    row = lax.broadcasted_iota(jnp.int32, a.shape, 0)
    a_ref[...] = jnp.where(row >= n_invalid, a, 0.0)


def _ffn_prefix(x, o_r, o_f, w_out, g2, w_up):
    full = lambda shape: pl.BlockSpec(shape, lambda i: (0,) * len(shape))
    return pl.pallas_call(
        functools.partial(_ffn_prefix_kernel, n_invalid=N_PAD),
        grid=(1,),
        in_specs=[full((1, PREFIX, D_MODEL)), full((1, PREFIX, RET_V)), full((1, PREFIX, FOX_W)),
                  full((D_MODEL, D_MODEL)), full((1, D_MODEL)), full((D_MODEL, D_FF))],
        out_specs=full((PREFIX, D_FF)),
        out_shape=jax.ShapeDtypeStruct((PREFIX, D_FF), F32),
        compiler_params=pltpu.CompilerParams(vmem_limit_bytes=VMEM_LIMIT),
        name="ffn_prefix",
    )(x, o_r, o_f, w_out, g2, w_up)


def _ffn_kernel(x_ref, or_ref, of_ref, wo_ref, g2_ref, wu_ref, cw_ref, cb_ref, wd_ref, gf_ref, halo_ref,
                o_ref, carry_ref, abuf_ref, gated_ref, *, tm):
    @pl.when(pl.program_id(1) == 0)
    def _():
        carry_ref[...] = halo_ref[...]

    h1 = _mixer_residual(x_ref, or_ref, of_ref, wo_ref)
    hn = ((h1 * _rms(h1)) * g2_ref[...]).astype(BF16)
    for c in range(D_FF // FF_CHUNK):
        cols = slice(c * FF_CHUNK, (c + 1) * FF_CHUNK)
        a = jnp.dot(hn, wu_ref[:, cols], preferred_element_type=F32)
        b = jnp.dot(hn, wu_ref[:, D_FF + c * FF_CHUNK:D_FF + (c + 1) * FF_CHUNK], preferred_element_type=F32)
        abuf_ref[0:SUBLANES, :] = carry_ref[:, cols]
        abuf_ref[SUBLANES:SUBLANES + tm, :] = a
        carry_ref[:, cols] = a[tm - SUBLANES:tm, :]
        acc = cb_ref[:, cols] + abuf_ref[SUBLANES - 2:SUBLANES - 2 + tm, :] * cw_ref[0:1, cols]
        acc = acc + abuf_ref[SUBLANES - 1:SUBLANES - 1 + tm, :] * cw_ref[1:2, cols]
        acc = acc + a * cw_ref[2:3, cols]
        gated_ref[:, cols] = (acc * jax.nn.sigmoid(acc) * b).astype(BF16)
    h2 = h1 + jnp.dot(gated_ref[...], wd_ref[...], preferred_element_type=F32)
    o_ref[0] = (h2 * _rms(h2)) * gf_ref[...]


def _ffn(x, o_r, o_f, w_out, g2, w_up, conv_w, conv_b, w_down, gf, halo, *, tm):
    nb, rows, _ = x.shape
    row_blk = lambda w: pl.BlockSpec((1, tm, w), lambda b, j: (b, j, 0))
    const = lambda shape: pl.BlockSpec(shape, lambda b, j: (0,) * len(shape), pipeline_mode=pl.Buffered(1))
    return pl.pallas_call(
        functools.partial(_ffn_kernel, tm=tm),
        grid=(nb, rows // tm),
        in_specs=[row_blk(D_MODEL), row_blk(RET_V), row_blk(FOX_W), const((D_MODEL, D_MODEL)),
                  const((1, D_MODEL)), const((D_MODEL, 2 * D_FF)), const((CONV_W, D_FF)), const((1, D_FF)),
                  const((D_FF, D_MODEL)), const((1, D_MODEL)), const((SUBLANES, D_FF))],
        out_specs=row_blk(D_MODEL),
        out_shape=jax.ShapeDtypeStruct((nb, rows, D_MODEL), F32),
        scratch_shapes=[pltpu.VMEM((SUBLANES, D_FF), F32), pltpu.VMEM((SUBLANES + tm, FF_CHUNK), F32),
                        pltpu.VMEM((tm, D_FF), BF16)],
        compiler_params=pltpu.CompilerParams(dimension_semantics=("arbitrary", "arbitrary"),
                                             vmem_limit_bytes=VMEM_LIMIT),
        name="ffn",
    )(x, o_r, o_f, w_out, g2, w_up, conv_w, conv_b, w_down, gf, halo)


def _rotary_tables(start, length):
    half = RET_DK // 2
    inv = (1.0 / (ROPE_BASE ** (np.arange(half, dtype=np.float32) / np.float32(half)))).astype(np.float32)
    pos = start + np.arange(length)
    ang = pos.astype(np.float32)[:, None] * inv[None, :]
    cos = np.tile(np.cos(ang), (1, LANES // half))
    sin = np.tile(np.concatenate([-np.sin(ang), np.sin(ang)], axis=1), (1, LANES // RET_DK))
    k_scale = (np.float32(RET_DK ** -0.5) * (pos >= N_PAD).astype(np.float32))[:, None]
    return cos, sin, cos * k_scale, sin * k_scale


def _retention_tables(t):
    f32 = np.float32
    log_g = np.log1p(-np.exp2(f32(-5.0) - np.arange(RET_HEADS, dtype=f32)))
    n = np.arange(t, dtype=f32)
    visible = (np.arange(t)[None, :] // CHUNK) <= (np.arange(t)[:, None] // CHUNK)
    d = np.exp(np.abs(n[:, None] - n[None, :])[None] * log_g[:, None, None]) * visible[None].astype(f32)
    lane_log_g = np.repeat(log_g, RET_DK).reshape(RET_HEADS // 2, 1, LANES)
    wq = np.exp((n + f32(1.0))[None, :, None] * lane_log_g)
    wk = np.exp((f32(t) - f32(1.0) - n)[None, :, None] * lane_log_g)
    gt = np.broadcast_to(np.exp(f32(t) * log_g)[:, None, None], (RET_HEADS, 1, RET_DV))
    return tuple(np.ascontiguousarray(a, dtype=f32) for a in (d, wq, wk, gt))


def _gate_lanes(v):
    spread = jnp.stack([v] + [jnp.zeros_like(v)] * (GATE_STRIDE - 1), axis=-1)
    spread = spread.reshape(v.shape[:-1] + (GATE_STRIDE * FOX_HEADS,))
    return jnp.pad(spread, [(0, 0)] * (v.ndim - 1) + [(0, LANES - GATE_STRIDE * FOX_HEADS)])


def kernel(x, meta_tokens, attn_norm_g, w_in, fox_forget_b, ret_norm_g, w_out, ffn_norm_g, w_up, conv_w,
           conv_b, w_down, final_norm_g):
    assert w_in.shape[0] == 1, "single-layer block"
    batch, seq, _ = x.shape
    w_main = w_in[0].astype(BF16)
    w_ff = _gate_lanes(w_in[0, :, MAIN_W:]).astype(BF16)
    fb = _gate_lanes(fox_forget_b[0]).reshape(1, LANES)
    g1 = attn_norm_g[0].reshape(1, D_MODEL)
    g2 = ffn_norm_g[0].reshape(1, D_MODEL)
    gf = final_norm_g.reshape(1, D_MODEL)
    ng = ret_norm_g[0].reshape(1, RET_V)
    cb = conv_b[0].reshape(1, D_FF)
    h_pre = jnp.concatenate([jnp.zeros((N_PAD, D_MODEL), x.dtype), meta_tokens.astype(x.dtype)], axis=0)[None]

    zeros_c = jnp.zeros((SUBLANES, LANES), F32)
    rq0, rk0, rv0, rg0, fq0, fk0, fv0, c_pre = _inproj(h_pre, g1, w_main, w_ff, fb, _rotary_tables(0, PREFIX),
                                                        zeros_c, tl=PREFIX, n_invalid=N_PAD)
    r_zero = jnp.zeros((RET_HEADS, LANES, RET_DV), F32)
    or0, r_pre = _retention(rq0, rk0, rv0, rg0, r_zero, _retention_tables(PREFIX), ng, t=PREFIX)
    of0 = _fox(fq0, fk0, fv0, None, None, tk=PREFIX)

    rq, rk, rv, rg, fq, fk, fv, _, w_out_b, w_up_b, w_down_b = _inproj(
        x, g1, w_main, w_ff, fb, _rotary_tables(PREFIX, seq), c_pre[0], (w_out[0], w_up[0], w_down[0]),
        tl=1024, n_invalid=0)
    a_pre = _ffn_prefix(h_pre, or0, of0, w_out_b, g2, w_up_b)
    halo = a_pre[PREFIX - SUBLANES:]
    o_r, _ = _retention(rq, rk, rv, rg, r_pre[0], _retention_tables(256), ng, t=256)
    o_f = _fox(fq, fk, fv, fk0, fv0, tk=512)
    return _ffn(x, o_r, o_f, w_out_b, g2, w_up_b, conv_w[0], cb, w_down_b, gf, halo, tm=1024)
```

```python
import functools

import jax
import jax.numpy as jnp
import numpy as np
from jax import lax
from jax.experimental import pallas as pl
from jax.experimental.pallas import tpu as pltpu

F32 = jnp.float32
BF16 = jnp.bfloat16

D_MODEL = 1024
CHUNK = 64
N_META = 16
PREFIX = 128
N_PAD = PREFIX - N_META
RET_HEADS = 4
RET_DK = 64
RET_DV = 128
FOX_HEADS = 8
FOX_DH = 64
D_FF = 2816
CONV_W = 3
ROPE_BASE = 10000.0
EPS = 1e-6
NEG = -1e30

RET_QK = RET_HEADS * RET_DK
RET_V = RET_HEADS * RET_DV
FOX_W = FOX_HEADS * FOX_DH
MAIN_W = 2 * RET_QK + 2 * RET_V + 3 * FOX_W
LANES = 128
SUBLANES = 8
N_EXTRA = 7
GATE_STRIDE = 3
LOG2E = 1.4426950408889634
CUMSUM_ROWS = 256
VMEM_LIMIT = 56 * 1024 * 1024
FF_CHUNK = 256
RET_UNROLL = 8


def _rms(x):
    return lax.rsqrt(jnp.mean(x * x, axis=-1, keepdims=True) + EPS)


def _split3(c):
    hi = c.astype(BF16).astype(F32)
    r = c - hi
    mid = r.astype(BF16).astype(F32)
    return hi, mid, r - mid


def _pack3(c, gate_lane):
    hi, mid, lo = (jnp.where(gate_lane, part, 0.0) for part in _split3(c))
    return hi + pltpu.roll(mid, 1, 1) + pltpu.roll(lo, 2, 1)


def _inproj_kernel(x_ref, g_ref, wm_ref, wf_ref, fb_ref, cq_ref, sq_ref, ck_ref, sk_ref, c0_ref,
                   *rest, tl, n_invalid, n_cast):
    cast_in, outs, cast_out, carry_ref = (rest[:n_cast], rest[n_cast:n_cast + 8],
                                          rest[n_cast + 8:2 * n_cast + 8], rest[-1])
    rq_ref, rk_ref, rv_ref, rg_ref, fq_ref, fk_ref, fv_ref, cl_ref = outs
    j = pl.program_id(1)

    @pl.when(j == 0)
    def _():
        carry_ref[...] = c0_ref[...]

    x = x_ref[0]
    hn = ((x * _rms(x)) * g_ref[...]).astype(BF16)

    def proj(lo, width):
        return jnp.dot(hn, wm_ref[:, lo:lo + width], preferred_element_type=F32)

    lane = lax.broadcasted_iota(jnp.int32, (1, LANES), 1)
    first_half = (lane & 32) == 0

    gate_lane = functools.reduce(jnp.logical_or, [lane == GATE_STRIDE * h for h in range(FOX_HEADS)])
    ff = jnp.dot(hn, wf_ref[...], preferred_element_type=F32) + fb_ref[...]
    lf = jnp.minimum(ff, 0.0) - jnp.log(1.0 + jnp.exp(-jnp.abs(ff)))
    packed = _pack3(lf, gate_lane).astype(BF16)
    sub = min(tl, CUMSUM_ROWS)
    tri = lax.broadcasted_iota(jnp.int32, (sub, sub), 0) >= lax.broadcasted_iota(jnp.int32, (sub, sub), 1)
    tri = jnp.where(tri, 1.0, 0.0).astype(BF16)
    carry = carry_ref[0:1, :]
    c_blocks = []
    for r in range(tl // sub):
        c3 = jnp.dot(tri, packed[r * sub:(r + 1) * sub], preferred_element_type=F32)
        c_blk = (c3 + pltpu.roll(c3, LANES - 1, 1)) + pltpu.roll(c3, LANES - 2, 1) + carry
        carry = c_blk[sub - 1:sub, :]
        c_blocks.append(c_blk)
    c = jnp.concatenate(c_blocks, axis=0) if len(c_blocks) > 1 else c_blocks[0]
    carry_ref[...] = jnp.broadcast_to(carry, (SUBLANES, LANES))
    cl_ref[0] = carry_ref[...]

    c_triples = _pack3(c * LOG2E, gate_lane)
    neg_triples = -c_triples
    if n_invalid:
        row = j * tl + lax.broadcasted_iota(jnp.int32, (tl, LANES), 0)
        key_mask = jnp.where(row >= n_invalid, 0.0, NEG)
    fox_base = 2 * RET_QK + 2 * RET_V
    fq_all = proj(fox_base, FOX_W) * (FOX_DH ** -0.5 * LOG2E)
    fk_all = proj(fox_base + FOX_W, FOX_W)
    fv_all = proj(fox_base + 2 * FOX_W, FOX_W)
    for pair in range(FOX_HEADS // 2):
        cols = slice(LANES * pair, LANES * (pair + 1))
        fq, fk, fv = fq_all[:, cols], fk_all[:, cols], fv_all[:, cols]
        for hh in range(2):
            h = 2 * pair + hh
            is_data = (lane < FOX_DH) if hh == 0 else (lane >= FOX_DH)
            e = (lane - FOX_DH) if hh == 0 else lane
            e0 = FOX_DH if hh == 0 else 0
            c_lanes = (e >= 0) & (e < 3)
            negc_lanes = (e >= 3) & (e < 6)
            qx = jnp.where(c_lanes, pltpu.roll(c_triples, (e0 - GATE_STRIDE * h) % LANES, 1),
                           jnp.where((e >= 3) & (e < N_EXTRA), 1.0, 0.0))
            kx = jnp.where(negc_lanes, pltpu.roll(neg_triples, (e0 + 3 - GATE_STRIDE * h) % LANES, 1),
                           jnp.where(c_lanes, 1.0, 0.0))
            if n_invalid:
                kx = jnp.where(e == 6, key_mask, kx)
            vx = jnp.where(e == 0, 1.0, 0.0)
            fq_ref[0, h] = jnp.where(is_data, fq, qx).astype(BF16)
            fk_ref[0, h] = jnp.where(is_data, fk, kx).astype(BF16)
            fv_ref[0, h] = jnp.where(is_data, fv, vx).astype(BF16)

    rqk = proj(0, 2 * RET_QK)
    for base, c_ref, s_ref, o_ref in ((0, cq_ref, sq_ref, rq_ref), (RET_QK, ck_ref, sk_ref, rk_ref)):
        for blk in range(RET_QK // LANES):
            xb = rqk[:, base + LANES * blk:base + LANES * (blk + 1)]
            swapped = jnp.where(first_half, pltpu.roll(xb, LANES - 32, 1), pltpu.roll(xb, 32, 1))
            o_ref[0, :, LANES * blk:LANES * (blk + 1)] = (xb * c_ref[...] + swapped * s_ref[...]).astype(BF16)

    rv_ref[0] = proj(2 * RET_QK, RET_V).astype(BF16)
    rg = proj(2 * RET_QK + RET_V, RET_V)
    rg_ref[0] = (rg * jax.nn.sigmoid(rg)).astype(BF16)

    for src, dst in zip(cast_in, cast_out):
        dst[...] = src[...].astype(BF16)


def _inproj(x, g, w_main, w_ff, fb, tabs, c0, to_bf16=(), *, tl, n_invalid):
    nb, rows, _ = x.shape
    grid = (nb, rows // tl)
    n_steps = nb * (rows // tl)

    def slab(a):
        n = n_steps
        while a.shape[0] % (n * 2 * SUBLANES):
            n //= 2
        return pl.BlockSpec((a.shape[0] // n, a.shape[1]),
                            lambda b, j: ((b * (rows // tl) + j) // (n_steps // n), 0))

    cast_specs = [slab(a) for a in to_bf16]
    row_blk = lambda w: pl.BlockSpec((1, tl, w), lambda b, j: (b, j, 0))
    head_blk = pl.BlockSpec((1, FOX_HEADS, tl, LANES), lambda b, j: (b, 0, j, 0))
    const = lambda shape: pl.BlockSpec(shape, lambda b, j: (0,) * len(shape), pipeline_mode=pl.Buffered(1))
    tab_blk = pl.BlockSpec((tl, LANES), lambda b, j: (j, 0))
    out_shape = (
        jax.ShapeDtypeStruct((nb, rows, RET_QK), BF16), jax.ShapeDtypeStruct((nb, rows, RET_QK), BF16),
        jax.ShapeDtypeStruct((nb, rows, RET_V), BF16), jax.ShapeDtypeStruct((nb, rows, RET_V), BF16),
        jax.ShapeDtypeStruct((nb, FOX_HEADS, rows, LANES), BF16),
        jax.ShapeDtypeStruct((nb, FOX_HEADS, rows, LANES), BF16),
        jax.ShapeDtypeStruct((nb, FOX_HEADS, rows, LANES), BF16),
        jax.ShapeDtypeStruct((nb, SUBLANES, LANES), F32),
    )
    return pl.pallas_call(
        functools.partial(_inproj_kernel, tl=tl, n_invalid=n_invalid, n_cast=len(to_bf16)),
        grid=grid,
        in_specs=[row_blk(D_MODEL), const((1, D_MODEL)), const((D_MODEL, MAIN_W)), const((D_MODEL, LANES)),
                  const((1, LANES)), tab_blk, tab_blk, tab_blk, tab_blk, const((SUBLANES, LANES))] + cast_specs,
        out_specs=(row_blk(RET_QK), row_blk(RET_QK), row_blk(RET_V), row_blk(RET_V),
                   head_blk, head_blk, head_blk, pl.BlockSpec((1, SUBLANES, LANES), lambda b, j: (b, 0, 0)))
        + tuple(cast_specs),
        out_shape=out_shape + tuple(jax.ShapeDtypeStruct(a.shape, BF16) for a in to_bf16),
        scratch_shapes=[pltpu.VMEM((SUBLANES, LANES), F32)],
        compiler_params=pltpu.CompilerParams(dimension_semantics=("arbitrary", "arbitrary"),
                                             vmem_limit_bytes=VMEM_LIMIT),
        name=f"inproj_{rows}",
    )(x, g, w_main, w_ff, fb, *tabs, c0, *to_bf16)


def _retention_kernel(q_ref, k_ref, v_ref, gate_ref, r0_ref, d_ref, wq_ref, wk_ref, gt_ref, ng_ref,
                      o_ref, rf_ref, state_ref, *, t, n_blocks):
    state_ref[...] = r0_ref[...]
    lane = lax.broadcasted_iota(jnp.int32, (t, LANES), 1)

    def block(i, carry):
        rows = pl.ds(pl.multiple_of(i * t, t), t)
        q = q_ref[0, rows, :]
        k = k_ref[0, rows, :]
        v = v_ref[0, rows, :]
        qw = (q.astype(F32) * wq_ref[0]).astype(BF16)
        kw = (k.astype(F32) * wk_ref[0]).astype(BF16)
        for hh in range(2):
            in_head = (lane < RET_DK) if hh == 0 else (lane >= RET_DK)
            vh = v[:, RET_DV * hh:RET_DV * (hh + 1)]
            qm = jnp.where(in_head, q, jnp.zeros_like(q))
            s = lax.dot_general(qm, k, (((1,), (1,)), ((), ())), preferred_element_type=F32)
            intra = jnp.dot((s * d_ref[hh]).astype(BF16), vh, preferred_element_type=F32)
            r = state_ref[hh]
            inter = jnp.dot(qw, r.astype(BF16), preferred_element_type=F32)
            kwm = jnp.where(in_head, kw, jnp.zeros_like(kw))
            u = lax.dot_general(kwm, vh, (((0,), (0,)), ((), ())), preferred_element_type=F32)
            state_ref[hh] = gt_ref[hh] * r + u
            o = intra + inter
            cols = slice(RET_DV * hh, RET_DV * (hh + 1))
            o = (o * _rms(o)) * ng_ref[:, cols] * gate_ref[0, rows, cols].astype(F32)
            o_ref[0, rows, cols] = o.astype(BF16)
        return carry

    lax.fori_loop(0, n_blocks, block, 0, unroll=min(n_blocks, RET_UNROLL))
    rf_ref[0] = state_ref[...]


def _retention(rq, rk, rv, gate, r0, tabs, norm_g, *, t):
    nb, rows, _ = rq.shape
    d_tab, wq_tab, wk_tab, gt_tab = tabs
    n_pairs = RET_HEADS // 2
    seq = lambda w: pl.BlockSpec((1, rows, w), lambda b, p: (b, 0, p))
    return pl.pallas_call(
        functools.partial(_retention_kernel, t=t, n_blocks=rows // t),
        grid=(nb, n_pairs),
        in_specs=[seq(LANES), seq(LANES), seq(2 * RET_DV), seq(2 * RET_DV),
                  pl.BlockSpec((2, LANES, RET_DV), lambda b, p: (p, 0, 0)),
                  pl.BlockSpec((2, t, t), lambda b, p: (p, 0, 0)),
                  pl.BlockSpec((1, t, LANES), lambda b, p: (p, 0, 0)),
                  pl.BlockSpec((1, t, LANES), lambda b, p: (p, 0, 0)),
                  pl.BlockSpec((2, 1, RET_DV), lambda b, p: (p, 0, 0)),
                  pl.BlockSpec((1, 2 * RET_DV), lambda b, p: (0, p))],
        out_specs=(seq(2 * RET_DV), pl.BlockSpec((1, 2, LANES, RET_DV), lambda b, p: (b, p, 0, 0))),
        out_shape=(jax.ShapeDtypeStruct((nb, rows, RET_V), BF16),
                   jax.ShapeDtypeStruct((nb, RET_HEADS, LANES, RET_DV), F32)),
        scratch_shapes=[pltpu.VMEM((2, LANES, RET_DV), F32)],
        compiler_params=pltpu.CompilerParams(dimension_semantics=("arbitrary", "arbitrary"),
                                             vmem_limit_bytes=VMEM_LIMIT),
        name=f"retention_{rows}",
    )(rq, rk, rv, gate, r0, d_tab, wq_tab, wk_tab, gt_tab, norm_g)


def _fox_kernel(*refs, rows, tk, has_prefix):
    if has_prefix:
        q_ref, k_ref, v_ref, kp_ref, vp_ref, o_ref, m_ref, acc_ref = refs
    else:
        q_ref, k_ref, v_ref, o_ref, m_ref, acc_ref = refs
    lane = lax.broadcasted_iota(jnp.int32, (rows, LANES), 1)
    causal = lax.broadcasted_iota(jnp.int32, (tk, tk), 1) <= lax.broadcasted_iota(jnp.int32, (tk, tk), 0)

    def scores(hh, lo, k):
        return lax.dot_general(q_ref[0, hh, lo:rows, :], k, (((1,), (1,)), ((), ())), preferred_element_type=F32)

    def probs(s, m):
        return jnp.exp2((s - jnp.tile(m, (1, s.shape[1] // LANES))).astype(BF16))

    if has_prefix:
        for hh in range(2):
            s_pre = scores(hh, 0, kp_ref[0, hh])
            m = jnp.broadcast_to(jnp.max(s_pre, axis=1, keepdims=True), (rows, LANES))
            m_ref[hh] = m
            acc_ref[hh] = jnp.dot(probs(s_pre, m), vp_ref[0, hh], preferred_element_type=F32)
    else:
        m_ref[...] = jnp.full(m_ref.shape, NEG, F32)
        acc_ref[...] = jnp.zeros(acc_ref.shape, F32)

    for c in range(rows // tk):
        lo = c * tk
        for hh in range(2):
            s = scores(hh, lo, k_ref[0, hh, lo:lo + tk, :])
            s_top = jnp.where(causal, s[:tk], NEG)
            s = jnp.concatenate([s_top, s[tk:]], axis=0) if lo + tk < rows else s_top
            m_old = m_ref[hh, lo:rows]
            m_new = jnp.maximum(m_old, jnp.max(s, axis=1, keepdims=True))
            m_ref[hh, lo:rows] = m_new
            pv = jnp.dot(probs(s, m_new), v_ref[0, hh, lo:lo + tk, :], preferred_element_type=F32)
            acc_ref[hh, lo:rows] = jnp.exp2(m_old - m_new) * acc_ref[hh, lo:rows] + pv

    outs = []
    for hh in range(2):
        acc = acc_ref[hh]
        sum_lane = FOX_DH if hh == 0 else 0
        outs.append(acc / acc[:, sum_lane:sum_lane + 1])
    o_ref[0] = jnp.where(lane < FOX_DH, outs[0], outs[1]).astype(BF16)


def _fox(fq, fk, fv, kp, vp, *, tk):
    nb, _, rows, _ = fq.shape
    has_prefix = kp is not None
    pair_blk = pl.BlockSpec((1, 2, rows, LANES), lambda b, p: (b, p, 0, 0))
    in_specs = [pair_blk, pair_blk, pair_blk]
    args = [fq, fk, fv]
    if has_prefix:
        pre_blk = pl.BlockSpec((1, 2, PREFIX, LANES), lambda b, p: (0, p, 0, 0))
        in_specs += [pre_blk, pre_blk]
        args += [kp, vp]
    return pl.pallas_call(
        functools.partial(_fox_kernel, rows=rows, tk=tk, has_prefix=has_prefix),
        grid=(nb, FOX_HEADS // 2),
        in_specs=in_specs,
        out_specs=pl.BlockSpec((1, rows, LANES), lambda b, p: (b, 0, p)),
        out_shape=jax.ShapeDtypeStruct((nb, rows, FOX_W), BF16),
        scratch_shapes=[pltpu.VMEM((2, rows, LANES), F32), pltpu.VMEM((2, rows, LANES), F32)],
        compiler_params=pltpu.CompilerParams(dimension_semantics=("arbitrary", "arbitrary"),
                                             vmem_limit_bytes=VMEM_LIMIT),
        name=f"fox_{rows}",
    )(*args)


def _mixer_residual(x_ref, or_ref, of_ref, wo_ref):
    y = jnp.dot(or_ref[0], wo_ref[0:RET_V, :], preferred_element_type=F32)
    y = y + jnp.dot(of_ref[0], wo_ref[RET_V:RET_V + FOX_W, :], preferred_element_type=F32)
    return x_ref[0] + y


def _ffn_prefix_kernel(x_ref, or_ref, of_ref, wo_ref, g2_ref, wa_ref, a_ref, *, n_invalid):
    h1 = _mixer_residual(x_ref, or_ref, of_ref, wo_ref)
    hn = ((h1 * _rms(h1)) * g2_ref[...]).astype(BF16)
    a = jnp.dot(hn, wa_ref[...], preferred_element_type=F32)
    row = lax.broadcasted_iota(jnp.int32, a.shape, 0)
    a_ref[...] = jnp.where(row >= n_invalid, a, 0.0)


def _ffn_prefix(x, o_r, o_f, w_out, g2, w_up):
    full = lambda shape: pl.BlockSpec(shape, lambda i: (0,) * len(shape))
    return pl.pallas_call(
        functools.partial(_ffn_prefix_kernel, n_invalid=N_PAD),
        grid=(1,),
        in_specs=[full((1, PREFIX, D_MODEL)), full((1, PREFIX, RET_V)), full((1, PREFIX, FOX_W)),
                  full((D_MODEL, D_MODEL)), full((1, D_MODEL)), full((D_MODEL, D_FF))],
        out_specs=full((PREFIX, D_FF)),
        out_shape=jax.ShapeDtypeStruct((PREFIX, D_FF), F32),
        compiler_params=pltpu.CompilerParams(vmem_limit_bytes=VMEM_LIMIT),
        name="ffn_prefix",
    )(x, o_r, o_f, w_out, g2, w_up)


def _ffn_kernel(x_ref, or_ref, of_ref, wo_ref, g2_ref, wu_ref, cw_ref, cb_ref, wd_ref, gf_ref, halo_ref,
                o_ref, carry_ref, abuf_ref, gated_ref, *, tm):
    @pl.when(pl.program_id(1) == 0)
    def _():
        carry_ref[...] = halo_ref[...]

    h1 = _mixer_residual(x_ref, or_ref, of_ref, wo_ref)
    hn = ((h1 * _rms(h1)) * g2_ref[...]).astype(BF16)
    for c in range(D_FF // FF_CHUNK):
        cols = slice(c * FF_CHUNK, (c + 1) * FF_CHUNK)
        a = jnp.dot(hn, wu_ref[:, cols], preferred_element_type=F32)
        b = jnp.dot(hn, wu_ref[:, D_FF + c * FF_CHUNK:D_FF + (c + 1) * FF_CHUNK], preferred_element_type=F32)
        abuf_ref[0:SUBLANES, :] = carry_ref[:, cols]
        abuf_ref[SUBLANES:SUBLANES + tm, :] = a
        carry_ref[:, cols] = a[tm - SUBLANES:tm, :]
        acc = cb_ref[:, cols] + abuf_ref[SUBLANES - 2:SUBLANES - 2 + tm, :] * cw_ref[0:1, cols]
        acc = acc + abuf_ref[SUBLANES - 1:SUBLANES - 1 + tm, :] * cw_ref[1:2, cols]
        acc = acc + a * cw_ref[2:3, cols]
        gated_ref[:, cols] = (acc * jax.nn.sigmoid(acc) * b).astype(BF16)
    h2 = h1 + jnp.dot(gated_ref[...], wd_ref[...], preferred_element_type=F32)
    o_ref[0] = (h2 * _rms(h2)) * gf_ref[...]


def _ffn(x, o_r, o_f, w_out, g2, w_up, conv_w, conv_b, w_down, gf, halo, *, tm):
    nb, rows, _ = x.shape
    row_blk = lambda w: pl.BlockSpec((1, tm, w), lambda b, j: (b, j, 0))
    const = lambda shape: pl.BlockSpec(shape, lambda b, j: (0,) * len(shape), pipeline_mode=pl.Buffered(1))
    return pl.pallas_call(
        functools.partial(_ffn_kernel, tm=tm),
        grid=(nb, rows // tm),
        in_specs=[row_blk(D_MODEL), row_blk(RET_V), row_blk(FOX_W), const((D_MODEL, D_MODEL)),
                  const((1, D_MODEL)), const((D_MODEL, 2 * D_FF)), const((CONV_W, D_FF)), const((1, D_FF)),
                  const((D_FF, D_MODEL)), const((1, D_MODEL)), const((SUBLANES, D_FF))],
        out_specs=row_blk(D_MODEL),
        out_shape=jax.ShapeDtypeStruct((nb, rows, D_MODEL), F32),
        scratch_shapes=[pltpu.VMEM((SUBLANES, D_FF), F32), pltpu.VMEM((SUBLANES + tm, FF_CHUNK), F32),
                        pltpu.VMEM((tm, D_FF), BF16)],
        compiler_params=pltpu.CompilerParams(dimension_semantics=("arbitrary", "arbitrary"),
                                             vmem_limit_bytes=VMEM_LIMIT),
        name="ffn",
    )(x, o_r, o_f, w_out, g2, w_up, conv_w, conv_b, w_down, gf, halo)


def _rotary_tables(start, length):
    half = RET_DK // 2
    inv = (1.0 / (ROPE_BASE ** (np.arange(half, dtype=np.float32) / np.float32(half)))).astype(np.float32)
    pos = start + np.arange(length)
    ang = pos.astype(np.float32)[:, None] * inv[None, :]
    cos = np.tile(np.cos(ang), (1, LANES // half))
    sin = np.tile(np.concatenate([-np.sin(ang), np.sin(ang)], axis=1), (1, LANES // RET_DK))
    k_scale = (np.float32(RET_DK ** -0.5) * (pos >= N_PAD).astype(np.float32))[:, None]
    return cos, sin, cos * k_scale, sin * k_scale


def _retention_tables(t):
    f32 = np.float32
    log_g = np.log1p(-np.exp2(f32(-5.0) - np.arange(RET_HEADS, dtype=f32)))
    n = np.arange(t, dtype=f32)
    visible = (np.arange(t)[None, :] // CHUNK) <= (np.arange(t)[:, None] // CHUNK)
    d = np.exp(np.abs(n[:, None] - n[None, :])[None] * log_g[:, None, None]) * visible[None].astype(f32)
    lane_log_g = np.repeat(log_g, RET_DK).reshape(RET_HEADS // 2, 1, LANES)
    wq = np.exp((n + f32(1.0))[None, :, None] * lane_log_g)
    wk = np.exp((f32(t) - f32(1.0) - n)[None, :, None] * lane_log_g)
    gt = np.broadcast_to(np.exp(f32(t) * log_g)[:, None, None], (RET_HEADS, 1, RET_DV))
    return tuple(np.ascontiguousarray(a, dtype=f32) for a in (d, wq, wk, gt))


def _gate_lanes(v):
    spread = jnp.stack([v] + [jnp.zeros_like(v)] * (GATE_STRIDE - 1), axis=-1)
    spread = spread.reshape(v.shape[:-1] + (GATE_STRIDE * FOX_HEADS,))
    return jnp.pad(spread, [(0, 0)] * (v.ndim - 1) + [(0, LANES - GATE_STRIDE * FOX_HEADS)])


def kernel(x, meta_tokens, attn_norm_g, w_in, fox_forget_b, ret_norm_g, w_out, ffn_norm_g, w_up, conv_w,
           conv_b, w_down, final_norm_g):
    assert w_in.shape[0] == 1, "single-layer block"
    batch, seq, _ = x.shape
    w_main = w_in[0].astype(BF16)
    w_ff = _gate_lanes(w_in[0, :, MAIN_W:]).astype(BF16)
    fb = _gate_lanes(fox_forget_b[0]).reshape(1, LANES)
    g1 = attn_norm_g[0].reshape(1, D_MODEL)
    g2 = ffn_norm_g[0].reshape(1, D_MODEL)
    gf = final_norm_g.reshape(1, D_MODEL)
    ng = ret_norm_g[0].reshape(1, RET_V)
    cb = conv_b[0].reshape(1, D_FF)
    h_pre = jnp.concatenate([jnp.zeros((N_PAD, D_MODEL), x.dtype), meta_tokens.astype(x.dtype)], axis=0)[None]

    zeros_c = jnp.zeros((SUBLANES, LANES), F32)
    rq0, rk0, rv0, rg0, fq0, fk0, fv0, c_pre = _inproj(h_pre, g1, w_main, w_ff, fb, _rotary_tables(0, PREFIX),
                                                        zeros_c, tl=PREFIX, n_invalid=N_PAD)
    r_zero = jnp.zeros((RET_HEADS, LANES, RET_DV), F32)
    or0, r_pre = _retention(rq0, rk0, rv0, rg0, r_zero, _retention_tables(PREFIX), ng, t=PREFIX)
    of0 = _fox(fq0, fk0, fv0, None, None, tk=PREFIX)

    rq, rk, rv, rg, fq, fk, fv, _, w_out_b, w_up_b, w_down_b = _inproj(
        x, g1, w_main, w_ff, fb, _rotary_tables(PREFIX, seq), c_pre[0], (w_out[0], w_up[0], w_down[0]),
        tl=1024, n_invalid=0)
    a_pre = _ffn_prefix(h_pre, or0, of0, w_out_b, g2, w_up_b)
    halo = a_pre[PREFIX - SUBLANES:]
    o_r, _ = _retention(rq, rk, rv, rg, r_pre[0], _retention_tables(256), ng, t=256)
    o_f = _fox(fq, fk, fv, fk0, fv0, tk=512)
    return _ffn(x, o_r, o_f, w_out_b, g2, w_up_b, conv_w[0], cb, w_down_b, gf, halo, tm=1024)
```

```python
import functools

import jax
import jax.numpy as jnp
import numpy as np
from jax import lax
from jax.experimental import pallas as pl
from jax.experimental.pallas import tpu as pltpu

F32 = jnp.float32
BF16 = jnp.bfloat16

D_MODEL = 1024
CHUNK = 64
N_META = 16
PREFIX = 128
N_PAD = PREFIX - N_META
RET_HEADS = 4
RET_DK = 64
RET_DV = 128
FOX_HEADS = 8
FOX_DH = 64
D_FF = 2816
CONV_W = 3
ROPE_BASE = 10000.0
EPS = 1e-6
NEG = -1e30

RET_QK = RET_HEADS * RET_DK
RET_V = RET_HEADS * RET_DV
FOX_W = FOX_HEADS * FOX_DH
MAIN_W = 2 * RET_QK + 2 * RET_V + 3 * FOX_W
LANES = 128
SUBLANES = 8
N_EXTRA = 7
GATE_STRIDE = 3
LOG2E = 1.4426950408889634
CUMSUM_ROWS = 256
VMEM_LIMIT = 56 * 1024 * 1024
FF_CHUNK = 256
RET_UNROLL = 16


def _rms(x):
    return lax.rsqrt(jnp.mean(x * x, axis=-1, keepdims=True) + EPS)


def _split3(c):
    hi = c.astype(BF16).astype(F32)
    r = c - hi
    mid = r.astype(BF16).astype(F32)
    return hi, mid, r - mid


def _pack3(c, gate_lane):
    hi, mid, lo = (jnp.where(gate_lane, part, 0.0) for part in _split3(c))
    return hi + pltpu.roll(mid, 1, 1) + pltpu.roll(lo, 2, 1)


def _inproj_kernel(x_ref, g_ref, wm_ref, wf_ref, fb_ref, cq_ref, sq_ref, ck_ref, sk_ref, c0_ref,
                   *rest, tl, n_invalid, n_cast):
    cast_in, outs, cast_out, carry_ref = (rest[:n_cast], rest[n_cast:n_cast + 8],
                                          rest[n_cast + 8:2 * n_cast + 8], rest[-1])
    rq_ref, rk_ref, rv_ref, rg_ref, fq_ref, fk_ref, fv_ref, cl_ref = outs
    j = pl.program_id(1)

    @pl.when(j == 0)
    def _():
        carry_ref[...] = c0_ref[...]

    x = x_ref[0]
    hn = ((x * _rms(x)) * g_ref[...]).astype(BF16)

    def proj(lo, width):
        return jnp.dot(hn, wm_ref[:, lo:lo + width], preferred_element_type=F32)

    lane = lax.broadcasted_iota(jnp.int32, (1, LANES), 1)
    first_half = (lane & 32) == 0

    gate_lane = functools.reduce(jnp.logical_or, [lane == GATE_STRIDE * h for h in range(FOX_HEADS)])
    ff = jnp.dot(hn, wf_ref[...], preferred_element_type=F32) + fb_ref[...]
    lf = jnp.minimum(ff, 0.0) - jnp.log(1.0 + jnp.exp(-jnp.abs(ff)))
    packed = _pack3(lf, gate_lane).astype(BF16)
    sub = min(tl, CUMSUM_ROWS)
    tri = lax.broadcasted_iota(jnp.int32, (sub, sub), 0) >= lax.broadcasted_iota(jnp.int32, (sub, sub), 1)
    tri = jnp.where(tri, 1.0, 0.0).astype(BF16)
    carry = carry_ref[0:1, :]
    c_blocks = []
    for r in range(tl // sub):
        c3 = jnp.dot(tri, packed[r * sub:(r + 1) * sub], preferred_element_type=F32)
        c_blk = (c3 + pltpu.roll(c3, LANES - 1, 1)) + pltpu.roll(c3, LANES - 2, 1) + carry
        carry = c_blk[sub - 1:sub, :]
        c_blocks.append(c_blk)
    c = jnp.concatenate(c_blocks, axis=0) if len(c_blocks) > 1 else c_blocks[0]
    carry_ref[...] = jnp.broadcast_to(carry, (SUBLANES, LANES))
    cl_ref[0] = carry_ref[...]

    c_triples = _pack3(c * LOG2E, gate_lane)
    neg_triples = -c_triples
    if n_invalid:
        row = j * tl + lax.broadcasted_iota(jnp.int32, (tl, LANES), 0)
        key_mask = jnp.where(row >= n_invalid, 0.0, NEG)
    fox_base = 2 * RET_QK + 2 * RET_V
    fq_all = proj(fox_base, FOX_W) * (FOX_DH ** -0.5 * LOG2E)
    fk_all = proj(fox_base + FOX_W, FOX_W)
    fv_all = proj(fox_base + 2 * FOX_W, FOX_W)
    for pair in range(FOX_HEADS // 2):
        cols = slice(LANES * pair, LANES * (pair + 1))
        fq, fk, fv = fq_all[:, cols], fk_all[:, cols], fv_all[:, cols]
        for hh in range(2):
            h = 2 * pair + hh
            is_data = (lane < FOX_DH) if hh == 0 else (lane >= FOX_DH)
            e = (lane - FOX_DH) if hh == 0 else lane
            e0 = FOX_DH if hh == 0 else 0
            c_lanes = (e >= 0) & (e < 3)
            negc_lanes = (e >= 3) & (e < 6)
            qx = jnp.where(c_lanes, pltpu.roll(c_triples, (e0 - GATE_STRIDE * h) % LANES, 1),
                           jnp.where((e >= 3) & (e < N_EXTRA), 1.0, 0.0))
            kx = jnp.where(negc_lanes, pltpu.roll(neg_triples, (e0 + 3 - GATE_STRIDE * h) % LANES, 1),
                           jnp.where(c_lanes, 1.0, 0.0))
            if n_invalid:
                kx = jnp.where(e == 6, key_mask, kx)
            vx = jnp.where(e == 0, 1.0, 0.0)
            fq_ref[0, h] = jnp.where(is_data, fq, qx).astype(BF16)
            fk_ref[0, h] = jnp.where(is_data, fk, kx).astype(BF16)
            fv_ref[0, h] = jnp.where(is_data, fv, vx).astype(BF16)

    rqk = proj(0, 2 * RET_QK)
    for base, c_ref, s_ref, o_ref in ((0, cq_ref, sq_ref, rq_ref), (RET_QK, ck_ref, sk_ref, rk_ref)):
        for blk in range(RET_QK // LANES):
            xb = rqk[:, base + LANES * blk:base + LANES * (blk + 1)]
            swapped = jnp.where(first_half, pltpu.roll(xb, LANES - 32, 1), pltpu.roll(xb, 32, 1))
            o_ref[0, :, LANES * blk:LANES * (blk + 1)] = (xb * c_ref[...] + swapped * s_ref[...]).astype(BF16)

    rv_ref[0] = proj(2 * RET_QK, RET_V).astype(BF16)
    rg = proj(2 * RET_QK + RET_V, RET_V)
    rg_ref[0] = (rg * jax.nn.sigmoid(rg)).astype(BF16)

    for src, dst in zip(cast_in, cast_out):
        dst[...] = src[...].astype(BF16)


def _inproj(x, g, w_main, w_ff, fb, tabs, c0, to_bf16=(), *, tl, n_invalid):
    nb, rows, _ = x.shape
    grid = (nb, rows // tl)
    n_steps = nb * (rows // tl)

    def slab(a):
        n = n_steps
        while a.shape[0] % (n * 2 * SUBLANES):
            n //= 2
        return pl.BlockSpec((a.shape[0] // n, a.shape[1]),
                            lambda b, j: ((b * (rows // tl) + j) // (n_steps // n), 0))

    cast_specs = [slab(a) for a in to_bf16]
    row_blk = lambda w: pl.BlockSpec((1, tl, w), lambda b, j: (b, j, 0))
    head_blk = pl.BlockSpec((1, FOX_HEADS, tl, LANES), lambda b, j: (b, 0, j, 0))
    const = lambda shape: pl.BlockSpec(shape, lambda b, j: (0,) * len(shape), pipeline_mode=pl.Buffered(1))
    tab_blk = pl.BlockSpec((tl, LANES), lambda b, j: (j, 0))
    out_shape = (
        jax.ShapeDtypeStruct((nb, rows, RET_QK), BF16), jax.ShapeDtypeStruct((nb, rows, RET_QK), BF16),
        jax.ShapeDtypeStruct((nb, rows, RET_V), BF16), jax.ShapeDtypeStruct((nb, rows, RET_V), BF16),
        jax.ShapeDtypeStruct((nb, FOX_HEADS, rows, LANES), BF16),
        jax.ShapeDtypeStruct((nb, FOX_HEADS, rows, LANES), BF16),
        jax.ShapeDtypeStruct((nb, FOX_HEADS, rows, LANES), BF16),
        jax.ShapeDtypeStruct((nb, SUBLANES, LANES), F32),
    )
    return pl.pallas_call(
        functools.partial(_inproj_kernel, tl=tl, n_invalid=n_invalid, n_cast=len(to_bf16)),
        grid=grid,
        in_specs=[row_blk(D_MODEL), const((1, D_MODEL)), const((D_MODEL, MAIN_W)), const((D_MODEL, LANES)),
                  const((1, LANES)), tab_blk, tab_blk, tab_blk, tab_blk, const((SUBLANES, LANES))] + cast_specs,
        out_specs=(row_blk(RET_QK), row_blk(RET_QK), row_blk(RET_V), row_blk(RET_V),
                   head_blk, head_blk, head_blk, pl.BlockSpec((1, SUBLANES, LANES), lambda b, j: (b, 0, 0)))
        + tuple(cast_specs),
        out_shape=out_shape + tuple(jax.ShapeDtypeStruct(a.shape, BF16) for a in to_bf16),
        scratch_shapes=[pltpu.VMEM((SUBLANES, LANES), F32)],
        compiler_params=pltpu.CompilerParams(dimension_semantics=("arbitrary", "arbitrary"),
                                             vmem_limit_bytes=VMEM_LIMIT),
        name=f"inproj_{rows}",
    )(x, g, w_main, w_ff, fb, *tabs, c0, *to_bf16)


def _retention_kernel(q_ref, k_ref, v_ref, gate_ref, r0_ref, d_ref, wq_ref, wk_ref, gt_ref, ng_ref,
                      o_ref, rf_ref, state_ref, *, t, n_blocks):
    state_ref[...] = r0_ref[...]
    lane = lax.broadcasted_iota(jnp.int32, (t, LANES), 1)

    def block(i, carry):
        rows = pl.ds(pl.multiple_of(i * t, t), t)
        q = q_ref[0, rows, :]
        k = k_ref[0, rows, :]
        v = v_ref[0, rows, :]
        qw = (q.astype(F32) * wq_ref[0]).astype(BF16)
        kw = (k.astype(F32) * wk_ref[0]).astype(BF16)
        for hh in range(2):
            in_head = (lane < RET_DK) if hh == 0 else (lane >= RET_DK)
            vh = v[:, RET_DV * hh:RET_DV * (hh + 1)]
            qm = jnp.where(in_head, q, jnp.zeros_like(q))
            s = lax.dot_general(qm, k, (((1,), (1,)), ((), ())), preferred_element_type=F32)
            intra = jnp.dot((s * d_ref[hh]).astype(BF16), vh, preferred_element_type=F32)
            r = state_ref[hh]
            inter = jnp.dot(qw, r.astype(BF16), preferred_element_type=F32)
            kwm = jnp.where(in_head, kw, jnp.zeros_like(kw))
            u = lax.dot_general(kwm, vh, (((0,), (0,)), ((), ())), preferred_element_type=F32)
            state_ref[hh] = gt_ref[hh] * r + u
            o = intra + inter
            cols = slice(RET_DV * hh, RET_DV * (hh + 1))
            o = (o * _rms(o)) * ng_ref[:, cols] * gate_ref[0, rows, cols].astype(F32)
            o_ref[0, rows, cols] = o.astype(BF16)
        return carry

    lax.fori_loop(0, n_blocks, block, 0, unroll=min(n_blocks, RET_UNROLL))
    rf_ref[0] = state_ref[...]


def _retention(rq, rk, rv, gate, r0, tabs, norm_g, *, t):
    nb, rows, _ = rq.shape
    d_tab, wq_tab, wk_tab, gt_tab = tabs
    n_pairs = RET_HEADS // 2
    seq = lambda w: pl.BlockSpec((1, rows, w), lambda b, p: (b, 0, p))
    return pl.pallas_call(
        functools.partial(_retention_kernel, t=t, n_blocks=rows // t),
        grid=(nb, n_pairs),
        in_specs=[seq(LANES), seq(LANES), seq(2 * RET_DV), seq(2 * RET_DV),
                  pl.BlockSpec((2, LANES, RET_DV), lambda b, p: (p, 0, 0)),
                  pl.BlockSpec((2, t, t), lambda b, p: (p, 0, 0)),
                  pl.BlockSpec((1, t, LANES), lambda b, p: (p, 0, 0)),
                  pl.BlockSpec((1, t, LANES), lambda b, p: (p, 0, 0)),
                  pl.BlockSpec((2, 1, RET_DV), lambda b, p: (p, 0, 0)),
                  pl.BlockSpec((1, 2 * RET_DV), lambda b, p: (0, p))],
        out_specs=(seq(2 * RET_DV), pl.BlockSpec((1, 2, LANES, RET_DV), lambda b, p: (b, p, 0, 0))),
        out_shape=(jax.ShapeDtypeStruct((nb, rows, RET_V), BF16),
                   jax.ShapeDtypeStruct((nb, RET_HEADS, LANES, RET_DV), F32)),
        scratch_shapes=[pltpu.VMEM((2, LANES, RET_DV), F32)],
        compiler_params=pltpu.CompilerParams(dimension_semantics=("arbitrary", "arbitrary"),
                                             vmem_limit_bytes=VMEM_LIMIT),
        name=f"retention_{rows}",
    )(rq, rk, rv, gate, r0, d_tab, wq_tab, wk_tab, gt_tab, norm_g)


def _fox_kernel(*refs, rows, tk, has_prefix):
    if has_prefix:
        q_ref, k_ref, v_ref, kp_ref, vp_ref, o_ref, m_ref, acc_ref = refs
    else:
        q_ref, k_ref, v_ref, o_ref, m_ref, acc_ref = refs
    lane = lax.broadcasted_iota(jnp.int32, (rows, LANES), 1)
    causal = lax.broadcasted_iota(jnp.int32, (tk, tk), 1) <= lax.broadcasted_iota(jnp.int32, (tk, tk), 0)

    def scores(hh, lo, k):
        return lax.dot_general(q_ref[0, hh, lo:rows, :], k, (((1,), (1,)), ((), ())), preferred_element_type=F32)

    def probs(s, m):
        return jnp.exp2((s - jnp.tile(m, (1, s.shape[1] // LANES))).astype(BF16))

    if has_prefix:
        for hh in range(2):
            s_pre = scores(hh, 0, kp_ref[0, hh])
            m = jnp.broadcast_to(jnp.max(s_pre, axis=1, keepdims=True), (rows, LANES))
            m_ref[hh] = m
            acc_ref[hh] = jnp.dot(probs(s_pre, m), vp_ref[0, hh], preferred_element_type=F32)
    else:
        m_ref[...] = jnp.full(m_ref.shape, NEG, F32)
        acc_ref[...] = jnp.zeros(acc_ref.shape, F32)

    for c in range(rows // tk):
        lo = c * tk
        for hh in range(2):
            s = scores(hh, lo, k_ref[0, hh, lo:lo + tk, :])
            s_top = jnp.where(causal, s[:tk], NEG)
            s = jnp.concatenate([s_top, s[tk:]], axis=0) if lo + tk < rows else s_top
            m_old = m_ref[hh, lo:rows]
            m_new = jnp.maximum(m_old, jnp.max(s, axis=1, keepdims=True))
            m_ref[hh, lo:rows] = m_new
            pv = jnp.dot(probs(s, m_new), v_ref[0, hh, lo:lo + tk, :], preferred_element_type=F32)
            acc_ref[hh, lo:rows] = jnp.exp2(m_old - m_new) * acc_ref[hh, lo:rows] + pv

    outs = []
    for hh in range(2):
        acc = acc_ref[hh]
        sum_lane = FOX_DH if hh == 0 else 0
        outs.append(acc / acc[:, sum_lane:sum_lane + 1])
    o_ref[0] = jnp.where(lane < FOX_DH, outs[0], outs[1]).astype(BF16)


def _fox(fq, fk, fv, kp, vp, *, tk):
    nb, _, rows, _ = fq.shape
    has_prefix = kp is not None
    pair_blk = pl.BlockSpec((1, 2, rows, LANES), lambda b, p: (b, p, 0, 0))
    in_specs = [pair_blk, pair_blk, pair_blk]
    args = [fq, fk, fv]
    if has_prefix:
        pre_blk = pl.BlockSpec((1, 2, PREFIX, LANES), lambda b, p: (0, p, 0, 0))
        in_specs += [pre_blk, pre_blk]
        args += [kp, vp]
    return pl.pallas_call(
        functools.partial(_fox_kernel, rows=rows, tk=tk, has_prefix=has_prefix),
        grid=(nb, FOX_HEADS // 2),
        in_specs=in_specs,
        out_specs=pl.BlockSpec((1, rows, LANES), lambda b, p: (b, 0, p)),
        out_shape=jax.ShapeDtypeStruct((nb, rows, FOX_W), BF16),
        scratch_shapes=[pltpu.VMEM((2, rows, LANES), F32), pltpu.VMEM((2, rows, LANES), F32)],
        compiler_params=pltpu.CompilerParams(dimension_semantics=("arbitrary", "arbitrary"),
                                             vmem_limit_bytes=VMEM_LIMIT),
        name=f"fox_{rows}",
    )(*args)


def _mixer_residual(x_ref, or_ref, of_ref, wo_ref):
    y = jnp.dot(or_ref[0], wo_ref[0:RET_V, :], preferred_element_type=F32)
    y = y + jnp.dot(of_ref[0], wo_ref[RET_V:RET_V + FOX_W, :], preferred_element_type=F32)
    return x_ref[0] + y


def _ffn_prefix_kernel(x_ref, or_ref, of_ref, wo_ref, g2_ref, wa_ref, a_ref, *, n_invalid):
    h1 = _mixer_residual(x_ref, or_ref, of_ref, wo_ref)
    hn = ((h1 * _rms(h1)) * g2_ref[...]).astype(BF16)
    a = jnp.dot(hn, wa_ref[...], preferred_element_type=F32)
    row = lax.broadcasted_iota(jnp.int32, a.shape, 0)
    a_ref[...] = jnp.where(row >= n_invalid, a, 0.0)


def _ffn_prefix(x, o_r, o_f, w_out, g2, w_up):
    full = lambda shape: pl.BlockSpec(shape, lambda i: (0,) * len(shape))
    return pl.pallas_call(
        functools.partial(_ffn_prefix_kernel, n_invalid=N_PAD),
        grid=(1,),
        in_specs=[full((1, PREFIX, D_MODEL)), full((1, PREFIX, RET_V)), full((1, PREFIX, FOX_W)),
                  full((D_MODEL, D_MODEL)), full((1, D_MODEL)), full((D_MODEL, D_FF))],
        out_specs=full((PREFIX, D_FF)),
        out_shape=jax.ShapeDtypeStruct((PREFIX, D_FF), F32),
        compiler_params=pltpu.CompilerParams(vmem_limit_bytes=VMEM_LIMIT),
        name="ffn_prefix",
    )(x, o_r, o_f, w_out, g2, w_up)


def _ffn_kernel(x_ref, or_ref, of_ref, wo_ref, g2_ref, wu_ref, cw_ref, cb_ref, wd_ref, gf_ref, halo_ref,
                o_ref, carry_ref, abuf_ref, gated_ref, *, tm):
    @pl.when(pl.program_id(1) == 0)
    def _():
        carry_ref[...] = halo_ref[...]

    h1 = _mixer_residual(x_ref, or_ref, of_ref, wo_ref)
    hn = ((h1 * _rms(h1)) * g2_ref[...]).astype(BF16)
    for c in range(D_FF // FF_CHUNK):
        cols = slice(c * FF_CHUNK, (c + 1) * FF_CHUNK)
        a = jnp.dot(hn, wu_ref[:, cols], preferred_element_type=F32)
        b = jnp.dot(hn, wu_ref[:, D_FF + c * FF_CHUNK:D_FF + (c + 1) * FF_CHUNK], preferred_element_type=F32)
        abuf_ref[0:SUBLANES, :] = carry_ref[:, cols]
        abuf_ref[SUBLANES:SUBLANES + tm, :] = a
        carry_ref[:, cols] = a[tm - SUBLANES:tm, :]
        acc = cb_ref[:, cols] + abuf_ref[SUBLANES - 2:SUBLANES - 2 + tm, :] * cw_ref[0:1, cols]
        acc = acc + abuf_ref[SUBLANES - 1:SUBLANES - 1 + tm, :] * cw_ref[1:2, cols]
        acc = acc + a * cw_ref[2:3, cols]
        gated_ref[:, cols] = (acc * jax.nn.sigmoid(acc) * b).astype(BF16)
    h2 = h1 + jnp.dot(gated_ref[...], wd_ref[...], preferred_element_type=F32)
    o_ref[0] = (h2 * _rms(h2)) * gf_ref[...]


def _ffn(x, o_r, o_f, w_out, g2, w_up, conv_w, conv_b, w_down, gf, halo, *, tm):
    nb, rows, _ = x.shape
    row_blk = lambda w: pl.BlockSpec((1, tm, w), lambda b, j: (b, j, 0))
    const = lambda shape: pl.BlockSpec(shape, lambda b, j: (0,) * len(shape), pipeline_mode=pl.Buffered(1))
    return pl.pallas_call(
        functools.partial(_ffn_kernel, tm=tm),
        grid=(nb, rows // tm),
        in_specs=[row_blk(D_MODEL), row_blk(RET_V), row_blk(FOX_W), const((D_MODEL, D_MODEL)),
                  const((1, D_MODEL)), const((D_MODEL, 2 * D_FF)), const((CONV_W, D_FF)), const((1, D_FF)),
                  const((D_FF, D_MODEL)), const((1, D_MODEL)), const((SUBLANES, D_FF))],
        out_specs=row_blk(D_MODEL),
        out_shape=jax.ShapeDtypeStruct((nb, rows, D_MODEL), F32),
        scratch_shapes=[pltpu.VMEM((SUBLANES, D_FF), F32), pltpu.VMEM((SUBLANES + tm, FF_CHUNK), F32),
                        pltpu.VMEM((tm, D_FF), BF16)],
        compiler_params=pltpu.CompilerParams(dimension_semantics=("arbitrary", "arbitrary"),
                                             vmem_limit_bytes=VMEM_LIMIT),
        name="ffn",
    )(x, o_r, o_f, w_out, g2, w_up, conv_w, conv_b, w_down, gf, halo)


def _rotary_tables(start, length):
    half = RET_DK // 2
    inv = (1.0 / (ROPE_BASE ** (np.arange(half, dtype=np.float32) / np.float32(half)))).astype(np.float32)
    pos = start + np.arange(length)
    ang = pos.astype(np.float32)[:, None] * inv[None, :]
    cos = np.tile(np.cos(ang), (1, LANES // half))
    sin = np.tile(np.concatenate([-np.sin(ang), np.sin(ang)], axis=1), (1, LANES // RET_DK))
    k_scale = (np.float32(RET_DK ** -0.5) * (pos >= N_PAD).astype(np.float32))[:, None]
    return cos, sin, cos * k_scale, sin * k_scale


def _retention_tables(t):
    f32 = np.float32
    log_g = np.log1p(-np.exp2(f32(-5.0) - np.arange(RET_HEADS, dtype=f32)))
    n = np.arange(t, dtype=f32)
    visible = (np.arange(t)[None, :] // CHUNK) <= (np.arange(t)[:, None] // CHUNK)
    d = np.exp(np.abs(n[:, None] - n[None, :])[None] * log_g[:, None, None]) * visible[None].astype(f32)
    lane_log_g = np.repeat(log_g, RET_DK).reshape(RET_HEADS // 2, 1, LANES)
    wq = np.exp((n + f32(1.0))[None, :, None] * lane_log_g)
    wk = np.exp((f32(t) - f32(1.0) - n)[None, :, None] * lane_log_g)
    gt = np.broadcast_to(np.exp(f32(t) * log_g)[:, None, None], (RET_HEADS, 1, RET_DV))
    return tuple(np.ascontiguousarray(a, dtype=f32) for a in (d, wq, wk, gt))


def _gate_lanes(v):
    spread = jnp.stack([v] + [jnp.zeros_like(v)] * (GATE_STRIDE - 1), axis=-1)
    spread = spread.reshape(v.shape[:-1] + (GATE_STRIDE * FOX_HEADS,))
    return jnp.pad(spread, [(0, 0)] * (v.ndim - 1) + [(0, LANES - GATE_STRIDE * FOX_HEADS)])


def kernel(x, meta_tokens, attn_norm_g, w_in, fox_forget_b, ret_norm_g, w_out, ffn_norm_g, w_up, conv_w,
           conv_b, w_down, final_norm_g):
    assert w_in.shape[0] == 1, "single-layer block"
    batch, seq, _ = x.shape
    w_main = w_in[0].astype(BF16)
    w_ff = _gate_lanes(w_in[0, :, MAIN_W:]).astype(BF16)
    fb = _gate_lanes(fox_forget_b[0]).reshape(1, LANES)
    g1 = attn_norm_g[0].reshape(1, D_MODEL)
    g2 = ffn_norm_g[0].reshape(1, D_MODEL)
    gf = final_norm_g.reshape(1, D_MODEL)
    ng = ret_norm_g[0].reshape(1, RET_V)
    cb = conv_b[0].reshape(1, D_FF)
    h_pre = jnp.concatenate([jnp.zeros((N_PAD, D_MODEL), x.dtype), meta_tokens.astype(x.dtype)], axis=0)[None]

    zeros_c = jnp.zeros((SUBLANES, LANES), F32)
    rq0, rk0, rv0, rg0, fq0, fk0, fv0, c_pre = _inproj(h_pre, g1, w_main, w_ff, fb, _rotary_tables(0, PREFIX),
                                                        zeros_c, tl=PREFIX, n_invalid=N_PAD)
    r_zero = jnp.zeros((RET_HEADS, LANES, RET_DV), F32)
    or0, r_pre = _retention(rq0, rk0, rv0, rg0, r_zero, _retention_tables(PREFIX), ng, t=PREFIX)
    of0 = _fox(fq0, fk0, fv0, None, None, tk=PREFIX)

    rq, rk, rv, rg, fq, fk, fv, _, w_out_b, w_up_b, w_down_b = _inproj(
        x, g1, w_main, w_ff, fb, _rotary_tables(PREFIX, seq), c_pre[0], (w_out[0], w_up[0], w_down[0]),
        tl=1024, n_invalid=0)
    a_pre = _ffn_prefix(h_pre, or0, of0, w_out_b, g2, w_up_b)
    halo = a_pre[PREFIX - SUBLANES:]
    o_r, _ = _retention(rq, rk, rv, rg, r_pre[0], _retention_tables(256), ng, t=256)
    o_f = _fox(fq, fk, fv, fk0, fv0, tk=512)
    return _ffn(x, o_r, o_f, w_out_b, g2, w_up_b, conv_w[0], cb, w_down_b, gf, halo, tm=1024)
```
